```python
import math
import jax, jax.numpy as jnp
from jax import lax
import numpy as np

D_MODEL = 2048
BATCH = 2
SEQ = 4096
DEPTH = 4

CHUNK = 64
N_META = 16
META_PAD = (CHUNK - N_META % CHUNK) % CHUNK
SSD_D_INNER = D_MODEL
SSD_HEAD_DIM = 64
SSD_N_HEADS = SSD_D_INNER // SSD_HEAD_DIM
SSD_N_GROUPS = 4
SSD_D_STATE = 128
SSD_CONV = 4
SSD_XBC = SSD_D_INNER + 2 * SSD_N_GROUPS * SSD_D_STATE
POOL_WIDTH = D_MODEL // 2
POOL_WINDOWS = (2, 4, 8, 16)
POOL_GROUPS = len(POOL_WINDOWS)
POOL_GROUP_DIM = POOL_WIDTH // POOL_GROUPS
CONF_WIDTH = D_MODEL // 2
CONF_KERNEL = 31
N_BRANCHES = 3
SPLIT_IDX = (SSD_D_INNER,
             SSD_D_INNER + SSD_XBC,
             SSD_D_INNER + SSD_XBC + SSD_N_HEADS,
             SSD_D_INNER + SSD_XBC + SSD_N_HEADS + POOL_WIDTH)
IN_COLS = SSD_D_INNER + SSD_XBC + SSD_N_HEADS + POOL_WIDTH + 2 * CONF_WIDTH
N_EXPERT_GROUPS = 4
EXPERTS_PER_GROUP = 8
N_EXPERTS = N_EXPERT_GROUPS * EXPERTS_PER_GROUP
TOP_K = 2
EXPERT_HIDDEN = (3 * D_MODEL) // 8
MOE_BLOCK = 128
DEEPNORM_ALPHA = (2.0 * DEPTH) ** 0.25
DEEPNORM_BETA = (8.0 * DEPTH) ** -0.25
LN_EPS = 1e-5

kernel_name = "hybrid_ssd_pool_conformer_hmoe_deepnorm"


def layer_norm(x, g, b):
    xf = x.astype(jnp.float32)
    mu = jnp.mean(xf, -1, keepdims=True)
    var = jnp.mean(jnp.square(xf - mu), -1, keepdims=True)
    return ((xf - mu) * lax.rsqrt(var + LN_EPS) * g + b).astype(x.dtype)


def causal_dwconv(u, w, b):
    k, c = w.shape
    y = lax.conv_general_dilated(u, w[:, None, :].astype(u.dtype), window_strides=(1,),
                                 padding=((k - 1, 0),),
                                 dimension_numbers=('NWC', 'WIO', 'NWC'),
                                 feature_group_count=c)
    return y + b.astype(u.dtype)


def segsum_exp(a):
    l = a.shape[-1]
    cs = jnp.cumsum(a, -1)
    diff = cs[..., :, None] - cs[..., None, :]
    mask = jnp.tril(jnp.ones((l, l), dtype=bool))
    return jnp.exp(jnp.where(mask, diff, -jnp.inf))


def ssd_scan(xh, dt, a, bm, cm):
    bsz, l, h, p = xh.shape
    g, n = bm.shape[-2:]
    r = h // g
    nc = l // CHUNK
    xdt = (xh * dt[..., None]).reshape(bsz, nc, CHUNK, g, r, p)
    adt = jnp.transpose((dt * a).reshape(bsz, nc, CHUNK, g, r), (0, 3, 4, 1, 2))
    bc = bm.reshape(bsz, nc, CHUNK, g, n)
    cc = cm.reshape(bsz, nc, CHUNK, g, n)
    a_cs = jnp.cumsum(adt, -1)
    cb = jnp.einsum('bclgn,bcsgn->bcgls', cc, bc)
    y_diag = jnp.einsum('bcgls,bgrcls,bcsgrp->bclgrp', cb, segsum_exp(adt), xdt)
    decay_to_end = jnp.exp(a_cs[..., -1:] - a_cs)
    chunk_states = jnp.einsum('bcsgn,bgrcs,bcsgrp->cbgrpn', bc, decay_to_end, xdt)
    chunk_decay = jnp.moveaxis(jnp.exp(a_cs[..., -1]), -1, 0)

    def step(s, inp):
        st, dec = inp
        return s * dec[..., None, None] + st, s

    s0 = jnp.zeros(chunk_states.shape[1:], chunk_states.dtype)
    _, prev = lax.scan(step, s0, (chunk_states, chunk_decay))
    y_off = jnp.einsum('bclgn,cbgrpn,bgrcl->bclgrp', cc, prev, jnp.exp(a_cs))
    return (y_diag + y_off).reshape(bsz, l, h, p)


def ssd_branch(z, xbc, dt_raw, conv_w, conv_b, dt_bias, a_log, d_skip, norm_g):
    bsz, l, _ = xbc.shape
    f32 = jnp.float32
    xbc = jax.nn.silu(causal_dwconv(xbc, conv_w, conv_b))
    xs, bm, cm = jnp.split(xbc, [SSD_D_INNER, SSD_D_INNER + SSD_N_GROUPS * SSD_D_STATE], -1)
    xh = xs.astype(f32).reshape(bsz, l, SSD_N_HEADS, SSD_HEAD_DIM)
    bm = bm.astype(f32).reshape(bsz, l, SSD_N_GROUPS, SSD_D_STATE)
    cm = cm.astype(f32).reshape(bsz, l, SSD_N_GROUPS, SSD_D_STATE)
    dt = jax.nn.softplus(dt_raw.astype(f32) + dt_bias.astype(f32))
    a = -jnp.exp(a_log.astype(f32))
    pad4 = ((0, 0), (META_PAD, 0), (0, 0), (0, 0))
    y = ssd_scan(jnp.pad(xh, pad4), jnp.pad(dt, ((0, 0), (META_PAD, 0), (0, 0))), a,
                 jnp.pad(bm, pad4), jnp.pad(cm, pad4))[:, META_PAD:]
    y = (y + xh * d_skip.astype(f32)[:, None]).reshape(bsz, l, SSD_D_INNER)
    y = y * jax.nn.silu(z.astype(f32))
    yg = y.reshape(bsz, l, SSD_N_GROUPS, SSD_D_INNER // SSD_N_GROUPS)
    yg = yg * lax.rsqrt(jnp.mean(jnp.square(yg), -1, keepdims=True) + LN_EPS)
    return (yg.reshape(bsz, l, SSD_D_INNER) * norm_g).astype(z.dtype)


def pool_branch(u, w_grp, scale):
    bsz, l, _ = u.shape
    uf = u.astype(jnp.float32).reshape(bsz, l, POOL_GROUPS, POOL_GROUP_DIM)
    pos = jnp.arange(l, dtype=jnp.float32)
    outs = []
    for gi, win in enumerate(POOL_WINDOWS):
        ug = uf[:, :, gi]
        cs = jnp.cumsum(ug, axis=1)
        lagged = jnp.pad(cs, ((0, 0), (win, 0), (0, 0)))[:, :l]
        count = jnp.minimum(pos + 1.0, float(win))[None, :, None]
        outs.append((cs - lagged) / count - ug)
    mixed = jnp.stack(outs, 2)
    mixed = jnp.einsum('blgc,gcd->blgd', mixed, w_grp.astype(jnp.float32))
    return (mixed.reshape(bsz, l, POOL_WIDTH) * scale).astype(u.dtype)


def conformer_branch(u, dw_w, dw_b, ln_g, ln_b):
    a, gate = jnp.split(u, 2, -1)
    v = a * jax.nn.sigmoid(gate)
    v = causal_dwconv(v, dw_w, dw_b)
    v = layer_norm(v, ln_g, ln_b)
    return jax.nn.silu(v)


def hier_moe(h, rg_w, rg_b, re_w, re_b, w_gate, w_up, w_down):
    bsz, l, d = h.shape
    t = bsz * l
    hf = h.reshape(t, d)
    g_logits = (hf @ rg_w + rg_b).astype(jnp.float32)
    g_prob = jax.nn.softmax(g_logits, -1)
    g_sel = jnp.argmax(g_logits, -1).astype(jnp.int32)
    e_logits = (hf @ re_w + re_b).astype(jnp.float32).reshape(t, N_EXPERT_GROUPS, EXPERTS_PER_GROUP)
    e_in = jnp.take_along_axis(e_logits, g_sel[:, None, None], 1)[:, 0]
    top_p, top_i = lax.top_k(jax.nn.softmax(e_in, -1), TOP_K)
    p_group = jnp.take_along_axis(g_prob, g_sel[:, None], 1)
    combine = p_group * top_p / jnp.sum(top_p, -1, keepdims=True)
    eid = (g_sel[:, None] * EXPERTS_PER_GROUP + top_i).reshape(-1).astype(jnp.int32)
    tok = jnp.repeat(jnp.arange(t, dtype=jnp.int32), TOP_K)
    wgt = combine.reshape(-1)
    n_assign = t * TOP_K
    order = jnp.argsort(eid)
    eid_s, tok_s, wgt_s = eid[order], tok[order], wgt[order]
    counts = jnp.bincount(eid, length=N_EXPERTS)
    start = jnp.cumsum(counts) - counts
    padded = (counts + MOE_BLOCK - 1) // MOE_BLOCK * MOE_BLOCK
    pend = jnp.cumsum(padded)
    pstart = pend - padded
    slot = pstart[eid_s] + (jnp.arange(n_assign, dtype=jnp.int32) - start[eid_s])
    n_blocks = -(-n_assign // MOE_BLOCK) + N_EXPERTS
    n_slots = n_blocks * MOE_BLOCK
    slot_tok = jnp.zeros((n_slots,), jnp.int32).at[slot].set(tok_s)
    slot_w = jnp.zeros((n_slots,), jnp.float32).at[slot].set(wgt_s)
    block_expert = jnp.minimum(
        jnp.searchsorted(pend, jnp.arange(n_blocks) * MOE_BLOCK, side='right'), N_EXPERTS - 1)

    def block_fn(args):
        idx, e = args
        xb = hf[idx]
        hid = jax.nn.silu(xb @ w_gate[e]) * (xb @ w_up[e])
        return hid @ w_down[e]

    out = lax.map(block_fn, (slot_tok.reshape(n_blocks, MOE_BLOCK), block_expert))
    out = (out.reshape(n_slots, d) * slot_w[:, None]).astype(h.dtype)
    y = jnp.zeros((t, d), h.dtype).at[slot_tok].add(out)
    return y.reshape(bsz, l, d)


def setup_inputs(seed: int = 0) -> dict:
    key = jax.random.key(seed)
    ks = iter(jax.random.split(key, 64))
    f32 = jnp.float32

    def nrm(shape, scale=1.0):
        return jax.random.normal(next(ks), shape, f32) * scale

    def gain(shape):
        return 1.0 + nrm(shape, 0.02)

    L = DEPTH
    D = D_MODEL
    dt0 = jnp.exp(jax.random.uniform(next(ks), (L, SSD_N_HEADS), f32, math.log(1e-3), math.log(1e-1)))
    inp = {
        "x": nrm((BATCH, SEQ, D)),
        "meta_tokens": nrm((N_META, D)),
        "ln_emb_g": gain((D,)),
        "ln_emb_b": nrm((D,), 0.02),
        "w_in": nrm((L, D, IN_COLS), D ** -0.5),
        "ssd_conv_w": nrm((L, SSD_CONV, SSD_XBC), SSD_CONV ** -0.5),
        "ssd_conv_b": nrm((L, SSD_XBC), 0.02),
        "ssd_dt_bias": dt0 + jnp.log(-jnp.expm1(-dt0)),
        "ssd_a_log": jnp.log(jax.random.uniform(next(ks), (L, SSD_N_HEADS), f32, 1.0, 16.0)),
        "ssd_d": 1.0 + nrm((L, SSD_N_HEADS), 0.1),
        "ssd_norm_g": gain((L, SSD_D_INNER)),
        "ssd_proj": nrm((L, SSD_D_INNER, D), SSD_D_INNER ** -0.5 * DEEPNORM_BETA),
        "pool_w": nrm((L, POOL_GROUPS, POOL_GROUP_DIM, POOL_GROUP_DIM), POOL_GROUP_DIM ** -0.5),
        "pool_scale": gain((L, POOL_WIDTH)),
        "pool_proj": nrm((L, POOL_WIDTH, D), POOL_WIDTH ** -0.5 * DEEPNORM_BETA),
        "conf_dw_w": nrm((L, CONF_KERNEL, CONF_WIDTH), CONF_KERNEL ** -0.5),
        "conf_dw_b": nrm((L, CONF_WIDTH), 0.02),
        "conf_ln_g": gain((L, CONF_WIDTH)),
        "conf_ln_b": nrm((L, CONF_WIDTH), 0.02),
        "conf_proj": nrm((L, CONF_WIDTH, D), CONF_WIDTH ** -0.5 * DEEPNORM_BETA),
        "gate_w": nrm((L, D, N_BRANCHES * D), D ** -0.5),
        "gate_b": nrm((L, N_BRANCHES * D), 0.02),
        "w_out": nrm((L, D, D), D ** -0.5 * DEEPNORM_BETA),
        "ln1_g": gain((L, D)),
        "ln1_b": nrm((L, D), 0.02),
        "router_group_w": nrm((L, D, N_EXPERT_GROUPS), D ** -0.5),
        "router_group_b": nrm((L, N_EXPERT_GROUPS), 0.01),
        "router_expert_w": nrm((L, D, N_EXPERTS), D ** -0.5),
        "router_expert_b": nrm((L, N_EXPERTS), 0.01),
        "exp_w_gate": nrm((L, N_EXPERTS, D, EXPERT_HIDDEN), D ** -0.5),
        "exp_w_up": nrm((L, N_EXPERTS, D, EXPERT_HIDDEN), D ** -0.5),
        "exp_w_down": nrm((L, N_EXPERTS, EXPERT_HIDDEN, D), EXPERT_HIDDEN ** -0.5 * DEEPNORM_BETA),
        "ln2_g": gain((L, D)),
        "ln2_b": nrm((L, D), 0.02),
    }
    return inp


def reference(x, meta_tokens, ln_emb_g, ln_emb_b, w_in, ssd_conv_w, ssd_conv_b, ssd_dt_bias,
              ssd_a_log, ssd_d, ssd_norm_g, ssd_proj, pool_w, pool_scale, pool_proj,
              conf_dw_w, conf_dw_b, conf_ln_g, conf_ln_b, conf_proj, gate_w, gate_b, w_out,
              ln1_g, ln1_b, router_group_w, router_group_b, router_expert_w, router_expert_b,
              exp_w_gate, exp_w_up, exp_w_down, ln2_g, ln2_b):
    bsz = x.shape[0]
    meta = jnp.broadcast_to(meta_tokens[None].astype(x.dtype), (bsz, N_META, D_MODEL))
    h = layer_norm(jnp.concatenate([meta, x], axis=1), ln_emb_g, ln_emb_b)
    l = h.shape[1]
    for i in range(DEPTH):
        proj = h @ w_in[i]
        z, xbc, dt_raw, u_pool, u_conf = jnp.split(proj, SPLIT_IDX, axis=-1)
        y_ssd = ssd_branch(z, xbc, dt_raw, ssd_conv_w[i], ssd_conv_b[i], ssd_dt_bias[i],
                           ssd_a_log[i], ssd_d[i], ssd_norm_g[i]) @ ssd_proj[i]
        y_pool = pool_branch(u_pool, pool_w[i], pool_scale[i]) @ pool_proj[i]
        y_conf = conformer_branch(u_conf, conf_dw_w[i], conf_dw_b[i], conf_ln_g[i],
                                  conf_ln_b[i]) @ conf_proj[i]
        gates = jax.nn.sigmoid(h @ gate_w[i] + gate_b[i]).reshape(bsz, l, N_BRANCHES, D_MODEL)
        merged = gates[:, :, 0] * y_ssd + gates[:, :, 1] * y_pool + gates[:, :, 2] * y_conf
        h = layer_norm(DEEPNORM_ALPHA * h + merged @ w_out[i], ln1_g[i], ln1_b[i])
        y_moe = hier_moe(h, router_group_w[i], router_group_b[i], router_expert_w[i],
                         router_expert_b[i], exp_w_gate[i], exp_w_up[i], exp_w_down[i])
        h = layer_norm(DEEPNORM_ALPHA * h + y_moe, ln2_g[i], ln2_b[i])
    return h[:, N_META:]
```

```python
import functools

import jax
import jax.numpy as jnp
from jax import lax
from jax.experimental import pallas as pl
from jax.experimental.pallas import tpu as pltpu

F32 = jnp.float32
BF16 = jnp.bfloat16

PAD_ROWS = 112
SSD_CHUNK = 128
SSD_GROUPS = 4
SSD_HEAD_DIM = 64
POOL_WINDOWS = (2, 4, 8, 16)
MOE_BLOCK = 256
LN_EPS = 1e-5
LANES = 128
NEG_BIG = -1e30


def _cparams(sem, vmem_mb):
    return pltpu.CompilerParams(dimension_semantics=sem, vmem_limit_bytes=vmem_mb << 20)


def _sigmoid(x):
    return 1.0 / (1.0 + jnp.exp(-x))


def _softplus(x):
    return jnp.maximum(x, 0.0) + jnp.log1p(jnp.exp(-jnp.abs(x)))


def _ln(x, g, b):
    mu = jnp.mean(x, -1, keepdims=True)
    xc = x - mu
    var = jnp.mean(xc * xc, -1, keepdims=True)
    return xc * lax.rsqrt(var + LN_EPS) * g + b


def _rows_in_batch(blk, bm, lp):
    base = lax.rem(blk, lp // bm) * bm
    return base + lax.broadcasted_iota(jnp.int32, (bm, 1), 0)


def _pick(n, cands):
    for c in cands:
        if n % c == 0:
            return c
    raise ValueError(f"no block size for {n} among {cands}")


def _embed_kernel(x_ref, g_ref, b_ref, h_ref, hb_ref, *, bm, lp):
    y = _ln(x_ref[...], g_ref[...], b_ref[...])
    y = jnp.where(_rows_in_batch(pl.program_id(0), bm, lp) >= PAD_ROWS, y, 0.0)
    h_ref[...] = y
    hb_ref[...] = y.astype(BF16)


def _embed_ln(xcat, g, b, lp):
    t, d = xcat.shape
    bm = _pick(lp, (528, 384, 192, 128, 64, 8))
    row = pl.BlockSpec((bm, d), lambda i: (i, 0))
    vec = pl.BlockSpec((1, d), lambda i: (0, 0))
    return pl.pallas_call(
        functools.partial(_embed_kernel, bm=bm, lp=lp),
        grid=(t // bm,),
        in_specs=[row, vec, vec],
        out_specs=[row, row],
        out_shape=[jax.ShapeDtypeStruct((t, d), F32), jax.ShapeDtypeStruct((t, d), BF16)],
        compiler_params=_cparams(("parallel",), 40),
        name="embed_ln",
    )(xcat, g.reshape(1, d), b.reshape(1, d))


def _add_ln_kernel(h_ref, y_ref, g_ref, b_ref, o_ref, ob_ref, *, bm, lp, alpha):
    y = _ln(alpha * h_ref[...] + y_ref[...], g_ref[...], b_ref[...])
    y = jnp.where(_rows_in_batch(pl.program_id(0), bm, lp) >= PAD_ROWS, y, 0.0)
    o_ref[...] = y
    ob_ref[...] = y.astype(BF16)


def _add_ln(h, y, g, b, lp, alpha):
    t, d = h.shape
    bm = _pick(lp, (528, 384, 192, 128, 64, 8))
    row = pl.BlockSpec((bm, d), lambda i: (i, 0))
    vec = pl.BlockSpec((1, d), lambda i: (0, 0))
    return pl.pallas_call(
        functools.partial(_add_ln_kernel, bm=bm, lp=lp, alpha=alpha),
        grid=(t // bm,),
        in_specs=[row, row, vec, vec],
        out_specs=[row, row],
        out_shape=[jax.ShapeDtypeStruct((t, d), F32), jax.ShapeDtypeStruct((t, d), BF16)],
        compiler_params=_cparams(("parallel",), 48),
        name="add_ln",
    )(h, y, g.reshape(1, d), b.reshape(1, d))


def _mm_kernel(x_ref, w_ref, o_ref, wbf_ref):
    @pl.when(pl.program_id(1) == 0)
    def _():
        wbf_ref[...] = w_ref[...].astype(BF16)

    o_ref[...] = jnp.dot(x_ref[...], wbf_ref[...], preferred_element_type=F32).astype(o_ref.dtype)


def _mm(x, w, layer, col0, n_cols, out_dtype=F32):
    t, k = x.shape
    bn = _pick(n_cols, (512, 256, 128))
    bm = _pick(t, (1056, 528, 384, 192, 128, 64, 8))
    assert col0 % bn == 0
    cb0 = col0 // bn
    return pl.pallas_call(
        _mm_kernel,
        grid=(n_cols // bn, t // bm),
        in_specs=[pl.BlockSpec((bm, k), lambda n, m: (m, 0)),
                  pl.BlockSpec((None, k, bn), lambda n, m: (layer, 0, cb0 + n))],
        out_specs=pl.BlockSpec((bm, bn), lambda n, m: (m, n)),
        out_shape=jax.ShapeDtypeStruct((t, n_cols), out_dtype),
        scratch_shapes=[pltpu.VMEM((k, bn), BF16)],
        compiler_params=_cparams(("arbitrary", "arbitrary"), 40),
        name="mm",
    )(x, w)


def _ssd_kernel(z_ref, xbc_ref, dtr_ref, cw_ref, cb_ref, dtb_ref, alog_ref, dexp_ref, ng_ref, o_ref,
                cbuf, xc, ybuf, st, *, q, d_inner, n_state):
    c = pl.program_id(1)
    xbc_w = cbuf.shape[1]
    kc = cw_ref.shape[0]
    hist = 8

    @pl.when(c == 0)
    def _():
        cbuf[0:hist, :] = jnp.zeros((hist, xbc_w), F32)
        st[...] = jnp.zeros(st.shape, F32)

    cbuf[hist:hist + q, :] = xbc_ref[...]
    cw = 512
    for j in range(xbc_w // cw):
        sl = slice(j * cw, (j + 1) * cw)
        conv = cb_ref[:, sl] + cw_ref[kc - 1:kc, sl] * cbuf[hist:hist + q, sl]
        for k in range(kc - 1):
            off = hist - (kc - 1) + k
            conv = conv + cw_ref[k:k + 1, sl] * cbuf[off:off + q, sl]
        xc[:, sl] = conv * _sigmoid(conv)
    cbuf[0:hist, :] = cbuf[q:q + hist, :]

    row = c * q + lax.broadcasted_iota(jnp.int32, (q, 1), 0)
    dt = _softplus(dtr_ref[...] + dtb_ref[...])
    dt = jnp.where(row >= PAD_ROWS, dt, 0.0)
    adt = dt * (-jnp.exp(alog_ref[...]))
    li = lax.broadcasted_iota(jnp.int32, (q, q), 0)
    si = lax.broadcasted_iota(jnp.int32, (q, q), 1)
    tri = li >= si
    acs = jnp.dot(tri.astype(F32), adt, preferred_element_type=F32,
                  precision=lax.Precision.HIGHEST)
    acs_t = acs.T
    dt_t = dt.T
    last = acs_t[:, q - 1:q]
    w_t = jnp.exp(last - acs_t) * dt_t
    dlast = jnp.exp(last)
    lane = lax.broadcasted_iota(jnp.int32, (1, LANES), 1)
    lo_half = lane < SSD_HEAD_DIM

    heads_per_group = d_inner // SSD_HEAD_DIM // SSD_GROUPS
    pairs_per_group = heads_per_group // 2
    for g in range(SSD_GROUPS):
        b0 = d_inner + g * n_state
        c0 = d_inner + SSD_GROUPS * n_state + g * n_state
        bg = xc[:, b0:b0 + n_state]
        cg = xc[:, c0:c0 + n_state]
        cbm = lax.dot_general(cg.astype(BF16), bg.astype(BF16), (((1,), (1,)), ((), ())),
                              preferred_element_type=F32)
        bg_t = bg.T
        for j in range(pairs_per_group):
            pair = g * pairs_per_group + j
            h0 = 2 * pair
            xs_pair = xc[:, h0 * SSD_HEAD_DIM:h0 * SSD_HEAD_DIM + LANES]
            st_pair = st[pair]
            lhs_y, lhs_s, rhs_x, rhs_st = [], [], [], []
            for u in range(2):
                h = h0 + u
                col = acs[:, h:h + 1]
                rw = acs_t[h:h + 1, :]
                decay = jnp.where(tri, jnp.exp(col - rw), 0.0)
                lhs_y.append((cbm * decay * dt_t[h:h + 1, :]).astype(BF16))
                lhs_s.append((bg_t * w_t[h:h + 1, :]).astype(BF16))
                keep = lo_half if u == 0 else jnp.logical_not(lo_half)
                rhs_x.append(jnp.where(keep, xs_pair, 0.0).astype(BF16))
                rhs_st.append(jnp.where(keep, st_pair, 0.0).astype(BF16))
            for u in range(2):
                col = acs[:, h0 + u:h0 + u + 1]
                lhs_y.append((jnp.exp(col) * cg).astype(BF16))
            y_pair = jnp.dot(jnp.concatenate(lhs_y, axis=1), jnp.concatenate(rhs_x + rhs_st, axis=0),
                             preferred_element_type=F32)
            s_new = jnp.dot(jnp.concatenate(lhs_s, axis=1), jnp.concatenate(rhs_x, axis=0),
                            preferred_element_type=F32)
            dpair = jnp.where(lo_half, dlast[h0:h0 + 1, :], dlast[h0 + 1:h0 + 2, :])
            st[pair] = st_pair * dpair + s_new
            ybuf[:, h0 * SSD_HEAD_DIM:h0 * SSD_HEAD_DIM + LANES] = y_pair

    gw = d_inner // SSD_GROUPS
    for g in range(SSD_GROUPS):
        sl = slice(g * gw, (g + 1) * gw)
        y = ybuf[:, sl] + xc[:, sl] * dexp_ref[:, sl]
        zz = z_ref[:, sl]
        y = y * (zz * _sigmoid(zz))
        y = y * lax.rsqrt(jnp.mean(y * y, -1, keepdims=True) + LN_EPS)
        o_ref[:, sl] = (y * ng_ref[:, sl]).astype(BF16)


def _ssd(z, xbc, dtr, conv_w, conv_b, dt_bias, a_log, d_skip, norm_g, nb, lp):
    t, d_inner = z.shape
    xbc_w = xbc.shape[1]
    q = SSD_CHUNK
    n_heads = d_inner // SSD_HEAD_DIM
    n_state = (xbc_w - d_inner) // (2 * SSD_GROUPS)
    assert n_state == LANES and n_heads % (2 * SSD_GROUPS) == 0 and lp % q == 0
    assert xbc_w % 512 == 0 and PAD_ROWS % 8 == 0
    nc = lp // q
    padh = LANES - n_heads
    row = lambda w: pl.BlockSpec((q, w), lambda b, c: (b * nc + c, 0))
    full = lambda a: pl.BlockSpec(a.shape, lambda b, c: (0,) * a.ndim)
    dtb = jnp.pad(dt_bias, (0, padh)).reshape(1, LANES)
    alog = jnp.pad(a_log, (0, padh)).reshape(1, LANES)
    dexp = jnp.repeat(d_skip, SSD_HEAD_DIM).reshape(1, d_inner)
    cb = conv_b.reshape(1, xbc_w)
    ng = norm_g.reshape(1, d_inner)
    return pl.pallas_call(
        functools.partial(_ssd_kernel, q=q, d_inner=d_inner, n_state=n_state),
        grid=(nb, nc),
        in_specs=[row(d_inner), row(xbc_w), row(LANES), full(conv_w), full(cb), full(dtb), full(alog),
                  full(dexp), full(ng)],
        out_specs=row(d_inner),
        out_shape=jax.ShapeDtypeStruct((t, d_inner), BF16),
        scratch_shapes=[pltpu.VMEM((8 + q, xbc_w), F32), pltpu.VMEM((q, xbc_w), F32),
                        pltpu.VMEM((q, d_inner), F32), pltpu.VMEM((n_heads // 2, n_state, LANES), F32)],
        compiler_params=_cparams(("arbitrary", "arbitrary"), 32),
        name="ssd",
    )(z, xbc, dtr, conv_w, cb, dtb, alog, dexp, ng)


def _pool_kernel(u_ref, w_ref, sc_ref, o_ref, buf, *, r, gd):
    c = pl.program_id(1)
    hist = 16
    width = buf.shape[1]

    @pl.when(c == 0)
    def _():
        buf[0:hist, :] = jnp.zeros((hist, width), F32)

    buf[hist:hist + r, :] = u_ref[...]
    pos = c * r + lax.broadcasted_iota(jnp.int32, (r, 1), 0) - PAD_ROWS
    for gi, win in enumerate(POOL_WINDOWS):
        sl = slice(gi * gd, (gi + 1) * gd)
        u = buf[hist:hist + r, sl]
        s = u
        for j in range(1, win):
            s = s + buf[hist - j:hist - j + r, sl]
        cnt = jnp.clip(pos + 1, 1, win).astype(F32)
        mixed = s / cnt - u
        y = jnp.dot(mixed.astype(BF16), w_ref[gi].astype(BF16), preferred_element_type=F32)
        o_ref[:, sl] = (y * sc_ref[:, sl]).astype(BF16)
    buf[0:hist, :] = buf[r:r + hist, :]


def _pool(pc, pool_w, scale, nb, lp):
    t = pc.shape[0]
    ng, gd, _ = pool_w.shape
    width = ng * gd
    assert ng == len(POOL_WINDOWS) and gd % LANES == 0
    r = _pick(lp, (384, 192, 128, 64))
    nc = lp // r
    return pl.pallas_call(
        functools.partial(_pool_kernel, r=r, gd=gd),
        grid=(nb, nc),
        in_specs=[pl.BlockSpec((r, width), lambda b, c: (b * nc + c, 0)),
                  pl.BlockSpec(pool_w.shape, lambda b, c: (0, 0, 0)),
                  pl.BlockSpec((1, width), lambda b, c: (0, 0))],
        out_specs=pl.BlockSpec((r, width), lambda b, c: (b * nc + c, 0)),
        out_shape=jax.ShapeDtypeStruct((t, width), BF16),
        scratch_shapes=[pltpu.VMEM((16 + r, width), F32)],
        compiler_params=_cparams(("arbitrary", "arbitrary"), 32),
        name="pool",
    )(pc, pool_w, scale.reshape(1, width))


def _conf_kernel(a_ref, g_ref, w_ref, b_ref, lg_ref, lb_ref, o_ref, vbuf, cv, *, r):
    c = pl.program_id(1)
    hist = 32
    width = vbuf.shape[1]
    kc = w_ref.shape[0]

    @pl.when(c == 0)
    def _():
        vbuf[0:hist, :] = jnp.zeros((hist, width), F32)

    vbuf[hist:hist + r, :] = a_ref[...] * _sigmoid(g_ref[...])
    cw = 256
    for j in range(width // cw):
        sl = slice(j * cw, (j + 1) * cw)
        acc = b_ref[:, sl] + w_ref[kc - 1:kc, sl] * vbuf[hist:hist + r, sl]
        for k in range(kc - 1):
            off = hist - (kc - 1) + k
            acc = acc + w_ref[k:k + 1, sl] * vbuf[off:off + r, sl]
        cv[:, sl] = acc
    vbuf[0:hist, :] = vbuf[r:r + hist, :]
    v = _ln(cv[...], lg_ref[...], lb_ref[...])
    o_ref[...] = (v * _sigmoid(v)).astype(BF16)


def _conf(pc, col_blk, dw_w, dw_b, ln_g, ln_b, nb, lp):
    t = pc.shape[0]
    kc, width = dw_w.shape
    assert kc <= 33 and width % 256 == 0
    r = _pick(lp, (192, 128, 64))
    nc = lp // r
    vec = lambda: pl.BlockSpec((1, width), lambda b, c: (0, 0))
    return pl.pallas_call(
        functools.partial(_conf_kernel, r=r),
        grid=(nb, nc),
        in_specs=[pl.BlockSpec((r, width), lambda b, c: (b * nc + c, col_blk)),
                  pl.BlockSpec((r, width), lambda b, c: (b * nc + c, col_blk + 1)),
                  pl.BlockSpec((kc, width), lambda b, c: (0, 0)), vec(), vec(), vec()],
        out_specs=pl.BlockSpec((r, width), lambda b, c: (b * nc + c, 0)),
        out_shape=jax.ShapeDtypeStruct((t, width), BF16),
        scratch_shapes=[pltpu.VMEM((32 + r, width), F32), pltpu.VMEM((r, width), F32)],
        compiler_params=_cparams(("arbitrary", "arbitrary"), 32),
        name="conf",
    )(pc, pc, dw_w, dw_b.reshape(1, width), ln_g.reshape(1, width), ln_b.reshape(1, width))


def _merge_kernel(hb_ref, f0_ref, f1_ref, f2_ref, g0_ref, g1_ref, g2_ref, gb_ref, p0_ref, p1_ref, p2_ref,
                  o_ref, gs, s0, s1, s2):
    @pl.when(pl.program_id(1) == 0)
    def _():
        gs[0] = g0_ref[...].astype(BF16)
        gs[1] = g1_ref[...].astype(BF16)
        gs[2] = g2_ref[...].astype(BF16)
        s0[...] = p0_ref[...].astype(BF16)
        s1[...] = p1_ref[...].astype(BF16)
        s2[...] = p2_ref[...].astype(BF16)

    hb = hb_ref[...]
    acc = None
    for j, (f_ref, s_ref) in enumerate(((f0_ref, s0), (f1_ref, s1), (f2_ref, s2))):
        gate = _sigmoid(jnp.dot(hb, gs[j], preferred_element_type=F32) + gb_ref[j:j + 1, :])
        term = gate * jnp.dot(f_ref[...], s_ref[...], preferred_element_type=F32)
        acc = term if acc is None else acc + term
    o_ref[...] = acc.astype(BF16)


def _merge(hb, f_ssd, f_pool, f_conf, gate_w, gate_b, ssd_proj, pool_proj, conf_proj, layer):
    t, d = hb.shape
    bn = _pick(d, (256, 128))
    bm = _pick(t, (528, 384, 192, 128, 64, 8))
    nbn = d // bn
    rowspec = lambda a: pl.BlockSpec((bm, a.shape[1]), lambda n, m: (m, 0))
    gspec = lambda j: pl.BlockSpec((None, d, bn), lambda n, m: (layer, 0, j * nbn + n))
    pspec = lambda w: pl.BlockSpec((None, w.shape[1], bn), lambda n, m: (layer, 0, n))
    gb = gate_b[layer].reshape(3, d)
    return pl.pallas_call(
        _merge_kernel,
        grid=(nbn, t // bm),
        in_specs=[rowspec(hb), rowspec(f_ssd), rowspec(f_pool), rowspec(f_conf),
                  gspec(0), gspec(1), gspec(2), pl.BlockSpec((3, bn), lambda n, m: (0, n)),
                  pspec(ssd_proj), pspec(pool_proj), pspec(conf_proj)],
        out_specs=pl.BlockSpec((bm, bn), lambda n, m: (m, n)),
        out_shape=jax.ShapeDtypeStruct((t, d), BF16),
        scratch_shapes=[pltpu.VMEM((3, d, bn), BF16), pltpu.VMEM((ssd_proj.shape[1], bn), BF16),
                        pltpu.VMEM((pool_proj.shape[1], bn), BF16), pltpu.VMEM((conf_proj.shape[1], bn), BF16)],
        compiler_params=_cparams(("arbitrary", "arbitrary"), 52),
        name="merge",
    )(hb, f_ssd, f_pool, f_conf, gate_w, gate_w, gate_w, gb, ssd_proj, pool_proj, conf_proj)


def _router_kernel(h_ref, w_ref, b_ref, o_ref, *, bm, lp, n_groups, per_group):
    logits = jnp.dot(h_ref[...], w_ref[...], preferred_element_type=F32,
                     precision=lax.Precision.HIGHEST) + b_ref[...]
    lane = lax.broadcasted_iota(jnp.int32, logits.shape, 1).astype(F32)

    def first_max(vals):
        m = jnp.max(vals, axis=1, keepdims=True)
        idx = jnp.min(jnp.where(vals == m, lane, float(LANES)), axis=1, keepdims=True)
        return m, idx

    gmask = lane < n_groups
    gmax, gsel = first_max(jnp.where(gmask, logits, NEG_BIG))
    gsum = jnp.sum(jnp.where(gmask, jnp.exp(logits - gmax), 0.0), axis=1, keepdims=True)
    p_group = 1.0 / gsum
    lo = n_groups + per_group * gsel
    el = jnp.where(jnp.logical_and(lane >= lo, lane < lo + per_group), logits, NEG_BIG)
    m1, i1 = first_max(el)
    m2, i2 = first_max(jnp.where(lane == i1, NEG_BIG, el))
    ratio = jnp.exp(m2 - m1)
    w1 = p_group / (1.0 + ratio)
    w2 = w1 * ratio
    real = _rows_in_batch(pl.program_id(0), bm, lp) >= PAD_ROWS
    sentinel = float(n_groups * per_group)
    e1 = jnp.where(real, i1 - n_groups, sentinel)
    e2 = jnp.where(real, i2 - n_groups, sentinel)
    w1 = jnp.where(real, w1, 0.0)
    w2 = jnp.where(real, w2, 0.0)
    out = jnp.where(lane == 0, w1, jnp.where(lane == 1, w2, jnp.where(lane == 2, e1, jnp.where(lane == 3, e2, 0.0))))
    o_ref[...] = out


def _router(h, rw, rb, lp, n_groups, per_group):
    t, d = h.shape
    bm = _pick(lp, (528, 384, 192, 128, 64, 8))
    return pl.pallas_call(
        functools.partial(_router_kernel, bm=bm, lp=lp, n_groups=n_groups, per_group=per_group),
        grid=(t // bm,),
        in_specs=[pl.BlockSpec((bm, d), lambda i: (i, 0)), pl.BlockSpec((d, LANES), lambda i: (0, 0)),
                  pl.BlockSpec((1, LANES), lambda i: (0, 0))],
        out_specs=pl.BlockSpec((bm, LANES), lambda i: (i, 0)),
        out_shape=jax.ShapeDtypeStruct((t, LANES), F32),
        compiler_params=_cparams(("parallel",), 32),
        name="router",
    )(h, rw, rb)


def _moe_kernel(be_ref, tot_ref, x_ref, wg_ref, wu_ref, wd_ref, o_ref, wg_s, wu_s, wd_s):
    i = pl.program_id(0)
    total = tot_ref[0]
    ic = jnp.minimum(i, total - 1)
    e = be_ref[ic]
    e_prev = be_ref[jnp.maximum(ic - 1, 0)]
    active = i < total

    @pl.when(jnp.logical_and(active, jnp.logical_or(i == 0, e != e_prev)))
    def _():
        wg_s[...] = wg_ref[...].astype(BF16)
        wu_s[...] = wu_ref[...].astype(BF16)
        wd_s[...] = wd_ref[...].astype(BF16)

    @pl.when(active)
    def _():
        x = x_ref[...]
        gte = jnp.dot(x, wg_s[...], preferred_element_type=F32)
        up = jnp.dot(x, wu_s[...], preferred_element_type=F32)
        hid = (gte * _sigmoid(gte)) * up
        o_ref[...] = jnp.dot(hid.astype(BF16), wd_s[...], preferred_element_type=F32)


def _moe_ffn(xg, block_expert, total_blocks, w_gate, w_up, w_down, layer):
    n_slots, d = xg.shape
    hid = w_gate.shape[-1]
    n_blocks = n_slots // MOE_BLOCK

    def rows(i, be, tot):
        return (jnp.minimum(i, tot[0] - 1), 0)

    def wmap(i, be, tot):
        return (layer, be[jnp.minimum(i, tot[0] - 1)], 0, 0)

    grid_spec = pltpu.PrefetchScalarGridSpec(
        num_scalar_prefetch=2,
        grid=(n_blocks,),
        in_specs=[pl.BlockSpec((MOE_BLOCK, d), rows),
                  pl.BlockSpec((None, None, d, hid), wmap),
                  pl.BlockSpec((None, None, d, hid), wmap),
                  pl.BlockSpec((None, None, hid, d), wmap)],
        out_specs=pl.BlockSpec((MOE_BLOCK, d), rows),
        scratch_shapes=[pltpu.VMEM((d, hid), BF16), pltpu.VMEM((d, hid), BF16), pltpu.VMEM((hid, d), BF16)],
    )
    return pl.pallas_call(
        _moe_kernel,
        grid_spec=grid_spec,
        out_shape=jax.ShapeDtypeStruct((n_slots, d), F32),
        compiler_params=_cparams(("arbitrary",), 58),
        name="moe_ffn",
    )(block_expert, total_blocks, xg, w_gate, w_up, w_down)


def _dispatch(ew, n_experts):
    t = ew.shape[0]
    eid = ew[:, 2:4].astype(jnp.int32).reshape(-1)
    n_assign = eid.shape[0]
    tok = jnp.repeat(jnp.arange(t, dtype=jnp.int32), 2)
    onehot = (eid[:, None] == jnp.arange(n_experts, dtype=jnp.int32)[None, :]).astype(jnp.int32)
    csum = jnp.cumsum(onehot, axis=0)
    counts = csum[-1]
    rank = jnp.sum((csum - onehot) * onehot, axis=1)
    padded = (counts + MOE_BLOCK - 1) // MOE_BLOCK * MOE_BLOCK
    pend = jnp.cumsum(padded)
    pstart = pend - padded
    n_blocks = n_assign // MOE_BLOCK + n_experts
    n_slots = n_blocks * MOE_BLOCK
    valid = eid < n_experts
    slot = jnp.where(valid, pstart[jnp.minimum(eid, n_experts - 1)] + rank, n_slots)
    slot_tok = jnp.zeros((n_slots,), jnp.int32).at[slot].set(tok, mode="drop")
    block_expert = jnp.minimum(
        jnp.searchsorted(pend, jnp.arange(n_blocks, dtype=jnp.int32) * MOE_BLOCK, side="right"),
        n_experts - 1).astype(jnp.int32)
    total_blocks = (pend[-1] // MOE_BLOCK).astype(jnp.int32).reshape(1)
    slot_c = jnp.where(valid, slot, 0).reshape(t, 2)
    return slot_tok, block_expert, total_blocks, slot_c


def kernel(x, meta_tokens, ln_emb_g, ln_emb_b, w_in, ssd_conv_w, ssd_conv_b, ssd_dt_bias, ssd_a_log, ssd_d,
           ssd_norm_g, ssd_proj, pool_w, pool_scale, pool_proj, conf_dw_w, conf_dw_b, conf_ln_g, conf_ln_b,
           conf_proj, gate_w, gate_b, w_out, ln1_g, ln1_b, router_group_w, router_group_b, router_expert_w,
           router_expert_b, exp_w_gate, exp_w_up, exp_w_down, ln2_g, ln2_b):
    nb, seq, d = x.shape
    n_meta = meta_tokens.shape[0]
    depth = w_in.shape[0]
    lp = PAD_ROWS + n_meta + seq
    t = nb * lp
    alpha = (2.0 * depth) ** 0.25
    d_inner = ssd_norm_g.shape[-1]
    xbc_w = ssd_conv_w.shape[-1]
    n_heads = ssd_a_log.shape[-1]
    pool_width = pool_scale.shape[-1]
    conf_width = conf_dw_b.shape[-1]
    n_groups = router_group_w.shape[-1]
    n_experts = router_expert_w.shape[-1]
    per_group = n_experts // n_groups
    assert (PAD_ROWS + n_meta) % SSD_CHUNK == 0 and lp % SSD_CHUNK == 0
    assert n_groups + n_experts <= LANES and n_heads <= LANES

    meta = jnp.broadcast_to(meta_tokens[None].astype(x.dtype), (nb, n_meta, d))
    xcat = jnp.concatenate([jnp.zeros((nb, PAD_ROWS, d), x.dtype), meta, x], axis=1).reshape(t, d)
    h, hb = _embed_ln(xcat, ln_emb_g, ln_emb_b, lp)

    c_dt = d_inner + xbc_w
    c_pc = c_dt + n_heads
    w_dt = jnp.pad(w_in[:, :, c_dt:c_pc], ((0, 0), (0, 0), (0, LANES - n_heads)))
    w_pc = w_in[:, :, c_pc:]
    rw = jnp.concatenate([router_group_w, router_expert_w], axis=-1)
    rw = jnp.pad(rw, ((0, 0), (0, 0), (0, LANES - rw.shape[-1])))
    rb = jnp.concatenate([router_group_b, router_expert_b], axis=-1)
    rb = jnp.pad(rb, ((0, 0), (0, LANES - rb.shape[-1]))).reshape(depth, 1, LANES)

    for i in range(depth):
        z = _mm(hb, w_in, i, 0, d_inner)
        xbc = _mm(hb, w_in, i, d_inner, xbc_w)
        dtr = _mm(hb, w_dt, i, 0, LANES)
        pc = _mm(hb, w_pc, i, 0, pool_width + 2 * conf_width)
        f_ssd = _ssd(z, xbc, dtr, ssd_conv_w[i], ssd_conv_b[i], ssd_dt_bias[i], ssd_a_log[i], ssd_d[i],
                     ssd_norm_g[i], nb, lp)
        f_pool = _pool(pc, pool_w[i], pool_scale[i], nb, lp)
        assert pool_width % conf_width == 0
        f_conf = _conf(pc, pool_width // conf_width, conf_dw_w[i], conf_dw_b[i], conf_ln_g[i], conf_ln_b[i], nb, lp)
        merged = _merge(hb, f_ssd, f_pool, f_conf, gate_w, gate_b, ssd_proj, pool_proj, conf_proj, i)
        y = _mm(merged, w_out, i, 0, d)
        h, hb = _add_ln(h, y, ln1_g[i], ln1_b[i], lp, alpha)

        ew = _router(h, rw[i], rb[i], lp, n_groups, per_group)
        slot_tok, block_expert, total_blocks, slot_c = _dispatch(ew, n_experts)
        xg = jnp.take(hb, slot_tok, axis=0)
        out = _moe_ffn(xg, block_expert, total_blocks, exp_w_gate, exp_w_up, exp_w_down, i)
        y = ew[:, 0:1] * jnp.take(out, slot_c[:, 0], axis=0) + ew[:, 1:2] * jnp.take(out, slot_c[:, 1], axis=0)
        h, hb = _add_ln(h, y, ln2_g[i], ln2_b[i], lp, alpha)

    return h.reshape(nb, lp, d)[:, PAD_ROWS + n_meta:]
```

```python
import functools

import jax
import jax.numpy as jnp
from jax import lax
from jax.experimental import pallas as pl
from jax.experimental.pallas import tpu as pltpu

F32 = jnp.float32
BF16 = jnp.bfloat16

PAD_ROWS = 112
SSD_CHUNK = 128
SSD_GROUPS = 4
SSD_HEAD_DIM = 64
POOL_WINDOWS = (2, 4, 8, 16)
MOE_BLOCK = 256
LN_EPS = 1e-5
LANES = 128
NEG_BIG = -1e30
LOG2E = 1.4426950408889634


def _cparams(sem, vmem_mb):
    return pltpu.CompilerParams(dimension_semantics=sem, vmem_limit_bytes=vmem_mb << 20)


def _sigmoid(x):
    return 1.0 / (1.0 + jnp.exp(-x))


def _softplus(x):
    return jnp.maximum(x, 0.0) + jnp.log1p(jnp.exp(-jnp.abs(x)))


def _ln(x, g, b):
    mu = jnp.mean(x, -1, keepdims=True)
    xc = x - mu
    var = jnp.mean(xc * xc, -1, keepdims=True)
    return xc * lax.rsqrt(var + LN_EPS) * g + b


def _rows_in_batch(blk, bm, lp):
    base = lax.rem(blk, lp // bm) * bm
    return base + lax.broadcasted_iota(jnp.int32, (bm, 1), 0)


def _pick(n, cands):
    for c in cands:
        if n % c == 0:
            return c
    raise ValueError(f"no block size for {n} among {cands}")


def _embed_kernel(x_ref, g_ref, b_ref, h_ref, hb_ref, *, bm, lp):
    y = _ln(x_ref[...], g_ref[...], b_ref[...])
    y = jnp.where(_rows_in_batch(pl.program_id(0), bm, lp) >= PAD_ROWS, y, 0.0)
    h_ref[...] = y
    hb_ref[...] = y.astype(BF16)


def _embed_ln(xcat, g, b, lp):
    t, d = xcat.shape
    bm = _pick(lp, (528, 384, 192, 128, 64, 8))
    row = pl.BlockSpec((bm, d), lambda i: (i, 0))
    vec = pl.BlockSpec((1, d), lambda i: (0, 0))
    return pl.pallas_call(
        functools.partial(_embed_kernel, bm=bm, lp=lp),
        grid=(t // bm,),
        in_specs=[row, vec, vec],
        out_specs=[row, row],
        out_shape=[jax.ShapeDtypeStruct((t, d), F32), jax.ShapeDtypeStruct((t, d), BF16)],
        compiler_params=_cparams(("parallel",), 40),
        name="embed_ln",
    )(xcat, g.reshape(1, d), b.reshape(1, d))


def _combine_ln_kernel(h_ref, g0_ref, g1_ref, ew_ref, g_ref, b_ref, o_ref, ob_ref, *, bm, lp, alpha):
    y = ew_ref[:, 0:1] * g0_ref[...] + ew_ref[:, 1:2] * g1_ref[...]
    y = _ln(alpha * h_ref[...] + y, g_ref[...], b_ref[...])
    y = jnp.where(_rows_in_batch(pl.program_id(0), bm, lp) >= PAD_ROWS, y, 0.0)
    o_ref[...] = y
    ob_ref[...] = y.astype(BF16)


def _combine_ln(h, g0, g1, ew, g, b, lp, alpha):
    t, d = h.shape
    bm = _pick(lp, (528, 384, 192, 128, 64, 8))
    row = pl.BlockSpec((bm, d), lambda i: (i, 0))
    vec = pl.BlockSpec((1, d), lambda i: (0, 0))
    return pl.pallas_call(
        functools.partial(_combine_ln_kernel, bm=bm, lp=lp, alpha=alpha),
        grid=(t // bm,),
        in_specs=[row, row, row, pl.BlockSpec((bm, LANES), lambda i: (i, 0)), vec, vec],
        out_specs=[row, row],
        out_shape=[jax.ShapeDtypeStruct((t, d), F32), jax.ShapeDtypeStruct((t, d), BF16)],
        compiler_params=_cparams(("parallel",), 56),
        name="combine_ln",
    )(h, g0, g1, ew, g.reshape(1, d), b.reshape(1, d))


def _mm_kernel(x_ref, w_ref, o_ref, wbf_ref):
    @pl.when(pl.program_id(1) == 0)
    def _():
        wbf_ref[...] = w_ref[...].astype(BF16)

    o_ref[...] = jnp.dot(x_ref[...], wbf_ref[...], preferred_element_type=F32).astype(o_ref.dtype)


def _mm(x, w, layer, col0, n_cols, out_dtype=F32):
    t, k = x.shape
    bn = _pick(n_cols, (1024, 512, 256, 128))
    bm = _pick(t, (1056, 528, 384, 192, 128, 64, 8))
    assert col0 % bn == 0
    cb0 = col0 // bn
    return pl.pallas_call(
        _mm_kernel,
        grid=(n_cols // bn, t // bm),
        in_specs=[pl.BlockSpec((bm, k), lambda n, m: (m, 0)),
                  pl.BlockSpec((None, k, bn), lambda n, m: (layer, 0, cb0 + n))],
        out_specs=pl.BlockSpec((bm, bn), lambda n, m: (m, n)),
        out_shape=jax.ShapeDtypeStruct((t, n_cols), out_dtype),
        scratch_shapes=[pltpu.VMEM((k, bn), BF16)],
        compiler_params=_cparams(("arbitrary", "arbitrary"), 48),
        name="mm",
    )(x, w)


def _mm_ln_kernel(x_ref, w_ref, h_ref, g_ref, b_ref, o_ref, ob_ref, wbf_ref, *, bm, lp, alpha):
    @pl.when(pl.program_id(0) == 0)
    def _():
        wbf_ref[...] = w_ref[...].astype(BF16)

    y = jnp.dot(x_ref[...], wbf_ref[...], preferred_element_type=F32)
    y = _ln(alpha * h_ref[...] + y, g_ref[...], b_ref[...])
    y = jnp.where(_rows_in_batch(pl.program_id(0), bm, lp) >= PAD_ROWS, y, 0.0)
    o_ref[...] = y
    ob_ref[...] = y.astype(BF16)


def _mm_ln(x, w, layer, h, g, b, lp, alpha):
    t, k = x.shape
    d = w.shape[-1]
    bm = _pick(lp, (528, 384, 192, 128, 64, 8))
    row = lambda width: pl.BlockSpec((bm, width), lambda i: (i, 0))
    vec = pl.BlockSpec((1, d), lambda i: (0, 0))
    return pl.pallas_call(
        functools.partial(_mm_ln_kernel, bm=bm, lp=lp, alpha=alpha),
        grid=(t // bm,),
        in_specs=[row(k), pl.BlockSpec((None, k, d), lambda i: (layer, 0, 0), pipeline_mode=pl.Buffered(1)),
                  row(d), vec, vec],
        out_specs=[row(d), row(d)],
        out_shape=[jax.ShapeDtypeStruct((t, d), F32), jax.ShapeDtypeStruct((t, d), BF16)],
        scratch_shapes=[pltpu.VMEM((k, d), BF16)],
        compiler_params=_cparams(("arbitrary",), 56),
        name="mm_ln",
    )(x, w, h, g.reshape(1, d), b.reshape(1, d))


def _ssd_kernel(z_ref, xbc_ref, dtr_ref, cw_ref, cb_ref, dtb_ref, alog_ref, dexp_ref, ng_ref, o_ref,
                cbuf, xc, ybuf, st, *, q, d_inner, n_state):
    c = pl.program_id(1)
    xbc_w = cbuf.shape[1]
    kc = cw_ref.shape[0]
    hist = 8

    @pl.when(c == 0)
    def _():
        cbuf[0:hist, :] = jnp.zeros((hist, xbc_w), F32)
        st[...] = jnp.zeros(st.shape, F32)

    cbuf[hist:hist + q, :] = xbc_ref[...]
    cw = 512
    for j in range(xbc_w // cw):
        sl = slice(j * cw, (j + 1) * cw)
        conv = cb_ref[:, sl] + cw_ref[kc - 1:kc, sl] * cbuf[hist:hist + q, sl]
        for k in range(kc - 1):
            off = hist - (kc - 1) + k
            conv = conv + cw_ref[k:k + 1, sl] * cbuf[off:off + q, sl]
        xc[:, sl] = conv * _sigmoid(conv)
    cbuf[0:hist, :] = cbuf[q:q + hist, :]

    row = c * q + lax.broadcasted_iota(jnp.int32, (q, 1), 0)
    dt = _softplus(dtr_ref[...] + dtb_ref[...])
    dt = jnp.where(row >= PAD_ROWS, dt, 0.0)
    adt = dt * (-jnp.exp(alog_ref[...]))
    li = lax.broadcasted_iota(jnp.int32, (q, q), 0)
    si = lax.broadcasted_iota(jnp.int32, (q, q), 1)
    tri = li >= si
    acs = jnp.dot(tri.astype(F32), adt, preferred_element_type=F32,
                  precision=lax.Precision.HIGHEST) * LOG2E
    acs_t = acs.T
    dt_t = dt.T
    e_acs = jnp.exp2(acs)
    last = acs_t[:, q - 1:q]
    w_t = jnp.exp2(last - acs_t) * dt_t
    dlast = jnp.exp2(last)
    lane = lax.broadcasted_iota(jnp.int32, (1, LANES), 1)
    lo_half = lane < SSD_HEAD_DIM

    heads_per_group = d_inner // SSD_HEAD_DIM // SSD_GROUPS
    pairs_per_group = heads_per_group // 2
    for g in range(SSD_GROUPS):
        b0 = d_inner + g * n_state
        c0 = d_inner + SSD_GROUPS * n_state + g * n_state
        bg = xc[:, b0:b0 + n_state]
        cg = xc[:, c0:c0 + n_state]
        cbm = lax.dot_general(cg.astype(BF16), bg.astype(BF16), (((1,), (1,)), ((), ())),
                              preferred_element_type=F32)
        bg_t = bg.T
        for j in range(pairs_per_group):
            pair = g * pairs_per_group + j
            h0 = 2 * pair
            xs_pair = xc[:, h0 * SSD_HEAD_DIM:h0 * SSD_HEAD_DIM + LANES]
            st_pair = st[pair]
            lhs_y, lhs_s, rhs_x, rhs_st = [], [], [], []
            for u in range(2):
                h = h0 + u
                col = acs[:, h:h + 1]
                rw = acs_t[h:h + 1, :]
                decay = jnp.where(tri, jnp.exp2(col - rw), 0.0)
                lhs_y.append((cbm * decay * dt_t[h:h + 1, :]).astype(BF16))
                lhs_s.append((bg_t * w_t[h:h + 1, :]).astype(BF16))
                keep = lo_half if u == 0 else jnp.logical_not(lo_half)
                rhs_x.append(jnp.where(keep, xs_pair, 0.0).astype(BF16))
                rhs_st.append(jnp.where(keep, st_pair, 0.0).astype(BF16))
            for u in range(2):
                lhs_y.append((e_acs[:, h0 + u:h0 + u + 1] * cg).astype(BF16))
            y_pair = jnp.dot(jnp.concatenate(lhs_y, axis=1), jnp.concatenate(rhs_x + rhs_st, axis=0),
                             preferred_element_type=F32)
            s_new = jnp.dot(jnp.concatenate(lhs_s, axis=1), jnp.concatenate(rhs_x, axis=0),
                            preferred_element_type=F32)
            dpair = jnp.where(lo_half, dlast[h0:h0 + 1, :], dlast[h0 + 1:h0 + 2, :])
            st[pair] = st_pair * dpair + s_new
            ybuf[:, h0 * SSD_HEAD_DIM:h0 * SSD_HEAD_DIM + LANES] = y_pair

    gw = d_inner // SSD_GROUPS
    for g in range(SSD_GROUPS):
        sl = slice(g * gw, (g + 1) * gw)
        y = ybuf[:, sl] + xc[:, sl] * dexp_ref[:, sl]
        zz = z_ref[:, sl]
        y = y * (zz * _sigmoid(zz))
        y = y * lax.rsqrt(jnp.mean(y * y, -1, keepdims=True) + LN_EPS)
        o_ref[:, sl] = (y * ng_ref[:, sl]).astype(BF16)


def _ssd(z, xbc, dtr, conv_w, conv_b, dt_bias, a_log, d_skip, norm_g, nb, lp):
    t, d_inner = z.shape
    xbc_w = xbc.shape[1]
    q = SSD_CHUNK
    n_heads = d_inner // SSD_HEAD_DIM
    n_state = (xbc_w - d_inner) // (2 * SSD_GROUPS)
    assert n_state == LANES and n_heads % (2 * SSD_GROUPS) == 0 and lp % q == 0
    assert xbc_w % 512 == 0 and PAD_ROWS % 8 == 0
    nc = lp // q
    padh = LANES - n_heads
    row = lambda w: pl.BlockSpec((q, w), lambda b, c: (b * nc + c, 0))
    full = lambda a: pl.BlockSpec(a.shape, lambda b, c: (0,) * a.ndim)
    dtb = jnp.pad(dt_bias, (0, padh)).reshape(1, LANES)
    alog = jnp.pad(a_log, (0, padh)).reshape(1, LANES)
    dexp = jnp.repeat(d_skip, SSD_HEAD_DIM).reshape(1, d_inner)
    cb = conv_b.reshape(1, xbc_w)
    ng = norm_g.reshape(1, d_inner)
    return pl.pallas_call(
        functools.partial(_ssd_kernel, q=q, d_inner=d_inner, n_state=n_state),
        grid=(nb, nc),
        in_specs=[row(d_inner), row(xbc_w), row(LANES), full(conv_w), full(cb), full(dtb), full(alog),
                  full(dexp), full(ng)],
        out_specs=row(d_inner),
        out_shape=jax.ShapeDtypeStruct((t, d_inner), BF16),
        scratch_shapes=[pltpu.VMEM((8 + q, xbc_w), F32), pltpu.VMEM((q, xbc_w), F32),
                        pltpu.VMEM((q, d_inner), F32), pltpu.VMEM((n_heads // 2, n_state, LANES), F32)],
        compiler_params=_cparams(("arbitrary", "arbitrary"), 32),
        name="ssd",
    )(z, xbc, dtr, conv_w, cb, dtb, alog, dexp, ng)


def _pool_kernel(u_ref, w_ref, sc_ref, o_ref, buf, *, r, gd):
    c = pl.program_id(1)
    hist = 16
    width = buf.shape[1]

    @pl.when(c == 0)
    def _():
        buf[0:hist, :] = jnp.zeros((hist, width), F32)

    buf[hist:hist + r, :] = u_ref[...]
    pos = c * r + lax.broadcasted_iota(jnp.int32, (r, 1), 0) - PAD_ROWS
    for gi, win in enumerate(POOL_WINDOWS):
        sl = slice(gi * gd, (gi + 1) * gd)
        u = buf[hist:hist + r, sl]
        s = u
        for j in range(1, win):
            s = s + buf[hist - j:hist - j + r, sl]
        cnt = jnp.clip(pos + 1, 1, win).astype(F32)
        mixed = s / cnt - u
        y = jnp.dot(mixed.astype(BF16), w_ref[gi].astype(BF16), preferred_element_type=F32)
        o_ref[:, sl] = (y * sc_ref[:, sl]).astype(BF16)
    buf[0:hist, :] = buf[r:r + hist, :]


def _pool(pc, pool_w, scale, nb, lp):
    t = pc.shape[0]
    ng, gd, _ = pool_w.shape
    width = ng * gd
    assert ng == len(POOL_WINDOWS) and gd % LANES == 0
    r = _pick(lp, (384, 192, 128, 64))
    nc = lp // r
    return pl.pallas_call(
        functools.partial(_pool_kernel, r=r, gd=gd),
        grid=(nb, nc),
        in_specs=[pl.BlockSpec((r, width), lambda b, c: (b * nc + c, 0)),
                  pl.BlockSpec(pool_w.shape, lambda b, c: (0, 0, 0)),
                  pl.BlockSpec((1, width), lambda b, c: (0, 0))],
        out_specs=pl.BlockSpec((r, width), lambda b, c: (b * nc + c, 0)),
        out_shape=jax.ShapeDtypeStruct((t, width), BF16),
        scratch_shapes=[pltpu.VMEM((16 + r, width), F32)],
        compiler_params=_cparams(("arbitrary", "arbitrary"), 32),
        name="pool",
    )(pc, pool_w, scale.reshape(1, width))


def _conf_kernel(a_ref, g_ref, w_ref, b_ref, lg_ref, lb_ref, o_ref, vbuf, cv, *, r):
    c = pl.program_id(1)
    hist = 32
    width = vbuf.shape[1]
    kc = w_ref.shape[0]

    @pl.when(c == 0)
    def _():
        vbuf[0:hist, :] = jnp.zeros((hist, width), F32)

    vbuf[hist:hist + r, :] = a_ref[...] * _sigmoid(g_ref[...])
    cw = 256
    for j in range(width // cw):
        sl = slice(j * cw, (j + 1) * cw)
        acc = b_ref[:, sl] + w_ref[kc - 1:kc, sl] * vbuf[hist:hist + r, sl]
        for k in range(kc - 1):
            off = hist - (kc - 1) + k
            acc = acc + w_ref[k:k + 1, sl] * vbuf[off:off + r, sl]
        cv[:, sl] = acc
    vbuf[0:hist, :] = vbuf[r:r + hist, :]
    v = _ln(cv[...], lg_ref[...], lb_ref[...])
    o_ref[...] = (v * _sigmoid(v)).astype(BF16)


def _conf(pc, col_blk, dw_w, dw_b, ln_g, ln_b, nb, lp):
    t = pc.shape[0]
    kc, width = dw_w.shape
    assert kc <= 33 and width % 256 == 0
    r = _pick(lp, (192, 128, 64))
    nc = lp // r
    vec = lambda: pl.BlockSpec((1, width), lambda b, c: (0, 0))
    return pl.pallas_call(
        functools.partial(_conf_kernel, r=r),
        grid=(nb, nc),
        in_specs=[pl.BlockSpec((r, width), lambda b, c: (b * nc + c, col_blk)),
                  pl.BlockSpec((r, width), lambda b, c: (b * nc + c, col_blk + 1)),
                  pl.BlockSpec((kc, width), lambda b, c: (0, 0)), vec(), vec(), vec()],
        out_specs=pl.BlockSpec((r, width), lambda b, c: (b * nc + c, 0)),
        out_shape=jax.ShapeDtypeStruct((t, width), BF16),
        scratch_shapes=[pltpu.VMEM((32 + r, width), F32), pltpu.VMEM((r, width), F32)],
        compiler_params=_cparams(("arbitrary", "arbitrary"), 32),
        name="conf",
    )(pc, pc, dw_w, dw_b.reshape(1, width), ln_g.reshape(1, width), ln_b.reshape(1, width))


def _merge_kernel(hb_ref, f0_ref, f1_ref, f2_ref, g0_ref, g1_ref, g2_ref, gb_ref, p0_ref, p1_ref, p2_ref,
                  o_ref, gs, s0, s1, s2):
    @pl.when(pl.program_id(1) == 0)
    def _():
        gs[0] = g0_ref[...].astype(BF16)
        gs[1] = g1_ref[...].astype(BF16)
        gs[2] = g2_ref[...].astype(BF16)
        s0[...] = p0_ref[...].astype(BF16)
        s1[...] = p1_ref[...].astype(BF16)
        s2[...] = p2_ref[...].astype(BF16)

    hb = hb_ref[...]
    acc = None
    for j, (f_ref, s_ref) in enumerate(((f0_ref, s0), (f1_ref, s1), (f2_ref, s2))):
        gate = _sigmoid(jnp.dot(hb, gs[j], preferred_element_type=F32) + gb_ref[j:j + 1, :])
        term = gate * jnp.dot(f_ref[...], s_ref[...], preferred_element_type=F32)
        acc = term if acc is None else acc + term
    o_ref[...] = acc.astype(BF16)


def _merge(hb, f_ssd, f_pool, f_conf, gate_w, gate_b, ssd_proj, pool_proj, conf_proj, layer):
    t, d = hb.shape
    bn = _pick(d, (512, 256, 128))
    bm = _pick(t, (528, 384, 192, 128, 64, 8))
    nbn = d // bn
    once = pl.Buffered(1)
    rowspec = lambda a: pl.BlockSpec((bm, a.shape[1]), lambda n, m: (m, 0))
    gspec = lambda j: pl.BlockSpec((None, d, bn), lambda n, m: (layer, 0, j * nbn + n), pipeline_mode=once)
    pspec = lambda w: pl.BlockSpec((None, w.shape[1], bn), lambda n, m: (layer, 0, n), pipeline_mode=once)
    gb = gate_b[layer].reshape(3, d)
    return pl.pallas_call(
        _merge_kernel,
        grid=(nbn, t // bm),
        in_specs=[rowspec(hb), rowspec(f_ssd), rowspec(f_pool), rowspec(f_conf),
                  gspec(0), gspec(1), gspec(2), pl.BlockSpec((3, bn), lambda n, m: (0, n)),
                  pspec(ssd_proj), pspec(pool_proj), pspec(conf_proj)],
        out_specs=pl.BlockSpec((bm, bn), lambda n, m: (m, n)),
        out_shape=jax.ShapeDtypeStruct((t, d), BF16),
        scratch_shapes=[pltpu.VMEM((3, d, bn), BF16), pltpu.VMEM((ssd_proj.shape[1], bn), BF16),
                        pltpu.VMEM((pool_proj.shape[1], bn), BF16), pltpu.VMEM((conf_proj.shape[1], bn), BF16)],
        compiler_params=_cparams(("arbitrary", "arbitrary"), 52),
        name="merge",
    )(hb, f_ssd, f_pool, f_conf, gate_w, gate_w, gate_w, gb, ssd_proj, pool_proj, conf_proj)


def _router_kernel(h_ref, w_ref, b_ref, o_ref, cnt_ref, run, *, bm, lp, n_groups, per_group):
    @pl.when(pl.program_id(0) == 0)
    def _():
        run[...] = jnp.zeros(run.shape, F32)

    logits = jnp.dot(h_ref[...], w_ref[...], preferred_element_type=F32,
                     precision=lax.Precision.HIGHEST) + b_ref[...]
    lane = lax.broadcasted_iota(jnp.int32, logits.shape, 1).astype(F32)

    def first_max(vals):
        m = jnp.max(vals, axis=1, keepdims=True)
        idx = jnp.min(jnp.where(vals == m, lane, float(LANES)), axis=1, keepdims=True)
        return m, idx

    gmask = lane < n_groups
    gmax, gsel = first_max(jnp.where(gmask, logits, NEG_BIG))
    gsum = jnp.sum(jnp.where(gmask, jnp.exp(logits - gmax), 0.0), axis=1, keepdims=True)
    p_group = 1.0 / gsum
    lo = n_groups + per_group * gsel
    el = jnp.where(jnp.logical_and(lane >= lo, lane < lo + per_group), logits, NEG_BIG)
    m1, i1 = first_max(el)
    m2, i2 = first_max(jnp.where(lane == i1, NEG_BIG, el))
    ratio = jnp.exp(m2 - m1)
    w1 = p_group / (1.0 + ratio)
    w2 = w1 * ratio
    real = _rows_in_batch(pl.program_id(0), bm, lp) >= PAD_ROWS
    sentinel = float(n_groups * per_group)
    e1 = jnp.where(real, i1 - n_groups, sentinel)
    e2 = jnp.where(real, i2 - n_groups, sentinel)
    w1 = jnp.where(real, w1, 0.0)
    w2 = jnp.where(real, w2, 0.0)
    hit1 = jnp.logical_and(lane == i1, real)
    hit2 = jnp.logical_and(lane == i2, real)
    onehot = jnp.where(jnp.logical_or(hit1, hit2), 1.0, 0.0)
    li = lax.broadcasted_iota(jnp.int32, (bm, bm), 0)
    si = lax.broadcasted_iota(jnp.int32, (bm, bm), 1)
    before = jnp.where(li > si, 1.0, 0.0).astype(BF16)
    seen = jnp.dot(before, onehot.astype(BF16), preferred_element_type=F32) + run[...]
    r1 = jnp.sum(jnp.where(hit1, seen, 0.0), axis=1, keepdims=True)
    r2 = jnp.sum(jnp.where(hit2, seen, 0.0), axis=1, keepdims=True)
    run[...] = run[...] + jnp.sum(onehot, axis=0, keepdims=True)
    cnt_ref[...] = run[...]
    out = jnp.zeros(logits.shape, F32)
    for k, v in enumerate((w1, w2, e1, e2, r1, r2)):
        out = jnp.where(lane == k, v, out)
    o_ref[...] = out


def _router(h, rw, rb, lp, n_groups, per_group):
    t, d = h.shape
    bm = _pick(lp, (528, 384, 192, 128, 64, 8))
    return pl.pallas_call(
        functools.partial(_router_kernel, bm=bm, lp=lp, n_groups=n_groups, per_group=per_group),
        grid=(t // bm,),
        in_specs=[pl.BlockSpec((bm, d), lambda i: (i, 0)), pl.BlockSpec((d, LANES), lambda i: (0, 0)),
                  pl.BlockSpec((1, LANES), lambda i: (0, 0))],
        out_specs=[pl.BlockSpec((bm, LANES), lambda i: (i, 0)), pl.BlockSpec((1, LANES), lambda i: (0, 0))],
        out_shape=[jax.ShapeDtypeStruct((t, LANES), F32), jax.ShapeDtypeStruct((1, LANES), F32)],
        scratch_shapes=[pltpu.VMEM((1, LANES), F32)],
        compiler_params=_cparams(("arbitrary",), 40),
        name="router",
    )(h, rw, rb)


def _moe_kernel(be_ref, tot_ref, x_ref, wg_ref, wu_ref, wd_ref, o_ref, wg_s, wu_s, wd_s):
    i = pl.program_id(0)
    total = tot_ref[0]
    ic = jnp.minimum(i, total - 1)
    e = be_ref[ic]
    e_prev = be_ref[jnp.maximum(ic - 1, 0)]
    active = i < total

    @pl.when(jnp.logical_and(active, jnp.logical_or(i == 0, e != e_prev)))
    def _():
        wg_s[...] = wg_ref[...].astype(BF16)
        wu_s[...] = wu_ref[...].astype(BF16)
        wd_s[...] = wd_ref[...].astype(BF16)

    @pl.when(active)
    def _():
        x = x_ref[...]
        gte = jnp.dot(x, wg_s[...], preferred_element_type=F32)
        up = jnp.dot(x, wu_s[...], preferred_element_type=F32)
        hid = (gte * _sigmoid(gte)) * up
        o_ref[...] = jnp.dot(hid.astype(BF16), wd_s[...], preferred_element_type=F32)


def _moe_ffn(xg, block_expert, total_blocks, w_gate, w_up, w_down, layer):
    n_slots, d = xg.shape
    hid = w_gate.shape[-1]
    n_blocks = n_slots // MOE_BLOCK

    def rows(i, be, tot):
        return (jnp.minimum(i, tot[0] - 1), 0)

    def wmap(i, be, tot):
        return (layer, be[jnp.minimum(i, tot[0] - 1)], 0, 0)

    grid_spec = pltpu.PrefetchScalarGridSpec(
        num_scalar_prefetch=2,
        grid=(n_blocks,),
        in_specs=[pl.BlockSpec((MOE_BLOCK, d), rows),
                  pl.BlockSpec((None, None, d, hid), wmap),
                  pl.BlockSpec((None, None, d, hid), wmap),
                  pl.BlockSpec((None, None, hid, d), wmap)],
        out_specs=pl.BlockSpec((MOE_BLOCK, d), rows),
        scratch_shapes=[pltpu.VMEM((d, hid), BF16), pltpu.VMEM((d, hid), BF16), pltpu.VMEM((hid, d), BF16)],
    )
    return pl.pallas_call(
        _moe_kernel,
        grid_spec=grid_spec,
        out_shape=jax.ShapeDtypeStruct((n_slots, d), F32),
        compiler_params=_cparams(("arbitrary",), 58),
        name="moe_ffn",
    )(block_expert, total_blocks, xg, w_gate, w_up, w_down)


def _dispatch(ew, cnt, n_groups, n_experts):
    t = ew.shape[0]
    eid = ew[:, 2:4].astype(jnp.int32).reshape(-1)
    rank = ew[:, 4:6].astype(jnp.int32).reshape(-1)
    counts = cnt[0, n_groups:n_groups + n_experts].astype(jnp.int32)
    n_assign = eid.shape[0]
    tok = jnp.repeat(jnp.arange(t, dtype=jnp.int32), 2)
    padded = (counts + MOE_BLOCK - 1) // MOE_BLOCK * MOE_BLOCK
    pend = jnp.cumsum(padded)
    pstart = pend - padded
    n_blocks = n_assign // MOE_BLOCK + n_experts
    n_slots = n_blocks * MOE_BLOCK
    valid = eid < n_experts
    slot = jnp.where(valid, pstart[jnp.minimum(eid, n_experts - 1)] + rank, n_slots)
    slot_tok = jnp.zeros((n_slots,), jnp.int32).at[slot].set(tok, mode="drop")
    block_expert = jnp.minimum(
        jnp.searchsorted(pend, jnp.arange(n_blocks, dtype=jnp.int32) * MOE_BLOCK, side="right"),
        n_experts - 1).astype(jnp.int32)
    total_blocks = (pend[-1] // MOE_BLOCK).astype(jnp.int32).reshape(1)
    slot_c = jnp.where(valid, slot, 0).reshape(t, 2)
    return slot_tok, block_expert, total_blocks, slot_c


def kernel(x, meta_tokens, ln_emb_g, ln_emb_b, w_in, ssd_conv_w, ssd_conv_b, ssd_dt_bias, ssd_a_log, ssd_d,
           ssd_norm_g, ssd_proj, pool_w, pool_scale, pool_proj, conf_dw_w, conf_dw_b, conf_ln_g, conf_ln_b,
           conf_proj, gate_w, gate_b, w_out, ln1_g, ln1_b, router_group_w, router_group_b, router_expert_w,
           router_expert_b, exp_w_gate, exp_w_up, exp_w_down, ln2_g, ln2_b):
    nb, seq, d = x.shape
    n_meta = meta_tokens.shape[0]
    depth = w_in.shape[0]
    lp = PAD_ROWS + n_meta + seq
    t = nb * lp
    alpha = (2.0 * depth) ** 0.25
    d_inner = ssd_norm_g.shape[-1]
    xbc_w = ssd_conv_w.shape[-1]
    n_heads = ssd_a_log.shape[-1]
    pool_width = pool_scale.shape[-1]
    conf_width = conf_dw_b.shape[-1]
    n_groups = router_group_w.shape[-1]
    n_experts = router_expert_w.shape[-1]
    per_group = n_experts // n_groups
    assert (PAD_ROWS + n_meta) % SSD_CHUNK == 0 and lp % SSD_CHUNK == 0
    assert n_groups + n_experts <= LANES and n_heads <= LANES

    meta = jnp.broadcast_to(meta_tokens[None].astype(x.dtype), (nb, n_meta, d))
    xcat = jnp.concatenate([jnp.zeros((nb, PAD_ROWS, d), x.dtype), meta, x], axis=1).reshape(t, d)
    h, hb = _embed_ln(xcat, ln_emb_g, ln_emb_b, lp)

    c_dt = d_inner + xbc_w
    c_pc = c_dt + n_heads
    w_dt = jnp.pad(w_in[:, :, c_dt:c_pc], ((0, 0), (0, 0), (0, LANES - n_heads)))
    w_pc = w_in[:, :, c_pc:]
    rw = jnp.concatenate([router_group_w, router_expert_w], axis=-1)
    rw = jnp.pad(rw, ((0, 0), (0, 0), (0, LANES - rw.shape[-1])))
    rb = jnp.concatenate([router_group_b, router_expert_b], axis=-1)
    rb = jnp.pad(rb, ((0, 0), (0, LANES - rb.shape[-1]))).reshape(depth, 1, LANES)

    for i in range(depth):
        z = _mm(hb, w_in, i, 0, d_inner)
        xbc = _mm(hb, w_in, i, d_inner, xbc_w)
        dtr = _mm(hb, w_dt, i, 0, LANES)
        pc = _mm(hb, w_pc, i, 0, pool_width + 2 * conf_width)
        f_ssd = _ssd(z, xbc, dtr, ssd_conv_w[i], ssd_conv_b[i], ssd_dt_bias[i], ssd_a_log[i], ssd_d[i],
                     ssd_norm_g[i], nb, lp)
        f_pool = _pool(pc, pool_w[i], pool_scale[i], nb, lp)
        assert pool_width % conf_width == 0
        f_conf = _conf(pc, pool_width // conf_width, conf_dw_w[i], conf_dw_b[i], conf_ln_g[i], conf_ln_b[i], nb, lp)
        merged = _merge(hb, f_ssd, f_pool, f_conf, gate_w, gate_b, ssd_proj, pool_proj, conf_proj, i)
        h, hb = _mm_ln(merged, w_out, i, h, ln1_g[i], ln1_b[i], lp, alpha)

        ew, cnt = _router(h, rw[i], rb[i], lp, n_groups, per_group)
        slot_tok, block_expert, total_blocks, slot_c = _dispatch(ew, cnt, n_groups, n_experts)
        xg = jnp.take(hb, slot_tok, axis=0)
        out = _moe_ffn(xg, block_expert, total_blocks, exp_w_gate, exp_w_up, exp_w_down, i)
        g0 = jnp.take(out, slot_c[:, 0], axis=0)
        g1 = jnp.take(out, slot_c[:, 1], axis=0)
        h, hb = _combine_ln(h, g0, g1, ew, ln2_g[i], ln2_b[i], lp, alpha)

    return h.reshape(nb, lp, d)[:, PAD_ROWS + n_meta:]
```

```python
import functools

import jax
import jax.numpy as jnp
from jax import lax
from jax.experimental import pallas as pl
from jax.experimental.pallas import tpu as pltpu

F32 = jnp.float32
BF16 = jnp.bfloat16

PAD_ROWS = 112
SSD_CHUNK = 128
SSD_GROUPS = 4
SSD_HEAD_DIM = 64
POOL_WINDOWS = (2, 4, 8, 16)
MOE_BLOCK = 256
LN_EPS = 1e-5
LANES = 128
NEG_BIG = -1e30
LOG2E = 1.4426950408889634


def _cparams(sem, vmem_mb):
    return pltpu.CompilerParams(dimension_semantics=sem, vmem_limit_bytes=vmem_mb << 20)


def _sigmoid(x):
    return 1.0 / (1.0 + jnp.exp(-x))


def _softplus(x):
    return jnp.maximum(x, 0.0) + jnp.log1p(jnp.exp(-jnp.abs(x)))


def _ln(x, g, b):
    mu = jnp.mean(x, -1, keepdims=True)
    xc = x - mu
    var = jnp.mean(xc * xc, -1, keepdims=True)
    return xc * lax.rsqrt(var + LN_EPS) * g + b


def _rows_in_batch(blk, bm, lp):
    base = lax.rem(blk, lp // bm) * bm
    return base + lax.broadcasted_iota(jnp.int32, (bm, 1), 0)


def _pick(n, cands):
    for c in cands:
        if n % c == 0:
            return c
    raise ValueError(f"no block size for {n} among {cands}")


def _embed_kernel(x_ref, g_ref, b_ref, h_ref, hb_ref, *, bm, lp):
    y = _ln(x_ref[...], g_ref[...], b_ref[...])
    y = jnp.where(_rows_in_batch(pl.program_id(0), bm, lp) >= PAD_ROWS, y, 0.0)
    h_ref[...] = y
    hb_ref[...] = y.astype(BF16)


def _embed_ln(xcat, g, b, lp):
    t, d = xcat.shape
    bm = _pick(lp, (528, 384, 192, 128, 64, 8))
    row = pl.BlockSpec((bm, d), lambda i: (i, 0))
    vec = pl.BlockSpec((1, d), lambda i: (0, 0))
    return pl.pallas_call(
        functools.partial(_embed_kernel, bm=bm, lp=lp),
        grid=(t // bm,),
        in_specs=[row, vec, vec],
        out_specs=[row, row],
        out_shape=[jax.ShapeDtypeStruct((t, d), F32), jax.ShapeDtypeStruct((t, d), BF16)],
        compiler_params=_cparams(("parallel",), 40),
        name="embed_ln",
    )(xcat, g.reshape(1, d), b.reshape(1, d))


def _combine_ln_kernel(h_ref, g0_ref, g1_ref, ew_ref, g_ref, b_ref, o_ref, ob_ref, *, bm, lp, alpha):
    y = ew_ref[:, 0:1] * g0_ref[...] + ew_ref[:, 1:2] * g1_ref[...]
    y = _ln(alpha * h_ref[...] + y, g_ref[...], b_ref[...])
    y = jnp.where(_rows_in_batch(pl.program_id(0), bm, lp) >= PAD_ROWS, y, 0.0)
    o_ref[...] = y
    ob_ref[...] = y.astype(BF16)


def _combine_ln(h, g0, g1, ew, g, b, lp, alpha):
    t, d = h.shape
    bm = _pick(lp, (528, 384, 192, 128, 64, 8))
    row = pl.BlockSpec((bm, d), lambda i: (i, 0))
    vec = pl.BlockSpec((1, d), lambda i: (0, 0))
    return pl.pallas_call(
        functools.partial(_combine_ln_kernel, bm=bm, lp=lp, alpha=alpha),
        grid=(t // bm,),
        in_specs=[row, row, row, pl.BlockSpec((bm, LANES), lambda i: (i, 0)), vec, vec],
        out_specs=[row, row],
        out_shape=[jax.ShapeDtypeStruct((t, d), F32), jax.ShapeDtypeStruct((t, d), BF16)],
        compiler_params=_cparams(("parallel",), 56),
        name="combine_ln",
    )(h, g0, g1, ew, g.reshape(1, d), b.reshape(1, d))


def _mm_kernel(x_ref, w_ref, o_ref, wbf_ref):
    @pl.when(pl.program_id(1) == 0)
    def _():
        wbf_ref[...] = w_ref[...].astype(BF16)

    o_ref[...] = jnp.dot(x_ref[...], wbf_ref[...], preferred_element_type=F32).astype(o_ref.dtype)


def _mm(x, w, layer, col0, n_cols, out_dtype=F32):
    t, k = x.shape
    bn = _pick(n_cols, (1024, 512, 256, 128))
    bm = _pick(t, (1056, 528, 384, 192, 128, 64, 8))
    assert col0 % bn == 0
    cb0 = col0 // bn
    return pl.pallas_call(
        _mm_kernel,
        grid=(n_cols // bn, t // bm),
        in_specs=[pl.BlockSpec((bm, k), lambda n, m: (m, 0)),
                  pl.BlockSpec((None, k, bn), lambda n, m: (layer, 0, cb0 + n))],
        out_specs=pl.BlockSpec((bm, bn), lambda n, m: (m, n)),
        out_shape=jax.ShapeDtypeStruct((t, n_cols), out_dtype),
        scratch_shapes=[pltpu.VMEM((k, bn), BF16)],
        compiler_params=_cparams(("arbitrary", "arbitrary"), 48),
        name="mm",
    )(x, w)


def _mm_ln_kernel(x_ref, w_ref, h_ref, g_ref, b_ref, o_ref, wbf_ref, *, bm, lp, alpha):
    @pl.when(pl.program_id(0) == 0)
    def _():
        wbf_ref[...] = w_ref[...].astype(BF16)

    y = jnp.dot(x_ref[...], wbf_ref[...], preferred_element_type=F32)
    y = _ln(alpha * h_ref[...] + y, g_ref[...], b_ref[...])
    o_ref[...] = jnp.where(_rows_in_batch(pl.program_id(0), bm, lp) >= PAD_ROWS, y, 0.0)


def _mm_ln(x, w, layer, h, g, b, lp, alpha):
    t, k = x.shape
    d = w.shape[-1]
    bm = _pick(lp, (528, 384, 192, 128, 64, 8))
    row = lambda width: pl.BlockSpec((bm, width), lambda i: (i, 0))
    vec = pl.BlockSpec((1, d), lambda i: (0, 0))
    return pl.pallas_call(
        functools.partial(_mm_ln_kernel, bm=bm, lp=lp, alpha=alpha),
        grid=(t // bm,),
        in_specs=[row(k), pl.BlockSpec((None, k, d), lambda i: (layer, 0, 0), pipeline_mode=pl.Buffered(1)),
                  row(d), vec, vec],
        out_specs=row(d),
        out_shape=jax.ShapeDtypeStruct((t, d), F32),
        scratch_shapes=[pltpu.VMEM((k, d), BF16)],
        compiler_params=_cparams(("arbitrary",), 56),
        name="mm_ln",
    )(x, w, h, g.reshape(1, d), b.reshape(1, d))


def _ssd_kernel(z_ref, xbc_ref, dtr_ref, cw_ref, cb_ref, dtb_ref, alog_ref, dexp_ref, ng_ref, o_ref,
                cbuf, xc, ybuf, st, *, q, d_inner, n_state):
    c = pl.program_id(1)
    xbc_w = cbuf.shape[1]
    kc = cw_ref.shape[0]
    hist = 8

    @pl.when(c == 0)
    def _():
        cbuf[0:hist, :] = jnp.zeros((hist, xbc_w), F32)
        st[...] = jnp.zeros(st.shape, F32)

    cbuf[hist:hist + q, :] = xbc_ref[...]
    cw = 512
    for j in range(xbc_w // cw):
        sl = slice(j * cw, (j + 1) * cw)
        conv = cb_ref[:, sl] + cw_ref[kc - 1:kc, sl] * cbuf[hist:hist + q, sl]
        for k in range(kc - 1):
            off = hist - (kc - 1) + k
            conv = conv + cw_ref[k:k + 1, sl] * cbuf[off:off + q, sl]
        xc[:, sl] = conv * _sigmoid(conv)
    cbuf[0:hist, :] = cbuf[q:q + hist, :]

    row = c * q + lax.broadcasted_iota(jnp.int32, (q, 1), 0)
    dt = _softplus(dtr_ref[...] + dtb_ref[...])
    dt = jnp.where(row >= PAD_ROWS, dt, 0.0)
    adt = dt * (-jnp.exp(alog_ref[...]))
    li = lax.broadcasted_iota(jnp.int32, (q, q), 0)
    si = lax.broadcasted_iota(jnp.int32, (q, q), 1)
    tri = li >= si
    acs = jnp.dot(tri.astype(F32), adt, preferred_element_type=F32,
                  precision=lax.Precision.HIGHEST) * LOG2E
    acs_t = acs.T
    dt_t = dt.T
    e_acs = jnp.exp2(acs)
    last = acs_t[:, q - 1:q]
    w_t = jnp.exp2(last - acs_t) * dt_t
    dlast = jnp.exp2(last)
    lane = lax.broadcasted_iota(jnp.int32, (1, LANES), 1)
    lo_half = lane < SSD_HEAD_DIM

    heads_per_group = d_inner // SSD_HEAD_DIM // SSD_GROUPS
    pairs_per_group = heads_per_group // 2
    for g in range(SSD_GROUPS):
        b0 = d_inner + g * n_state
        c0 = d_inner + SSD_GROUPS * n_state + g * n_state
        bg = xc[:, b0:b0 + n_state]
        cg = xc[:, c0:c0 + n_state]
        cbm = lax.dot_general(cg.astype(BF16), bg.astype(BF16), (((1,), (1,)), ((), ())),
                              preferred_element_type=F32)
        bg_t = bg.T
        for j in range(pairs_per_group):
            pair = g * pairs_per_group + j
            h0 = 2 * pair
            xs_pair = xc[:, h0 * SSD_HEAD_DIM:h0 * SSD_HEAD_DIM + LANES]
            st_pair = st[pair]
            lhs_y, lhs_s, rhs_x, rhs_st = [], [], [], []
            for u in range(2):
                h = h0 + u
                col = acs[:, h:h + 1]
                rw = acs_t[h:h + 1, :]
                decay = jnp.where(tri, jnp.exp2(col - rw), 0.0)
                lhs_y.append((cbm * decay * dt_t[h:h + 1, :]).astype(BF16))
                lhs_s.append((bg_t * w_t[h:h + 1, :]).astype(BF16))
                keep = lo_half if u == 0 else jnp.logical_not(lo_half)
                rhs_x.append(jnp.where(keep, xs_pair, 0.0).astype(BF16))
                rhs_st.append(jnp.where(keep, st_pair, 0.0).astype(BF16))
            for u in range(2):
                lhs_y.append((e_acs[:, h0 + u:h0 + u + 1] * cg).astype(BF16))
            y_pair = jnp.dot(jnp.concatenate(lhs_y, axis=1), jnp.concatenate(rhs_x + rhs_st, axis=0),
                             preferred_element_type=F32)
            s_new = jnp.dot(jnp.concatenate(lhs_s, axis=1), jnp.concatenate(rhs_x, axis=0),
                            preferred_element_type=F32)
            dpair = jnp.where(lo_half, dlast[h0:h0 + 1, :], dlast[h0 + 1:h0 + 2, :])
            st[pair] = st_pair * dpair + s_new
            ybuf[:, h0 * SSD_HEAD_DIM:h0 * SSD_HEAD_DIM + LANES] = y_pair

    gw = d_inner // SSD_GROUPS
    for g in range(SSD_GROUPS):
        sl = slice(g * gw, (g + 1) * gw)
        y = ybuf[:, sl] + xc[:, sl] * dexp_ref[:, sl]
        zz = z_ref[:, sl]
        y = y * (zz * _sigmoid(zz))
        y = y * lax.rsqrt(jnp.mean(y * y, -1, keepdims=True) + LN_EPS)
        o_ref[:, sl] = (y * ng_ref[:, sl]).astype(BF16)


def _ssd(z, xbc, dtr, conv_w, conv_b, dt_bias, a_log, d_skip, norm_g, nb, lp):
    t, d_inner = z.shape
    xbc_w = xbc.shape[1]
    q = SSD_CHUNK
    n_heads = d_inner // SSD_HEAD_DIM
    n_state = (xbc_w - d_inner) // (2 * SSD_GROUPS)
    assert n_state == LANES and n_heads % (2 * SSD_GROUPS) == 0 and lp % q == 0
    assert xbc_w % 512 == 0 and PAD_ROWS % 8 == 0
    nc = lp // q
    padh = LANES - n_heads
    row = lambda w: pl.BlockSpec((q, w), lambda b, c: (b * nc + c, 0))
    full = lambda a: pl.BlockSpec(a.shape, lambda b, c: (0,) * a.ndim)
    dtb = jnp.pad(dt_bias, (0, padh)).reshape(1, LANES)
    alog = jnp.pad(a_log, (0, padh)).reshape(1, LANES)
    dexp = jnp.repeat(d_skip, SSD_HEAD_DIM).reshape(1, d_inner)
    cb = conv_b.reshape(1, xbc_w)
    ng = norm_g.reshape(1, d_inner)
    return pl.pallas_call(
        functools.partial(_ssd_kernel, q=q, d_inner=d_inner, n_state=n_state),
        grid=(nb, nc),
        in_specs=[row(d_inner), row(xbc_w), row(LANES), full(conv_w), full(cb), full(dtb), full(alog),
                  full(dexp), full(ng)],
        out_specs=row(d_inner),
        out_shape=jax.ShapeDtypeStruct((t, d_inner), BF16),
        scratch_shapes=[pltpu.VMEM((8 + q, xbc_w), F32), pltpu.VMEM((q, xbc_w), F32),
                        pltpu.VMEM((q, d_inner), F32), pltpu.VMEM((n_heads // 2, n_state, LANES), F32)],
        compiler_params=_cparams(("arbitrary", "arbitrary"), 32),
        name="ssd",
    )(z, xbc, dtr, conv_w, cb, dtb, alog, dexp, ng)


def _pool_kernel(u_ref, w_ref, sc_ref, o_ref, buf, *, r, gd):
    c = pl.program_id(1)
    hist = 16
    width = buf.shape[1]

    @pl.when(c == 0)
    def _():
        buf[0:hist, :] = jnp.zeros((hist, width), F32)

    buf[hist:hist + r, :] = u_ref[...]
    pos = c * r + lax.broadcasted_iota(jnp.int32, (r, 1), 0) - PAD_ROWS
    for gi, win in enumerate(POOL_WINDOWS):
        sl = slice(gi * gd, (gi + 1) * gd)
        u = buf[hist:hist + r, sl]
        s = u
        for j in range(1, win):
            s = s + buf[hist - j:hist - j + r, sl]
        cnt = jnp.clip(pos + 1, 1, win).astype(F32)
        mixed = s / cnt - u
        y = jnp.dot(mixed.astype(BF16), w_ref[gi].astype(BF16), preferred_element_type=F32)
        o_ref[:, sl] = (y * sc_ref[:, sl]).astype(BF16)
    buf[0:hist, :] = buf[r:r + hist, :]


def _pool(pc, pool_w, scale, nb, lp):
    t = pc.shape[0]
    ng, gd, _ = pool_w.shape
    width = ng * gd
    assert ng == len(POOL_WINDOWS) and gd % LANES == 0
    r = _pick(lp, (384, 192, 128, 64))
    nc = lp // r
    return pl.pallas_call(
        functools.partial(_pool_kernel, r=r, gd=gd),
        grid=(nb, nc),
        in_specs=[pl.BlockSpec((r, width), lambda b, c: (b * nc + c, 0)),
                  pl.BlockSpec(pool_w.shape, lambda b, c: (0, 0, 0)),
                  pl.BlockSpec((1, width), lambda b, c: (0, 0))],
        out_specs=pl.BlockSpec((r, width), lambda b, c: (b * nc + c, 0)),
        out_shape=jax.ShapeDtypeStruct((t, width), BF16),
        scratch_shapes=[pltpu.VMEM((16 + r, width), F32)],
        compiler_params=_cparams(("arbitrary", "arbitrary"), 32),
        name="pool",
    )(pc, pool_w, scale.reshape(1, width))


def _conf_kernel(a_ref, g_ref, w_ref, b_ref, lg_ref, lb_ref, o_ref, vbuf, cv, *, r):
    c = pl.program_id(1)
    hist = 32
    width = vbuf.shape[1]
    kc = w_ref.shape[0]

    @pl.when(c == 0)
    def _():
        vbuf[0:hist, :] = jnp.zeros((hist, width), F32)

    vbuf[hist:hist + r, :] = a_ref[...] * _sigmoid(g_ref[...])
    cw = 256
    for j in range(width // cw):
        sl = slice(j * cw, (j + 1) * cw)
        acc = b_ref[:, sl] + w_ref[kc - 1:kc, sl] * vbuf[hist:hist + r, sl]
        for k in range(kc - 1):
            off = hist - (kc - 1) + k
            acc = acc + w_ref[k:k + 1, sl] * vbuf[off:off + r, sl]
        cv[:, sl] = acc
    vbuf[0:hist, :] = vbuf[r:r + hist, :]
    v = _ln(cv[...], lg_ref[...], lb_ref[...])
    o_ref[...] = (v * _sigmoid(v)).astype(BF16)


def _conf(pc, col_blk, dw_w, dw_b, ln_g, ln_b, nb, lp):
    t = pc.shape[0]
    kc, width = dw_w.shape
    assert kc <= 33 and width % 256 == 0
    r = _pick(lp, (192, 128, 64))
    nc = lp // r
    vec = lambda: pl.BlockSpec((1, width), lambda b, c: (0, 0))
    return pl.pallas_call(
        functools.partial(_conf_kernel, r=r),
        grid=(nb, nc),
        in_specs=[pl.BlockSpec((r, width), lambda b, c: (b * nc + c, col_blk)),
                  pl.BlockSpec((r, width), lambda b, c: (b * nc + c, col_blk + 1)),
                  pl.BlockSpec((kc, width), lambda b, c: (0, 0)), vec(), vec(), vec()],
        out_specs=pl.BlockSpec((r, width), lambda b, c: (b * nc + c, 0)),
        out_shape=jax.ShapeDtypeStruct((t, width), BF16),
        scratch_shapes=[pltpu.VMEM((32 + r, width), F32), pltpu.VMEM((r, width), F32)],
        compiler_params=_cparams(("arbitrary", "arbitrary"), 32),
        name="conf",
    )(pc, pc, dw_w, dw_b.reshape(1, width), ln_g.reshape(1, width), ln_b.reshape(1, width))


def _merge_kernel(hb_ref, f0_ref, f1_ref, f2_ref, g0_ref, g1_ref, g2_ref, gb_ref, p0_ref, p1_ref, p2_ref,
                  o_ref, gs, s0, s1, s2):
    @pl.when(pl.program_id(1) == 0)
    def _():
        gs[0] = g0_ref[...].astype(BF16)
        gs[1] = g1_ref[...].astype(BF16)
        gs[2] = g2_ref[...].astype(BF16)
        s0[...] = p0_ref[...].astype(BF16)
        s1[...] = p1_ref[...].astype(BF16)
        s2[...] = p2_ref[...].astype(BF16)

    hb = hb_ref[...]
    acc = None
    for j, (f_ref, s_ref) in enumerate(((f0_ref, s0), (f1_ref, s1), (f2_ref, s2))):
        gate = _sigmoid(jnp.dot(hb, gs[j], preferred_element_type=F32) + gb_ref[j:j + 1, :])
        term = gate * jnp.dot(f_ref[...], s_ref[...], preferred_element_type=F32)
        acc = term if acc is None else acc + term
    o_ref[...] = acc.astype(BF16)


def _merge(hb, f_ssd, f_pool, f_conf, gate_w, gate_b, ssd_proj, pool_proj, conf_proj, layer):
    t, d = hb.shape
    bn = _pick(d, (512, 256, 128))
    bm = _pick(t, (528, 384, 192, 128, 64, 8))
    nbn = d // bn
    once = pl.Buffered(1)
    rowspec = lambda a: pl.BlockSpec((bm, a.shape[1]), lambda n, m: (m, 0))
    gspec = lambda j: pl.BlockSpec((None, d, bn), lambda n, m: (layer, 0, j * nbn + n), pipeline_mode=once)
    pspec = lambda w: pl.BlockSpec((None, w.shape[1], bn), lambda n, m: (layer, 0, n), pipeline_mode=once)
    gb = gate_b[layer].reshape(3, d)
    return pl.pallas_call(
        _merge_kernel,
        grid=(nbn, t // bm),
        in_specs=[rowspec(hb), rowspec(f_ssd), rowspec(f_pool), rowspec(f_conf),
                  gspec(0), gspec(1), gspec(2), pl.BlockSpec((3, bn), lambda n, m: (0, n)),
                  pspec(ssd_proj), pspec(pool_proj), pspec(conf_proj)],
        out_specs=pl.BlockSpec((bm, bn), lambda n, m: (m, n)),
        out_shape=jax.ShapeDtypeStruct((t, d), BF16),
        scratch_shapes=[pltpu.VMEM((3, d, bn), BF16), pltpu.VMEM((ssd_proj.shape[1], bn), BF16),
                        pltpu.VMEM((pool_proj.shape[1], bn), BF16), pltpu.VMEM((conf_proj.shape[1], bn), BF16)],
        compiler_params=_cparams(("arbitrary", "arbitrary"), 52),
        name="merge",
    )(hb, f_ssd, f_pool, f_conf, gate_w, gate_w, gate_w, gb, ssd_proj, pool_proj, conf_proj)


def _router_kernel(h_ref, w_ref, b_ref, o_ref, cnt_ref, run, *, bm, lp, n_groups, per_group):
    @pl.when(pl.program_id(0) == 0)
    def _():
        run[...] = jnp.zeros(run.shape, F32)

    logits = jnp.dot(h_ref[...], w_ref[...], preferred_element_type=F32,
                     precision=lax.Precision.HIGHEST) + b_ref[...]
    lane = lax.broadcasted_iota(jnp.int32, logits.shape, 1).astype(F32)

    def first_max(vals):
        m = jnp.max(vals, axis=1, keepdims=True)
        idx = jnp.min(jnp.where(vals == m, lane, float(LANES)), axis=1, keepdims=True)
        return m, idx

    gmask = lane < n_groups
    gmax, gsel = first_max(jnp.where(gmask, logits, NEG_BIG))
    gsum = jnp.sum(jnp.where(gmask, jnp.exp(logits - gmax), 0.0), axis=1, keepdims=True)
    p_group = 1.0 / gsum
    lo = n_groups + per_group * gsel
    el = jnp.where(jnp.logical_and(lane >= lo, lane < lo + per_group), logits, NEG_BIG)
    m1, i1 = first_max(el)
    m2, i2 = first_max(jnp.where(lane == i1, NEG_BIG, el))
    ratio = jnp.exp(m2 - m1)
    w1 = p_group / (1.0 + ratio)
    w2 = w1 * ratio
    real = _rows_in_batch(pl.program_id(0), bm, lp) >= PAD_ROWS
    sentinel = float(n_groups * per_group)
    e1 = jnp.where(real, i1 - n_groups, sentinel)
    e2 = jnp.where(real, i2 - n_groups, sentinel)
    w1 = jnp.where(real, w1, 0.0)
    w2 = jnp.where(real, w2, 0.0)
    hit1 = jnp.logical_and(lane == i1, real)
    hit2 = jnp.logical_and(lane == i2, real)
    onehot = jnp.where(jnp.logical_or(hit1, hit2), 1.0, 0.0)
    li = lax.broadcasted_iota(jnp.int32, (bm, bm), 0)
    si = lax.broadcasted_iota(jnp.int32, (bm, bm), 1)
    before = jnp.where(li > si, 1.0, 0.0).astype(BF16)
    seen = jnp.dot(before, onehot.astype(BF16), preferred_element_type=F32) + run[...]
    r1 = jnp.sum(jnp.where(hit1, seen, 0.0), axis=1, keepdims=True)
    r2 = jnp.sum(jnp.where(hit2, seen, 0.0), axis=1, keepdims=True)
    run[...] = run[...] + jnp.sum(onehot, axis=0, keepdims=True)
    cnt_ref[...] = run[...]
    out = jnp.zeros(logits.shape, F32)
    for k, v in enumerate((w1, w2, e1, e2, r1, r2)):
        out = jnp.where(lane == k, v, out)
    o_ref[...] = out


def _router(h, rw, rb, lp, n_groups, per_group):
    t, d = h.shape
    bm = _pick(lp, (528, 384, 192, 128, 64, 8))
    return pl.pallas_call(
        functools.partial(_router_kernel, bm=bm, lp=lp, n_groups=n_groups, per_group=per_group),
        grid=(t // bm,),
        in_specs=[pl.BlockSpec((bm, d), lambda i: (i, 0)), pl.BlockSpec((d, LANES), lambda i: (0, 0)),
                  pl.BlockSpec((1, LANES), lambda i: (0, 0))],
        out_specs=[pl.BlockSpec((bm, LANES), lambda i: (i, 0)), pl.BlockSpec((1, LANES), lambda i: (0, 0))],
        out_shape=[jax.ShapeDtypeStruct((t, LANES), F32), jax.ShapeDtypeStruct((1, LANES), F32)],
        scratch_shapes=[pltpu.VMEM((1, LANES), F32)],
        compiler_params=_cparams(("arbitrary",), 40),
        name="router",
    )(h, rw, rb)


def _moe_kernel(be_ref, nxt_ref, tot_ref, x_ref, wg_hbm, wu_hbm, wd_hbm, o_ref,
                wg_f, wu_f, wd_f, wg_s, wu_s, wd_s, sems, *, layer):
    i = pl.program_id(0)
    total = tot_ref[0]
    ic = jnp.maximum(jnp.minimum(i, total - 1), 0)
    e = be_ref[ic]
    e_prev = be_ref[jnp.maximum(ic - 1, 0)]
    active = i < total
    first_of_run = jnp.logical_and(active, jnp.logical_or(i == 0, e != e_prev))

    def weight_copies(expert):
        return (pltpu.make_async_copy(wg_hbm.at[layer, expert], wg_f, sems.at[0]),
                pltpu.make_async_copy(wu_hbm.at[layer, expert], wu_f, sems.at[1]),
                pltpu.make_async_copy(wd_hbm.at[layer, expert], wd_f, sems.at[2]))

    @pl.when(jnp.logical_and(active, i == 0))
    def _():
        for cp in weight_copies(e):
            cp.start()

    @pl.when(first_of_run)
    def _():
        for cp, src, dst in zip(weight_copies(e), (wg_f, wu_f, wd_f), (wg_s, wu_s, wd_s)):
            cp.wait()
            dst[...] = src[...].astype(BF16)
        nxt = nxt_ref[ic]

        @pl.when(nxt >= 0)
        def _():
            for cp in weight_copies(nxt):
                cp.start()

    @pl.when(active)
    def _():
        x = x_ref[...].astype(BF16)
        gte = jnp.dot(x, wg_s[...], preferred_element_type=F32)
        up = jnp.dot(x, wu_s[...], preferred_element_type=F32)
        hid = (gte * _sigmoid(gte)) * up
        o_ref[...] = jnp.dot(hid.astype(BF16), wd_s[...], preferred_element_type=F32)

    @pl.when(jnp.logical_not(active))
    def _():
        o_ref[...] = jnp.zeros(o_ref.shape, F32)


def _moe_ffn(xg, block_expert, next_expert, total_blocks, w_gate, w_up, w_down, layer):
    n_slots, d = xg.shape
    hid = w_gate.shape[-1]
    n_blocks = n_slots // MOE_BLOCK

    def rows(i, be, nxt, tot):
        return (jnp.maximum(jnp.minimum(i, tot[0] - 1), 0), 0)

    hbm = pl.BlockSpec(memory_space=pl.ANY)
    grid_spec = pltpu.PrefetchScalarGridSpec(
        num_scalar_prefetch=3,
        grid=(n_blocks,),
        in_specs=[pl.BlockSpec((MOE_BLOCK, d), rows), hbm, hbm, hbm],
        out_specs=pl.BlockSpec((MOE_BLOCK, d), lambda i, be, nxt, tot: (i, 0)),
        scratch_shapes=[pltpu.VMEM((d, hid), F32), pltpu.VMEM((d, hid), F32), pltpu.VMEM((hid, d), F32),
                        pltpu.VMEM((d, hid), BF16), pltpu.VMEM((d, hid), BF16), pltpu.VMEM((hid, d), BF16),
                        pltpu.SemaphoreType.DMA((3,))],
    )
    return pl.pallas_call(
        functools.partial(_moe_kernel, layer=layer),
        grid_spec=grid_spec,
        out_shape=jax.ShapeDtypeStruct((n_slots, d), F32),
        compiler_params=_cparams(("arbitrary",), 48),
        name="moe_ffn",
    )(block_expert, next_expert, total_blocks, xg, w_gate, w_up, w_down)


def _dispatch(ew, cnt, n_groups, n_experts):
    t = ew.shape[0]
    eid = ew[:, 2:4].astype(jnp.int32).reshape(-1)
    rank = ew[:, 4:6].astype(jnp.int32).reshape(-1)
    counts = cnt[0, n_groups:n_groups + n_experts].astype(jnp.int32)
    n_assign = eid.shape[0]
    tok = jnp.repeat(jnp.arange(t, dtype=jnp.int32), 2)
    padded = (counts + MOE_BLOCK - 1) // MOE_BLOCK * MOE_BLOCK
    pend = jnp.cumsum(padded)
    pstart = pend - padded
    n_blocks = n_assign // MOE_BLOCK + n_experts
    n_slots = n_blocks * MOE_BLOCK
    valid = eid < n_experts
    slot = jnp.where(valid, pstart[jnp.minimum(eid, n_experts - 1)] + rank, n_slots)
    slot_tok = jnp.zeros((n_slots,), jnp.int32).at[slot].set(tok, mode="drop")
    block_expert = jnp.minimum(
        jnp.searchsorted(pend, jnp.arange(n_blocks, dtype=jnp.int32) * MOE_BLOCK, side="right"),
        n_experts - 1).astype(jnp.int32)
    total_blocks = (pend[-1] // MOE_BLOCK).astype(jnp.int32).reshape(1)
    ids = jnp.arange(n_experts, dtype=jnp.int32)
    later = jnp.where(counts > 0, ids, n_experts)
    nxt = lax.cummin(jnp.concatenate([later[1:], jnp.full((1,), n_experts, jnp.int32)]), reverse=True)
    nxt = jnp.where(nxt >= n_experts, -1, nxt)
    next_expert = nxt[block_expert]
    slot_c = jnp.where(valid, slot, 0).reshape(t, 2)
    return slot_tok, block_expert, next_expert, total_blocks, slot_c


def kernel(x, meta_tokens, ln_emb_g, ln_emb_b, w_in, ssd_conv_w, ssd_conv_b, ssd_dt_bias, ssd_a_log, ssd_d,
           ssd_norm_g, ssd_proj, pool_w, pool_scale, pool_proj, conf_dw_w, conf_dw_b, conf_ln_g, conf_ln_b,
           conf_proj, gate_w, gate_b, w_out, ln1_g, ln1_b, router_group_w, router_group_b, router_expert_w,
           router_expert_b, exp_w_gate, exp_w_up, exp_w_down, ln2_g, ln2_b):
    nb, seq, d = x.shape
    n_meta = meta_tokens.shape[0]
    depth = w_in.shape[0]
    lp = PAD_ROWS + n_meta + seq
    t = nb * lp
    alpha = (2.0 * depth) ** 0.25
    d_inner = ssd_norm_g.shape[-1]
    xbc_w = ssd_conv_w.shape[-1]
    n_heads = ssd_a_log.shape[-1]
    pool_width = pool_scale.shape[-1]
    conf_width = conf_dw_b.shape[-1]
    n_groups = router_group_w.shape[-1]
    n_experts = router_expert_w.shape[-1]
    per_group = n_experts // n_groups
    assert (PAD_ROWS + n_meta) % SSD_CHUNK == 0 and lp % SSD_CHUNK == 0
    assert n_groups + n_experts <= LANES and n_heads <= LANES

    meta = jnp.broadcast_to(meta_tokens[None].astype(x.dtype), (nb, n_meta, d))
    xcat = jnp.concatenate([jnp.zeros((nb, PAD_ROWS, d), x.dtype), meta, x], axis=1).reshape(t, d)
    h, hb = _embed_ln(xcat, ln_emb_g, ln_emb_b, lp)

    c_dt = d_inner + xbc_w
    c_pc = c_dt + n_heads
    w_dt = jnp.pad(w_in[:, :, c_dt:c_pc], ((0, 0), (0, 0), (0, LANES - n_heads)))
    w_pc = w_in[:, :, c_pc:]
    rw = jnp.concatenate([router_group_w, router_expert_w], axis=-1)
    rw = jnp.pad(rw, ((0, 0), (0, 0), (0, LANES - rw.shape[-1])))
    rb = jnp.concatenate([router_group_b, router_expert_b], axis=-1)
    rb = jnp.pad(rb, ((0, 0), (0, LANES - rb.shape[-1]))).reshape(depth, 1, LANES)

    for i in range(depth):
        z = _mm(hb, w_in, i, 0, d_inner)
        xbc = _mm(hb, w_in, i, d_inner, xbc_w)
        dtr = _mm(hb, w_dt, i, 0, LANES)
        pc = _mm(hb, w_pc, i, 0, pool_width + 2 * conf_width)
        f_ssd = _ssd(z, xbc, dtr, ssd_conv_w[i], ssd_conv_b[i], ssd_dt_bias[i], ssd_a_log[i], ssd_d[i],
                     ssd_norm_g[i], nb, lp)
        f_pool = _pool(pc, pool_w[i], pool_scale[i], nb, lp)
        assert pool_width % conf_width == 0
        f_conf = _conf(pc, pool_width // conf_width, conf_dw_w[i], conf_dw_b[i], conf_ln_g[i], conf_ln_b[i], nb, lp)
        merged = _merge(hb, f_ssd, f_pool, f_conf, gate_w, gate_b, ssd_proj, pool_proj, conf_proj, i)
        h = _mm_ln(merged, w_out, i, h, ln1_g[i], ln1_b[i], lp, alpha)

        ew, cnt = _router(h, rw[i], rb[i], lp, n_groups, per_group)
        slot_tok, block_expert, next_expert, total_blocks, slot_c = _dispatch(ew, cnt, n_groups, n_experts)
        xg = h.at[slot_tok].get(mode="promise_in_bounds")
        out = _moe_ffn(xg, block_expert, next_expert, total_blocks, exp_w_gate, exp_w_up, exp_w_down, i)
        g0 = out.at[slot_c[:, 0]].get(mode="promise_in_bounds")
        g1 = out.at[slot_c[:, 1]].get(mode="promise_in_bounds")
        h, hb = _combine_ln(h, g0, g1, ew, ln2_g[i], ln2_b[i], lp, alpha)

    return h.reshape(nb, lp, d)[:, PAD_ROWS + n_meta:]
```

```python
import functools

import jax
import jax.numpy as jnp
from jax import lax
from jax.experimental import pallas as pl
from jax.experimental.pallas import tpu as pltpu

F32 = jnp.float32
BF16 = jnp.bfloat16

PAD_ROWS = 112
SSD_CHUNK = 128
SSD_GROUPS = 4
SSD_HEAD_DIM = 64
POOL_WINDOWS = (2, 4, 8, 16)
MOE_BLOCK = 256
LN_EPS = 1e-5
LANES = 128
NEG_BIG = -1e30
LOG2E = 1.4426950408889634


def _cparams(sem, vmem_mb):
    return pltpu.CompilerParams(dimension_semantics=sem, vmem_limit_bytes=vmem_mb << 20)


def _sigmoid(x):
    return 1.0 / (1.0 + jnp.exp(-x))


def _softplus(x):
    return jnp.maximum(x, 0.0) + jnp.log1p(jnp.exp(-jnp.abs(x)))


def _ln(x, g, b):
    mu = jnp.mean(x, -1, keepdims=True)
    xc = x - mu
    var = jnp.mean(xc * xc, -1, keepdims=True)
    return xc * lax.rsqrt(var + LN_EPS) * g + b


def _rows_in_batch(blk, bm, lp):
    base = lax.rem(blk, lp // bm) * bm
    return base + lax.broadcasted_iota(jnp.int32, (bm, 1), 0)


def _pick(n, cands):
    for c in cands:
        if n % c == 0:
            return c
    raise ValueError(f"no block size for {n} among {cands}")


def _embed_kernel(x_ref, g_ref, b_ref, h_ref, hb_ref, *, bm, lp):
    y = _ln(x_ref[...], g_ref[...], b_ref[...])
    y = jnp.where(_rows_in_batch(pl.program_id(0), bm, lp) >= PAD_ROWS, y, 0.0)
    h_ref[...] = y
    hb_ref[...] = y.astype(BF16)


def _embed_ln(xcat, g, b, lp):
    t, d = xcat.shape
    bm = _pick(lp, (528, 384, 192, 128, 64, 8))
    row = pl.BlockSpec((bm, d), lambda i: (i, 0))
    vec = pl.BlockSpec((1, d), lambda i: (0, 0))
    return pl.pallas_call(
        functools.partial(_embed_kernel, bm=bm, lp=lp),
        grid=(t // bm,),
        in_specs=[row, vec, vec],
        out_specs=[row, row],
        out_shape=[jax.ShapeDtypeStruct((t, d), F32), jax.ShapeDtypeStruct((t, d), BF16)],
        compiler_params=_cparams(("parallel",), 40),
        name="embed_ln",
    )(xcat, g.reshape(1, d), b.reshape(1, d))


def _combine_ln_kernel(h_ref, g0_ref, g1_ref, ew_ref, g_ref, b_ref, o_ref, ob_ref, *, bm, lp, alpha):
    y = ew_ref[:, 0:1] * g0_ref[...] + ew_ref[:, 1:2] * g1_ref[...]
    y = _ln(alpha * h_ref[...] + y, g_ref[...], b_ref[...])
    y = jnp.where(_rows_in_batch(pl.program_id(0), bm, lp) >= PAD_ROWS, y, 0.0)
    o_ref[...] = y
    ob_ref[...] = y.astype(BF16)


def _combine_ln(h, g0, g1, ew, g, b, lp, alpha):
    t, d = h.shape
    bm = _pick(lp, (528, 384, 192, 128, 64, 8))
    row = pl.BlockSpec((bm, d), lambda i: (i, 0))
    vec = pl.BlockSpec((1, d), lambda i: (0, 0))
    return pl.pallas_call(
        functools.partial(_combine_ln_kernel, bm=bm, lp=lp, alpha=alpha),
        grid=(t // bm,),
        in_specs=[row, row, row, pl.BlockSpec((bm, LANES), lambda i: (i, 0)), vec, vec],
        out_specs=[row, row],
        out_shape=[jax.ShapeDtypeStruct((t, d), F32), jax.ShapeDtypeStruct((t, d), BF16)],
        compiler_params=_cparams(("parallel",), 56),
        name="combine_ln",
    )(h, g0, g1, ew, g.reshape(1, d), b.reshape(1, d))


def _mm_kernel(x_ref, w_ref, o_ref, wbf_ref):
    @pl.when(pl.program_id(1) == 0)
    def _():
        wbf_ref[...] = w_ref[...].astype(BF16)

    o_ref[...] = jnp.dot(x_ref[...], wbf_ref[...], preferred_element_type=F32).astype(o_ref.dtype)


def _mm(x, w, layer, col0, n_cols, out_dtype=F32):
    t, k = x.shape
    bn = _pick(n_cols, (1024, 512, 256, 128))
    bm = _pick(t, (1056, 528, 384, 192, 128, 64, 8))
    assert col0 % bn == 0
    cb0 = col0 // bn
    return pl.pallas_call(
        _mm_kernel,
        grid=(n_cols // bn, t // bm),
        in_specs=[pl.BlockSpec((bm, k), lambda n, m: (m, 0)),
                  pl.BlockSpec((None, k, bn), lambda n, m: (layer, 0, cb0 + n))],
        out_specs=pl.BlockSpec((bm, bn), lambda n, m: (m, n)),
        out_shape=jax.ShapeDtypeStruct((t, n_cols), out_dtype),
        scratch_shapes=[pltpu.VMEM((k, bn), BF16)],
        compiler_params=_cparams(("arbitrary", "arbitrary"), 48),
        name="mm",
    )(x, w)


def _mm_ln_kernel(x_ref, w_ref, h_ref, g_ref, b_ref, o_ref, wbf_ref, *, bm, lp, alpha):
    @pl.when(pl.program_id(0) == 0)
    def _():
        wbf_ref[...] = w_ref[...].astype(BF16)

    y = jnp.dot(x_ref[...], wbf_ref[...], preferred_element_type=F32)
    y = _ln(alpha * h_ref[...] + y, g_ref[...], b_ref[...])
    o_ref[...] = jnp.where(_rows_in_batch(pl.program_id(0), bm, lp) >= PAD_ROWS, y, 0.0)


def _mm_ln(x, w, layer, h, g, b, lp, alpha):
    t, k = x.shape
    d = w.shape[-1]
    bm = _pick(lp, (528, 384, 192, 128, 64, 8))
    row = lambda width: pl.BlockSpec((bm, width), lambda i: (i, 0))
    vec = pl.BlockSpec((1, d), lambda i: (0, 0))
    return pl.pallas_call(
        functools.partial(_mm_ln_kernel, bm=bm, lp=lp, alpha=alpha),
        grid=(t // bm,),
        in_specs=[row(k), pl.BlockSpec((None, k, d), lambda i: (layer, 0, 0), pipeline_mode=pl.Buffered(1)),
                  row(d), vec, vec],
        out_specs=row(d),
        out_shape=jax.ShapeDtypeStruct((t, d), F32),
        scratch_shapes=[pltpu.VMEM((k, d), BF16)],
        compiler_params=_cparams(("arbitrary",), 56),
        name="mm_ln",
    )(x, w, h, g.reshape(1, d), b.reshape(1, d))


def _ssd_kernel(z_ref, xbc_ref, dtr_ref, cw_ref, cb_ref, dtb_ref, alog_ref, dexp_ref, ng_ref, o_ref,
                cbuf, xc, ybuf, st, *, q, d_inner, n_state):
    c = pl.program_id(1)
    xbc_w = cbuf.shape[1]
    kc = cw_ref.shape[0]
    hist = 8

    @pl.when(c == 0)
    def _():
        cbuf[0:hist, :] = jnp.zeros((hist, xbc_w), F32)
        st[...] = jnp.zeros(st.shape, F32)

    cbuf[hist:hist + q, :] = xbc_ref[...]
    cw = 128
    for j in range(xbc_w // cw):
        sl = slice(j * cw, (j + 1) * cw)
        conv = cb_ref[:, sl] + cw_ref[kc - 1:kc, sl] * cbuf[hist:hist + q, sl]
        for k in range(kc - 1):
            off = hist - (kc - 1) + k
            conv = conv + cw_ref[k:k + 1, sl] * cbuf[off:off + q, sl]
        xc[:, sl] = conv * _sigmoid(conv)
    cbuf[0:hist, :] = cbuf[q:q + hist, :]

    row = c * q + lax.broadcasted_iota(jnp.int32, (q, 1), 0)
    dt = _softplus(dtr_ref[...] + dtb_ref[...])
    dt = jnp.where(row >= PAD_ROWS, dt, 0.0)
    adt = dt * (-jnp.exp(alog_ref[...]))
    li = lax.broadcasted_iota(jnp.int32, (q, q), 0)
    si = lax.broadcasted_iota(jnp.int32, (q, q), 1)
    tri = li >= si
    acs = jnp.dot(tri.astype(F32), adt, preferred_element_type=F32,
                  precision=lax.Precision.HIGHEST) * LOG2E
    acs_t = acs.T
    dt_t = dt.T
    e_acs = jnp.exp2(acs)
    last = acs_t[:, q - 1:q]
    w_t = jnp.exp2(last - acs_t) * dt_t
    dlast = jnp.exp2(last)
    lane = lax.broadcasted_iota(jnp.int32, (1, LANES), 1)
    lo_half = lane < SSD_HEAD_DIM

    heads_per_group = d_inner // SSD_HEAD_DIM // SSD_GROUPS
    pairs_per_group = heads_per_group // 2
    for g in range(SSD_GROUPS):
        b0 = d_inner + g * n_state
        c0 = d_inner + SSD_GROUPS * n_state + g * n_state
        bg = xc[:, b0:b0 + n_state]
        cg = xc[:, c0:c0 + n_state]
        cbm = lax.dot_general(cg.astype(BF16), bg.astype(BF16), (((1,), (1,)), ((), ())),
                              preferred_element_type=F32)
        bg_t = bg.T
        for j in range(pairs_per_group):
            pair = g * pairs_per_group + j
            h0 = 2 * pair
            xs_pair = xc[:, h0 * SSD_HEAD_DIM:h0 * SSD_HEAD_DIM + LANES]
            st_pair = st[pair]
            lhs_y, lhs_s, rhs_x, rhs_st = [], [], [], []
            for u in range(2):
                h = h0 + u
                col = acs[:, h:h + 1]
                rw = acs_t[h:h + 1, :]
                decay = jnp.where(tri, jnp.exp2(col - rw), 0.0)
                lhs_y.append((cbm * decay * dt_t[h:h + 1, :]).astype(BF16))
                lhs_s.append((bg_t * w_t[h:h + 1, :]).astype(BF16))
                keep = lo_half if u == 0 else jnp.logical_not(lo_half)
                rhs_x.append(jnp.where(keep, xs_pair, 0.0).astype(BF16))
                rhs_st.append(jnp.where(keep, st_pair, 0.0).astype(BF16))
            for u in range(2):
                lhs_y.append((e_acs[:, h0 + u:h0 + u + 1] * cg).astype(BF16))
            y_pair = jnp.dot(jnp.concatenate(lhs_y, axis=1), jnp.concatenate(rhs_x + rhs_st, axis=0),
                             preferred_element_type=F32)
            s_new = jnp.dot(jnp.concatenate(lhs_s, axis=1), jnp.concatenate(rhs_x, axis=0),
                            preferred_element_type=F32)
            dpair = jnp.where(lo_half, dlast[h0:h0 + 1, :], dlast[h0 + 1:h0 + 2, :])
            st[pair] = st_pair * dpair + s_new
            ybuf[:, h0 * SSD_HEAD_DIM:h0 * SSD_HEAD_DIM + LANES] = y_pair

    gw = d_inner // SSD_GROUPS
    for g in range(SSD_GROUPS):
        sl = slice(g * gw, (g + 1) * gw)
        y = ybuf[:, sl] + xc[:, sl] * dexp_ref[:, sl]
        zz = z_ref[:, sl]
        y = y * (zz * _sigmoid(zz))
        y = y * lax.rsqrt(jnp.mean(y * y, -1, keepdims=True) + LN_EPS)
        o_ref[:, sl] = (y * ng_ref[:, sl]).astype(BF16)


def _ssd(z, xbc, dtr, conv_w, conv_b, dt_bias, a_log, d_skip, norm_g, nb, lp):
    t, d_inner = z.shape
    xbc_w = xbc.shape[1]
    q = SSD_CHUNK
    n_heads = d_inner // SSD_HEAD_DIM
    n_state = (xbc_w - d_inner) // (2 * SSD_GROUPS)
    assert n_state == LANES and n_heads % (2 * SSD_GROUPS) == 0 and lp % q == 0
    assert xbc_w % 512 == 0 and PAD_ROWS % 8 == 0
    nc = lp // q
    padh = LANES - n_heads
    row = lambda w: pl.BlockSpec((q, w), lambda b, c: (b * nc + c, 0))
    full = lambda a: pl.BlockSpec(a.shape, lambda b, c: (0,) * a.ndim)
    dtb = jnp.pad(dt_bias, (0, padh)).reshape(1, LANES)
    alog = jnp.pad(a_log, (0, padh)).reshape(1, LANES)
    dexp = jnp.repeat(d_skip, SSD_HEAD_DIM).reshape(1, d_inner)
    cb = conv_b.reshape(1, xbc_w)
    ng = norm_g.reshape(1, d_inner)
    return pl.pallas_call(
        functools.partial(_ssd_kernel, q=q, d_inner=d_inner, n_state=n_state),
        grid=(nb, nc),
        in_specs=[row(d_inner), row(xbc_w), row(LANES), full(conv_w), full(cb), full(dtb), full(alog),
                  full(dexp), full(ng)],
        out_specs=row(d_inner),
        out_shape=jax.ShapeDtypeStruct((t, d_inner), BF16),
        scratch_shapes=[pltpu.VMEM((8 + q, xbc_w), F32), pltpu.VMEM((q, xbc_w), F32),
                        pltpu.VMEM((q, d_inner), F32), pltpu.VMEM((n_heads // 2, n_state, LANES), F32)],
        compiler_params=_cparams(("arbitrary", "arbitrary"), 32),
        name="ssd",
    )(z, xbc, dtr, conv_w, cb, dtb, alog, dexp, ng)


def _pool_kernel(u_ref, w_ref, sc_ref, o_ref, buf, *, r, gd):
    c = pl.program_id(1)
    hist = 16
    width = buf.shape[1]

    @pl.when(c == 0)
    def _():
        buf[0:hist, :] = jnp.zeros((hist, width), F32)

    buf[hist:hist + r, :] = u_ref[...]
    pos = c * r + lax.broadcasted_iota(jnp.int32, (r, 1), 0) - PAD_ROWS
    for gi, win in enumerate(POOL_WINDOWS):
        sl = slice(gi * gd, (gi + 1) * gd)
        u = buf[hist:hist + r, sl]
        s = u
        for j in range(1, win):
            s = s + buf[hist - j:hist - j + r, sl]
        cnt = jnp.clip(pos + 1, 1, win).astype(F32)
        mixed = s / cnt - u
        y = jnp.dot(mixed.astype(BF16), w_ref[gi].astype(BF16), preferred_element_type=F32)
        o_ref[:, sl] = (y * sc_ref[:, sl]).astype(BF16)
    buf[0:hist, :] = buf[r:r + hist, :]


def _pool(pc, pool_w, scale, nb, lp):
    t = pc.shape[0]
    ng, gd, _ = pool_w.shape
    width = ng * gd
    assert ng == len(POOL_WINDOWS) and gd % LANES == 0
    r = _pick(lp, (384, 192, 128, 64))
    nc = lp // r
    return pl.pallas_call(
        functools.partial(_pool_kernel, r=r, gd=gd),
        grid=(nb, nc),
        in_specs=[pl.BlockSpec((r, width), lambda b, c: (b * nc + c, 0)),
                  pl.BlockSpec(pool_w.shape, lambda b, c: (0, 0, 0)),
                  pl.BlockSpec((1, width), lambda b, c: (0, 0))],
        out_specs=pl.BlockSpec((r, width), lambda b, c: (b * nc + c, 0)),
        out_shape=jax.ShapeDtypeStruct((t, width), BF16),
        scratch_shapes=[pltpu.VMEM((16 + r, width), F32)],
        compiler_params=_cparams(("arbitrary", "arbitrary"), 32),
        name="pool",
    )(pc, pool_w, scale.reshape(1, width))


def _conf_kernel(a_ref, g_ref, w_ref, b_ref, lg_ref, lb_ref, o_ref, vbuf, cv, ubuf, *, r):
    c = pl.program_id(1)
    hist = 32
    sub = 8
    width = vbuf.shape[1]
    kc = w_ref.shape[0]

    @pl.when(c == 0)
    def _():
        vbuf[0:hist, :] = jnp.zeros((hist, width), F32)

    vbuf[hist:hist + r, :] = a_ref[...] * _sigmoid(g_ref[...])
    ext = r + sub
    cw = ubuf.shape[2]
    for jc in range(width // cw):
        sl = slice(jc * cw, (jc + 1) * cw)
        acc = None
        for rr in range(sub):
            u = None
            for j in range((kc - 1 - rr) // sub + 1):
                k = kc - 1 - (sub * j + rr)
                lo = hist - sub - sub * j
                term = w_ref[k:k + 1, sl] * vbuf[lo:lo + ext, sl]
                u = term if u is None else u + term
            if rr == 0:
                acc = b_ref[:, sl] + u[sub:sub + r]
            else:
                ubuf[rr - 1] = u
                acc = acc + ubuf[rr - 1, sub - rr:sub - rr + r, :]
        cv[:, sl] = acc
    vbuf[0:hist, :] = vbuf[r:r + hist, :]
    v = _ln(cv[...], lg_ref[...], lb_ref[...])
    o_ref[...] = (v * _sigmoid(v)).astype(BF16)


def _conf(pc, col_blk, dw_w, dw_b, ln_g, ln_b, nb, lp):
    t = pc.shape[0]
    kc, width = dw_w.shape
    assert kc <= 33 and width % 256 == 0
    r = _pick(lp, (192, 128, 64))
    nc = lp // r
    vec = lambda: pl.BlockSpec((1, width), lambda b, c: (0, 0))
    return pl.pallas_call(
        functools.partial(_conf_kernel, r=r),
        grid=(nb, nc),
        in_specs=[pl.BlockSpec((r, width), lambda b, c: (b * nc + c, col_blk)),
                  pl.BlockSpec((r, width), lambda b, c: (b * nc + c, col_blk + 1)),
                  pl.BlockSpec((kc, width), lambda b, c: (0, 0)), vec(), vec(), vec()],
        out_specs=pl.BlockSpec((r, width), lambda b, c: (b * nc + c, 0)),
        out_shape=jax.ShapeDtypeStruct((t, width), BF16),
        scratch_shapes=[pltpu.VMEM((32 + r, width), F32), pltpu.VMEM((r, width), F32),
                        pltpu.VMEM((7, r + 8, LANES), F32)],
        compiler_params=_cparams(("arbitrary", "arbitrary"), 32),
        name="conf",
    )(pc, pc, dw_w, dw_b.reshape(1, width), ln_g.reshape(1, width), ln_b.reshape(1, width))


def _merge_kernel(hb_ref, f0_ref, f1_ref, f2_ref, g0_ref, g1_ref, g2_ref, gb_ref, p0_ref, p1_ref, p2_ref,
                  o_ref, gs, s0, s1, s2):
    @pl.when(pl.program_id(1) == 0)
    def _():
        gs[0] = g0_ref[...].astype(BF16)
        gs[1] = g1_ref[...].astype(BF16)
        gs[2] = g2_ref[...].astype(BF16)
        s0[...] = p0_ref[...].astype(BF16)
        s1[...] = p1_ref[...].astype(BF16)
        s2[...] = p2_ref[...].astype(BF16)

    hb = hb_ref[...]
    acc = None
    for j, (f_ref, s_ref) in enumerate(((f0_ref, s0), (f1_ref, s1), (f2_ref, s2))):
        gate = _sigmoid(jnp.dot(hb, gs[j], preferred_element_type=F32) + gb_ref[j:j + 1, :])
        term = gate * jnp.dot(f_ref[...], s_ref[...], preferred_element_type=F32)
        acc = term if acc is None else acc + term
    o_ref[...] = acc.astype(BF16)


def _merge(hb, f_ssd, f_pool, f_conf, gate_w, gate_b, ssd_proj, pool_proj, conf_proj, layer):
    t, d = hb.shape
    bn = _pick(d, (512, 256, 128))
    bm = _pick(t, (528, 384, 192, 128, 64, 8))
    nbn = d // bn
    once = pl.Buffered(1)
    rowspec = lambda a: pl.BlockSpec((bm, a.shape[1]), lambda n, m: (m, 0))
    gspec = lambda j: pl.BlockSpec((None, d, bn), lambda n, m: (layer, 0, j * nbn + n), pipeline_mode=once)
    pspec = lambda w: pl.BlockSpec((None, w.shape[1], bn), lambda n, m: (layer, 0, n), pipeline_mode=once)
    gb = gate_b[layer].reshape(3, d)
    return pl.pallas_call(
        _merge_kernel,
        grid=(nbn, t // bm),
        in_specs=[rowspec(hb), rowspec(f_ssd), rowspec(f_pool), rowspec(f_conf),
                  gspec(0), gspec(1), gspec(2), pl.BlockSpec((3, bn), lambda n, m: (0, n)),
                  pspec(ssd_proj), pspec(pool_proj), pspec(conf_proj)],
        out_specs=pl.BlockSpec((bm, bn), lambda n, m: (m, n)),
        out_shape=jax.ShapeDtypeStruct((t, d), BF16),
        scratch_shapes=[pltpu.VMEM((3, d, bn), BF16), pltpu.VMEM((ssd_proj.shape[1], bn), BF16),
                        pltpu.VMEM((pool_proj.shape[1], bn), BF16), pltpu.VMEM((conf_proj.shape[1], bn), BF16)],
        compiler_params=_cparams(("arbitrary", "arbitrary"), 52),
        name="merge",
    )(hb, f_ssd, f_pool, f_conf, gate_w, gate_w, gate_w, gb, ssd_proj, pool_proj, conf_proj)


def _router_kernel(h_ref, w_ref, b_ref, o_ref, cnt_ref, run, *, bm, lp, n_groups, per_group):
    @pl.when(pl.program_id(0) == 0)
    def _():
        run[...] = jnp.zeros(run.shape, F32)

    h = h_ref[...]
    w = w_ref[...]
    h_hi = h.astype(BF16)
    h_lo = (h - h_hi.astype(F32)).astype(BF16)
    w_hi = w.astype(BF16)
    w_lo = (w - w_hi.astype(F32)).astype(BF16)
    logits = (jnp.dot(h_hi, w_hi, preferred_element_type=F32)
              + (jnp.dot(h_hi, w_lo, preferred_element_type=F32)
                 + jnp.dot(h_lo, w_hi, preferred_element_type=F32))) + b_ref[...]
    lane = lax.broadcasted_iota(jnp.int32, logits.shape, 1).astype(F32)

    def first_max(vals):
        m = jnp.max(vals, axis=1, keepdims=True)
        idx = jnp.min(jnp.where(vals == m, lane, float(LANES)), axis=1, keepdims=True)
        return m, idx

    gmask = lane < n_groups
    gmax, gsel = first_max(jnp.where(gmask, logits, NEG_BIG))
    gsum = jnp.sum(jnp.where(gmask, jnp.exp(logits - gmax), 0.0), axis=1, keepdims=True)
    p_group = 1.0 / gsum
    lo = n_groups + per_group * gsel
    el = jnp.where(jnp.logical_and(lane >= lo, lane < lo + per_group), logits, NEG_BIG)
    m1, i1 = first_max(el)
    m2, i2 = first_max(jnp.where(lane == i1, NEG_BIG, el))
    ratio = jnp.exp(m2 - m1)
    w1 = p_group / (1.0 + ratio)
    w2 = w1 * ratio
    real = _rows_in_batch(pl.program_id(0), bm, lp) >= PAD_ROWS
    sentinel = float(n_groups * per_group)
    e1 = jnp.where(real, i1 - n_groups, sentinel)
    e2 = jnp.where(real, i2 - n_groups, sentinel)
    w1 = jnp.where(real, w1, 0.0)
    w2 = jnp.where(real, w2, 0.0)
    hit1 = jnp.logical_and(lane == i1, real)
    hit2 = jnp.logical_and(lane == i2, real)
    onehot = jnp.where(jnp.logical_or(hit1, hit2), 1.0, 0.0)
    li = lax.broadcasted_iota(jnp.int32, (bm, bm), 0)
    si = lax.broadcasted_iota(jnp.int32, (bm, bm), 1)
    before = jnp.where(li > si, 1.0, 0.0).astype(BF16)
    seen = jnp.dot(before, onehot.astype(BF16), preferred_element_type=F32) + run[...]
    r1 = jnp.sum(jnp.where(hit1, seen, 0.0), axis=1, keepdims=True)
    r2 = jnp.sum(jnp.where(hit2, seen, 0.0), axis=1, keepdims=True)
    run[...] = run[...] + jnp.sum(onehot, axis=0, keepdims=True)
    cnt_ref[...] = run[...]
    out = jnp.zeros(logits.shape, F32)
    for k, v in enumerate((w1, w2, e1, e2, r1, r2)):
        out = jnp.where(lane == k, v, out)
    o_ref[...] = out


def _router(h, rw, rb, lp, n_groups, per_group):
    t, d = h.shape
    bm = _pick(lp, (528, 384, 192, 128, 64, 8))
    return pl.pallas_call(
        functools.partial(_router_kernel, bm=bm, lp=lp, n_groups=n_groups, per_group=per_group),
        grid=(t // bm,),
        in_specs=[pl.BlockSpec((bm, d), lambda i: (i, 0)), pl.BlockSpec((d, LANES), lambda i: (0, 0)),
                  pl.BlockSpec((1, LANES), lambda i: (0, 0))],
        out_specs=[pl.BlockSpec((bm, LANES), lambda i: (i, 0)), pl.BlockSpec((1, LANES), lambda i: (0, 0))],
        out_shape=[jax.ShapeDtypeStruct((t, LANES), F32), jax.ShapeDtypeStruct((1, LANES), F32)],
        scratch_shapes=[pltpu.VMEM((1, LANES), F32)],
        compiler_params=_cparams(("arbitrary",), 40),
        name="router",
    )(h, rw, rb)


def _moe_kernel(be_ref, nxt_ref, tot_ref, x_ref, wg_hbm, wu_hbm, wd_hbm, o_ref,
                wg_f, wu_f, wd_f, wg_s, wu_s, wd_s, sems, *, layer):
    i = pl.program_id(0)
    total = tot_ref[0]
    ic = jnp.maximum(jnp.minimum(i, total - 1), 0)
    e = be_ref[ic]
    e_prev = be_ref[jnp.maximum(ic - 1, 0)]
    active = i < total
    first_of_run = jnp.logical_and(active, jnp.logical_or(i == 0, e != e_prev))

    def weight_copies(expert):
        return (pltpu.make_async_copy(wg_hbm.at[layer, expert], wg_f, sems.at[0]),
                pltpu.make_async_copy(wu_hbm.at[layer, expert], wu_f, sems.at[1]),
                pltpu.make_async_copy(wd_hbm.at[layer, expert], wd_f, sems.at[2]))

    @pl.when(jnp.logical_and(active, i == 0))
    def _():
        for cp in weight_copies(e):
            cp.start()

    @pl.when(first_of_run)
    def _():
        for cp, src, dst in zip(weight_copies(e), (wg_f, wu_f, wd_f), (wg_s, wu_s, wd_s)):
            cp.wait()
            dst[...] = src[...].astype(BF16)
        nxt = nxt_ref[ic]

        @pl.when(nxt >= 0)
        def _():
            for cp in weight_copies(nxt):
                cp.start()

    @pl.when(active)
    def _():
        x = x_ref[...].astype(BF16)
        gte = jnp.dot(x, wg_s[...], preferred_element_type=F32)
        up = jnp.dot(x, wu_s[...], preferred_element_type=F32)
        hid = (gte * _sigmoid(gte)) * up
        o_ref[...] = jnp.dot(hid.astype(BF16), wd_s[...], preferred_element_type=F32)

    @pl.when(jnp.logical_not(active))
    def _():
        o_ref[...] = jnp.zeros(o_ref.shape, F32)


def _moe_ffn(xg, block_expert, next_expert, total_blocks, w_gate, w_up, w_down, layer):
    n_slots, d = xg.shape
    hid = w_gate.shape[-1]
    n_blocks = n_slots // MOE_BLOCK

    def rows(i, be, nxt, tot):
        return (jnp.maximum(jnp.minimum(i, tot[0] - 1), 0), 0)

    hbm = pl.BlockSpec(memory_space=pl.ANY)
    grid_spec = pltpu.PrefetchScalarGridSpec(
        num_scalar_prefetch=3,
        grid=(n_blocks,),
        in_specs=[pl.BlockSpec((MOE_BLOCK, d), rows), hbm, hbm, hbm],
        out_specs=pl.BlockSpec((MOE_BLOCK, d), lambda i, be, nxt, tot: (i, 0)),
        scratch_shapes=[pltpu.VMEM((d, hid), F32), pltpu.VMEM((d, hid), F32), pltpu.VMEM((hid, d), F32),
                        pltpu.VMEM((d, hid), BF16), pltpu.VMEM((d, hid), BF16), pltpu.VMEM((hid, d), BF16),
                        pltpu.SemaphoreType.DMA((3,))],
    )
    return pl.pallas_call(
        functools.partial(_moe_kernel, layer=layer),
        grid_spec=grid_spec,
        out_shape=jax.ShapeDtypeStruct((n_slots, d), F32),
        compiler_params=_cparams(("arbitrary",), 48),
        name="moe_ffn",
    )(block_expert, next_expert, total_blocks, xg, w_gate, w_up, w_down)


def _dispatch(ew, cnt, n_groups, n_experts):
    t = ew.shape[0]
    eid = ew[:, 2:4].astype(jnp.int32).reshape(-1)
    rank = ew[:, 4:6].astype(jnp.int32).reshape(-1)
    counts = cnt[0, n_groups:n_groups + n_experts].astype(jnp.int32)
    n_assign = eid.shape[0]
    tok = jnp.repeat(jnp.arange(t, dtype=jnp.int32), 2)
    padded = (counts + MOE_BLOCK - 1) // MOE_BLOCK * MOE_BLOCK
    pend = jnp.cumsum(padded)
    pstart = pend - padded
    n_blocks = n_assign // MOE_BLOCK + n_experts
    n_slots = n_blocks * MOE_BLOCK
    valid = eid < n_experts
    slot = jnp.where(valid, pstart[jnp.minimum(eid, n_experts - 1)] + rank, n_slots)
    slot_tok = (jnp.arange(n_slots, dtype=jnp.int32) % t).at[slot].set(tok, mode="drop")
    block_expert = jnp.minimum(
        jnp.searchsorted(pend, jnp.arange(n_blocks, dtype=jnp.int32) * MOE_BLOCK, side="right"),
        n_experts - 1).astype(jnp.int32)
    total_blocks = (pend[-1] // MOE_BLOCK).astype(jnp.int32).reshape(1)
    ids = jnp.arange(n_experts, dtype=jnp.int32)
    later = jnp.where(counts > 0, ids, n_experts)
    nxt = lax.cummin(jnp.concatenate([later[1:], jnp.full((1,), n_experts, jnp.int32)]), reverse=True)
    nxt = jnp.where(nxt >= n_experts, -1, nxt)
    next_expert = nxt[block_expert]
    slot_c = jnp.where(valid, slot, 0).reshape(t, 2)
    return slot_tok, block_expert, next_expert, total_blocks, slot_c


def kernel(x, meta_tokens, ln_emb_g, ln_emb_b, w_in, ssd_conv_w, ssd_conv_b, ssd_dt_bias, ssd_a_log, ssd_d,
           ssd_norm_g, ssd_proj, pool_w, pool_scale, pool_proj, conf_dw_w, conf_dw_b, conf_ln_g, conf_ln_b,
           conf_proj, gate_w, gate_b, w_out, ln1_g, ln1_b, router_group_w, router_group_b, router_expert_w,
           router_expert_b, exp_w_gate, exp_w_up, exp_w_down, ln2_g, ln2_b):
    nb, seq, d = x.shape
    n_meta = meta_tokens.shape[0]
    depth = w_in.shape[0]
    lp = PAD_ROWS + n_meta + seq
    t = nb * lp
    alpha = (2.0 * depth) ** 0.25
    d_inner = ssd_norm_g.shape[-1]
    xbc_w = ssd_conv_w.shape[-1]
    n_heads = ssd_a_log.shape[-1]
    pool_width = pool_scale.shape[-1]
    conf_width = conf_dw_b.shape[-1]
    n_groups = router_group_w.shape[-1]
    n_experts = router_expert_w.shape[-1]
    per_group = n_experts // n_groups
    assert (PAD_ROWS + n_meta) % SSD_CHUNK == 0 and lp % SSD_CHUNK == 0
    assert n_groups + n_experts <= LANES and n_heads <= LANES

    meta = jnp.broadcast_to(meta_tokens[None].astype(x.dtype), (nb, n_meta, d))
    xcat = jnp.concatenate([jnp.zeros((nb, PAD_ROWS, d), x.dtype), meta, x], axis=1).reshape(t, d)
    h, hb = _embed_ln(xcat, ln_emb_g, ln_emb_b, lp)

    c_dt = d_inner + xbc_w
    c_pc = c_dt + n_heads
    w_dt = jnp.pad(w_in[:, :, c_dt:c_pc], ((0, 0), (0, 0), (0, LANES - n_heads)))
    w_pc = w_in[:, :, c_pc:]
    rw = jnp.concatenate([router_group_w, router_expert_w], axis=-1)
    rw = jnp.pad(rw, ((0, 0), (0, 0), (0, LANES - rw.shape[-1])))
    rb = jnp.concatenate([router_group_b, router_expert_b], axis=-1)
    rb = jnp.pad(rb, ((0, 0), (0, LANES - rb.shape[-1]))).reshape(depth, 1, LANES)

    for i in range(depth):
        z = _mm(hb, w_in, i, 0, d_inner)
        xbc = _mm(hb, w_in, i, d_inner, xbc_w)
        dtr = _mm(hb, w_dt, i, 0, LANES)
        pc = _mm(hb, w_pc, i, 0, pool_width + 2 * conf_width)
        f_ssd = _ssd(z, xbc, dtr, ssd_conv_w[i], ssd_conv_b[i], ssd_dt_bias[i], ssd_a_log[i], ssd_d[i],
                     ssd_norm_g[i], nb, lp)
        f_pool = _pool(pc, pool_w[i], pool_scale[i], nb, lp)
        assert pool_width % conf_width == 0
        f_conf = _conf(pc, pool_width // conf_width, conf_dw_w[i], conf_dw_b[i], conf_ln_g[i], conf_ln_b[i], nb, lp)
        merged = _merge(hb, f_ssd, f_pool, f_conf, gate_w, gate_b, ssd_proj, pool_proj, conf_proj, i)
        h = _mm_ln(merged, w_out, i, h, ln1_g[i], ln1_b[i], lp, alpha)

        ew, cnt = _router(h, rw[i], rb[i], lp, n_groups, per_group)
        slot_tok, block_expert, next_expert, total_blocks, slot_c = _dispatch(ew, cnt, n_groups, n_experts)
        xg = h.at[slot_tok].get(mode="promise_in_bounds")
        out = _moe_ffn(xg, block_expert, next_expert, total_blocks, exp_w_gate, exp_w_up, exp_w_down, i)
        g0 = out.at[slot_c[:, 0]].get(mode="promise_in_bounds")
        g1 = out.at[slot_c[:, 1]].get(mode="promise_in_bounds")
        h, hb = _combine_ln(h, g0, g1, ew, ln2_g[i], ln2_b[i], lp, alpha)

    return h.reshape(nb, lp, d)[:, PAD_ROWS + n_meta:]
```

```python
import functools

import jax
import jax.numpy as jnp
from jax import lax
from jax.experimental import pallas as pl
from jax.experimental.pallas import tpu as pltpu

F32 = jnp.float32
BF16 = jnp.bfloat16

PAD_ROWS = 112
SSD_CHUNK = 128
SSD_GROUPS = 4
SSD_HEAD_DIM = 64
POOL_WINDOWS = (2, 4, 8, 16)
MOE_BLOCK = 256
LN_EPS = 1e-5
LANES = 128
NEG_BIG = -1e30
LOG2E = 1.4426950408889634


def _cparams(sem, vmem_mb):
    return pltpu.CompilerParams(dimension_semantics=sem, vmem_limit_bytes=vmem_mb << 20)


def _sigmoid(x):
    return 1.0 / (1.0 + jnp.exp(-x))


def _softplus(x):
    return jnp.maximum(x, 0.0) + jnp.log1p(jnp.exp(-jnp.abs(x)))


def _ln(x, g, b):
    mu = jnp.mean(x, -1, keepdims=True)
    xc = x - mu
    var = jnp.mean(xc * xc, -1, keepdims=True)
    return xc * lax.rsqrt(var + LN_EPS) * g + b


def _rows_in_batch(blk, bm, lp):
    base = lax.rem(blk, lp // bm) * bm
    return base + lax.broadcasted_iota(jnp.int32, (bm, 1), 0)


def _pick(n, cands):
    for c in cands:
        if n % c == 0:
            return c
    raise ValueError(f"no block size for {n} among {cands}")


def _embed_kernel(x_ref, g_ref, b_ref, h_ref, hb_ref, *, bm, lp):
    y = _ln(x_ref[...], g_ref[...], b_ref[...])
    y = jnp.where(_rows_in_batch(pl.program_id(0), bm, lp) >= PAD_ROWS, y, 0.0)
    h_ref[...] = y
    hb_ref[...] = y.astype(BF16)


def _embed_ln(xcat, g, b, lp):
    t, d = xcat.shape
    bm = _pick(lp, (528, 384, 192, 128, 64, 8))
    row = pl.BlockSpec((bm, d), lambda i: (i, 0))
    vec = pl.BlockSpec((1, d), lambda i: (0, 0))
    return pl.pallas_call(
        functools.partial(_embed_kernel, bm=bm, lp=lp),
        grid=(t // bm,),
        in_specs=[row, vec, vec],
        out_specs=[row, row],
        out_shape=[jax.ShapeDtypeStruct((t, d), F32), jax.ShapeDtypeStruct((t, d), BF16)],
        compiler_params=_cparams(("parallel",), 40),
        name="embed_ln",
    )(xcat, g.reshape(1, d), b.reshape(1, d))


def _combine_ln_kernel(h_ref, g0_ref, g1_ref, ew_ref, g_ref, b_ref, o_ref, ob_ref, *, bm, lp, alpha):
    y = ew_ref[:, 0:1] * g0_ref[...] + ew_ref[:, 1:2] * g1_ref[...]
    y = _ln(alpha * h_ref[...] + y, g_ref[...], b_ref[...])
    y = jnp.where(_rows_in_batch(pl.program_id(0), bm, lp) >= PAD_ROWS, y, 0.0)
    o_ref[...] = y
    ob_ref[...] = y.astype(BF16)


def _combine_ln(h, g0, g1, ew, g, b, lp, alpha):
    t, d = h.shape
    bm = _pick(lp, (528, 384, 192, 128, 64, 8))
    row = pl.BlockSpec((bm, d), lambda i: (i, 0))
    vec = pl.BlockSpec((1, d), lambda i: (0, 0))
    return pl.pallas_call(
        functools.partial(_combine_ln_kernel, bm=bm, lp=lp, alpha=alpha),
        grid=(t // bm,),
        in_specs=[row, row, row, pl.BlockSpec((bm, LANES), lambda i: (i, 0)), vec, vec],
        out_specs=[row, row],
        out_shape=[jax.ShapeDtypeStruct((t, d), F32), jax.ShapeDtypeStruct((t, d), BF16)],
        compiler_params=_cparams(("parallel",), 56),
        name="combine_ln",
    )(h, g0, g1, ew, g.reshape(1, d), b.reshape(1, d))


def _final_ln_kernel(h_ref, g0_ref, g1_ref, ew_ref, g_ref, b_ref, o_ref, *, alpha):
    y = ew_ref[:, 0:1] * g0_ref[...] + ew_ref[:, 1:2] * g1_ref[...]
    o_ref[...] = _ln(alpha * h_ref[...] + y, g_ref[...], b_ref[...])


def _final_ln(h, g0, g1, ew, g, b, nb, lp, seq, alpha):
    d = h.shape[1]
    head = lp - seq
    bm = _pick(head, (128, 64, 8))
    assert seq % bm == 0
    per_seq, per_lp, skip = seq // bm, lp // bm, head // bm
    src = lambda i: ((i // per_seq) * per_lp + skip + i % per_seq, 0)
    row = pl.BlockSpec((bm, d), src)
    vec = pl.BlockSpec((1, d), lambda i: (0, 0))
    return pl.pallas_call(
        functools.partial(_final_ln_kernel, alpha=alpha),
        grid=(nb * per_seq,),
        in_specs=[row, row, row, pl.BlockSpec((bm, LANES), src), vec, vec],
        out_specs=pl.BlockSpec((bm, d), lambda i: (i, 0)),
        out_shape=jax.ShapeDtypeStruct((nb * seq, d), F32),
        compiler_params=_cparams(("parallel",), 32),
        name="final_ln",
    )(h, g0, g1, ew, g.reshape(1, d), b.reshape(1, d))


def _mm_kernel(x_ref, w_ref, *rest, shift):
    if shift:
        wnext_ref, o_ref, wbf_ref = rest
    else:
        o_ref, wbf_ref = rest

    @pl.when(pl.program_id(1) == 0)
    def _():
        if shift:
            wbf_ref[...] = jnp.concatenate([w_ref[:, shift:], wnext_ref[:, :shift]], axis=1).astype(BF16)
        else:
            wbf_ref[...] = w_ref[...].astype(BF16)

    o_ref[...] = jnp.dot(x_ref[...], wbf_ref[...], preferred_element_type=F32).astype(o_ref.dtype)


def _mm(x, w, layer, col0, n_cols, out_dtype=F32):
    t, k = x.shape
    bn = _pick(n_cols, (1024, 512, 256, 128))
    bm = _pick(t, (1056, 528, 384, 192, 128, 64, 8))
    shift = col0 % LANES
    base = col0 - shift
    assert base % bn == 0
    cb0 = base // bn
    in_specs = [pl.BlockSpec((bm, k), lambda n, m: (m, 0)),
                pl.BlockSpec((None, k, bn), lambda n, m: (layer, 0, cb0 + n))]
    args = [x, w]
    if shift:
        tiles = bn // LANES
        in_specs.append(pl.BlockSpec((None, k, LANES), lambda n, m: (layer, 0, (cb0 + n + 1) * tiles)))
        args.append(w)
    return pl.pallas_call(
        functools.partial(_mm_kernel, shift=shift),
        grid=(n_cols // bn, t // bm),
        in_specs=in_specs,
        out_specs=pl.BlockSpec((bm, bn), lambda n, m: (m, n)),
        out_shape=jax.ShapeDtypeStruct((t, n_cols), out_dtype),
        scratch_shapes=[pltpu.VMEM((k, bn), BF16)],
        compiler_params=_cparams(("arbitrary", "arbitrary"), 48),
        name="mm",
    )(*args)


def _mm_ln_kernel(x_ref, w_ref, h_ref, g_ref, b_ref, o_ref, wbf_ref, *, bm, lp, alpha):
    @pl.when(pl.program_id(0) == 0)
    def _():
        wbf_ref[...] = w_ref[...].astype(BF16)

    y = jnp.dot(x_ref[...], wbf_ref[...], preferred_element_type=F32)
    y = _ln(alpha * h_ref[...] + y, g_ref[...], b_ref[...])
    o_ref[...] = jnp.where(_rows_in_batch(pl.program_id(0), bm, lp) >= PAD_ROWS, y, 0.0)


def _mm_ln(x, w, layer, h, g, b, lp, alpha):
    t, k = x.shape
    d = w.shape[-1]
    bm = _pick(lp, (528, 384, 192, 128, 64, 8))
    row = lambda width: pl.BlockSpec((bm, width), lambda i: (i, 0))
    vec = pl.BlockSpec((1, d), lambda i: (0, 0))
    return pl.pallas_call(
        functools.partial(_mm_ln_kernel, bm=bm, lp=lp, alpha=alpha),
        grid=(t // bm,),
        in_specs=[row(k), pl.BlockSpec((None, k, d), lambda i: (layer, 0, 0), pipeline_mode=pl.Buffered(1)),
                  row(d), vec, vec],
        out_specs=row(d),
        out_shape=jax.ShapeDtypeStruct((t, d), F32),
        scratch_shapes=[pltpu.VMEM((k, d), BF16)],
        compiler_params=_cparams(("arbitrary",), 56),
        name="mm_ln",
    )(x, w, h, g.reshape(1, d), b.reshape(1, d))


def _ssd_kernel(z_ref, xbc_ref, dtr_ref, cw_ref, cb_ref, dtb_ref, alog_ref, dexp_ref, ng_ref, o_ref,
                cbuf, xc, ybuf, st, *, q, d_inner, n_state):
    c = pl.program_id(1)
    xbc_w = cbuf.shape[1]
    kc = cw_ref.shape[0]
    hist = 8

    @pl.when(c == 0)
    def _():
        cbuf[0:hist, :] = jnp.zeros((hist, xbc_w), F32)
        st[...] = jnp.zeros(st.shape, F32)

    cbuf[hist:hist + q, :] = xbc_ref[...]
    cw = 128
    for j in range(xbc_w // cw):
        sl = slice(j * cw, (j + 1) * cw)
        conv = cb_ref[:, sl] + cw_ref[kc - 1:kc, sl] * cbuf[hist:hist + q, sl]
        for k in range(kc - 1):
            off = hist - (kc - 1) + k
            conv = conv + cw_ref[k:k + 1, sl] * cbuf[off:off + q, sl]
        xc[:, sl] = conv * _sigmoid(conv)
    cbuf[0:hist, :] = cbuf[q:q + hist, :]

    row = c * q + lax.broadcasted_iota(jnp.int32, (q, 1), 0)
    dt = _softplus(dtr_ref[...] + dtb_ref[...])
    dt = jnp.where(row >= PAD_ROWS, dt, 0.0)
    adt = dt * (-jnp.exp(alog_ref[...]))
    li = lax.broadcasted_iota(jnp.int32, (q, q), 0)
    si = lax.broadcasted_iota(jnp.int32, (q, q), 1)
    tri = li >= si
    acs = jnp.dot(tri.astype(F32), adt, preferred_element_type=F32,
                  precision=lax.Precision.HIGHEST) * LOG2E
    acs_t = acs.T
    dt_t = dt.T
    e_acs = jnp.exp2(acs)
    last = acs_t[:, q - 1:q]
    w_t = jnp.exp2(last - acs_t) * dt_t
    src_t = acs_t - jnp.log2(dt_t)
    dlast = jnp.exp2(last)
    lane = lax.broadcasted_iota(jnp.int32, (1, LANES), 1)
    lo_half = lane < SSD_HEAD_DIM

    heads_per_group = d_inner // SSD_HEAD_DIM // SSD_GROUPS
    pairs_per_group = heads_per_group // 2
    for g in range(SSD_GROUPS):
        b0 = d_inner + g * n_state
        c0 = d_inner + SSD_GROUPS * n_state + g * n_state
        bg = xc[:, b0:b0 + n_state]
        cg = xc[:, c0:c0 + n_state]
        cbm = lax.dot_general(cg.astype(BF16), bg.astype(BF16), (((1,), (1,)), ((), ())),
                              preferred_element_type=F32)
        bg_t = bg.T
        for j in range(pairs_per_group):
            pair = g * pairs_per_group + j
            h0 = 2 * pair
            xs_pair = xc[:, h0 * SSD_HEAD_DIM:h0 * SSD_HEAD_DIM + LANES]
            st_pair = st[pair]
            lhs_y, lhs_s, rhs_x, rhs_st = [], [], [], []
            for u in range(2):
                h = h0 + u
                col = acs[:, h:h + 1]
                rw = src_t[h:h + 1, :]
                decay = jnp.where(tri, jnp.exp2(col - rw), 0.0)
                lhs_y.append((cbm * decay).astype(BF16))
                lhs_s.append((bg_t * w_t[h:h + 1, :]).astype(BF16))
                keep = lo_half if u == 0 else jnp.logical_not(lo_half)
                rhs_x.append(jnp.where(keep, xs_pair, 0.0).astype(BF16))
                rhs_st.append(jnp.where(keep, st_pair, 0.0).astype(BF16))
            for u in range(2):
                lhs_y.append((e_acs[:, h0 + u:h0 + u + 1] * cg).astype(BF16))
            y_pair = jnp.dot(jnp.concatenate(lhs_y, axis=1), jnp.concatenate(rhs_x + rhs_st, axis=0),
                             preferred_element_type=F32)
            s_new = jnp.dot(jnp.concatenate(lhs_s, axis=1), jnp.concatenate(rhs_x, axis=0),
                            preferred_element_type=F32)
            dpair = jnp.where(lo_half, dlast[h0:h0 + 1, :], dlast[h0 + 1:h0 + 2, :])
            st[pair] = st_pair * dpair + s_new
            ybuf[:, h0 * SSD_HEAD_DIM:h0 * SSD_HEAD_DIM + LANES] = y_pair

    gw = d_inner // SSD_GROUPS
    for g in range(SSD_GROUPS):
        sl = slice(g * gw, (g + 1) * gw)
        y = ybuf[:, sl] + xc[:, sl] * dexp_ref[:, sl]
        zz = z_ref[:, sl]
        y = y * (zz * _sigmoid(zz))
        y = y * lax.rsqrt(jnp.mean(y * y, -1, keepdims=True) + LN_EPS)
        o_ref[:, sl] = (y * ng_ref[:, sl]).astype(BF16)


def _ssd(z, xbc, dtr, conv_w, conv_b, dt_bias, a_log, d_skip, norm_g, nb, lp):
    t, d_inner = z.shape
    xbc_w = xbc.shape[1]
    q = SSD_CHUNK
    n_heads = d_inner // SSD_HEAD_DIM
    n_state = (xbc_w - d_inner) // (2 * SSD_GROUPS)
    assert n_state == LANES and n_heads % (2 * SSD_GROUPS) == 0 and lp % q == 0
    assert xbc_w % 512 == 0 and PAD_ROWS % 8 == 0
    nc = lp // q
    padh = LANES - n_heads
    row = lambda w: pl.BlockSpec((q, w), lambda b, c: (b * nc + c, 0))
    full = lambda a: pl.BlockSpec(a.shape, lambda b, c: (0,) * a.ndim)
    dtb = jnp.pad(dt_bias, (0, padh)).reshape(1, LANES)
    alog = jnp.pad(a_log, (0, padh)).reshape(1, LANES)
    dexp = jnp.repeat(d_skip, SSD_HEAD_DIM).reshape(1, d_inner)
    cb = conv_b.reshape(1, xbc_w)
    ng = norm_g.reshape(1, d_inner)
    return pl.pallas_call(
        functools.partial(_ssd_kernel, q=q, d_inner=d_inner, n_state=n_state),
        grid=(nb, nc),
        in_specs=[row(d_inner), row(xbc_w), row(LANES), full(conv_w), full(cb), full(dtb), full(alog),
                  full(dexp), full(ng)],
        out_specs=row(d_inner),
        out_shape=jax.ShapeDtypeStruct((t, d_inner), BF16),
        scratch_shapes=[pltpu.VMEM((8 + q, xbc_w), F32), pltpu.VMEM((q, xbc_w), F32),
                        pltpu.VMEM((q, d_inner), F32), pltpu.VMEM((n_heads // 2, n_state, LANES), F32)],
        compiler_params=_cparams(("arbitrary", "arbitrary"), 32),
        name="ssd",
    )(z, xbc, dtr, conv_w, cb, dtb, alog, dexp, ng)


def _pool_kernel(u_ref, w_ref, sc_ref, o_ref, buf, *, r, gd):
    c = pl.program_id(1)
    hist = 16
    width = buf.shape[1]

    @pl.when(c == 0)
    def _():
        buf[0:hist, :] = jnp.zeros((hist, width), F32)

    buf[hist:hist + r, :] = u_ref[...]
    pos = c * r + lax.broadcasted_iota(jnp.int32, (r, 1), 0) - PAD_ROWS
    for gi, win in enumerate(POOL_WINDOWS):
        sl = slice(gi * gd, (gi + 1) * gd)
        u = buf[hist:hist + r, sl]
        s = u
        for j in range(1, win):
            s = s + buf[hist - j:hist - j + r, sl]
        cnt = jnp.clip(pos + 1, 1, win).astype(F32)
        mixed = s / cnt - u
        y = jnp.dot(mixed.astype(BF16), w_ref[gi].astype(BF16), preferred_element_type=F32)
        o_ref[:, sl] = (y * sc_ref[:, sl]).astype(BF16)
    buf[0:hist, :] = buf[r:r + hist, :]


def _pool(pc, pool_w, scale, nb, lp):
    t = pc.shape[0]
    ng, gd, _ = pool_w.shape
    width = ng * gd
    assert ng == len(POOL_WINDOWS) and gd % LANES == 0
    r = _pick(lp, (384, 192, 128, 64))
    nc = lp // r
    return pl.pallas_call(
        functools.partial(_pool_kernel, r=r, gd=gd),
        grid=(nb, nc),
        in_specs=[pl.BlockSpec((r, width), lambda b, c: (b * nc + c, 0)),
                  pl.BlockSpec(pool_w.shape, lambda b, c: (0, 0, 0)),
                  pl.BlockSpec((1, width), lambda b, c: (0, 0))],
        out_specs=pl.BlockSpec((r, width), lambda b, c: (b * nc + c, 0)),
        out_shape=jax.ShapeDtypeStruct((t, width), BF16),
        scratch_shapes=[pltpu.VMEM((16 + r, width), F32)],
        compiler_params=_cparams(("arbitrary", "arbitrary"), 32),
        name="pool",
    )(pc, pool_w, scale.reshape(1, width))


def _conf_kernel(a_ref, g_ref, w_ref, b_ref, lg_ref, lb_ref, o_ref, vbuf, cv, ubuf, *, r):
    c = pl.program_id(1)
    hist = 32
    sub = 8
    width = vbuf.shape[1]
    kc = w_ref.shape[0]

    @pl.when(c == 0)
    def _():
        vbuf[0:hist, :] = jnp.zeros((hist, width), F32)

    vbuf[hist:hist + r, :] = a_ref[...] * _sigmoid(g_ref[...])
    ext = r + sub
    cw = ubuf.shape[2]
    for jc in range(width // cw):
        sl = slice(jc * cw, (jc + 1) * cw)
        acc = None
        for rr in range(sub):
            u = None
            for j in range((kc - 1 - rr) // sub + 1):
                k = kc - 1 - (sub * j + rr)
                lo = hist - sub - sub * j
                term = w_ref[k:k + 1, sl] * vbuf[lo:lo + ext, sl]
                u = term if u is None else u + term
            if rr == 0:
                acc = b_ref[:, sl] + u[sub:sub + r]
            else:
                ubuf[rr - 1] = u
                acc = acc + ubuf[rr - 1, sub - rr:sub - rr + r, :]
        cv[:, sl] = acc
    vbuf[0:hist, :] = vbuf[r:r + hist, :]
    v = _ln(cv[...], lg_ref[...], lb_ref[...])
    o_ref[...] = (v * _sigmoid(v)).astype(BF16)


def _conf(pc, col_blk, dw_w, dw_b, ln_g, ln_b, nb, lp):
    t = pc.shape[0]
    kc, width = dw_w.shape
    assert kc <= 33 and width % 256 == 0
    r = _pick(lp, (192, 128, 64))
    nc = lp // r
    vec = lambda: pl.BlockSpec((1, width), lambda b, c: (0, 0))
    return pl.pallas_call(
        functools.partial(_conf_kernel, r=r),
        grid=(nb, nc),
        in_specs=[pl.BlockSpec((r, width), lambda b, c: (b * nc + c, col_blk)),
                  pl.BlockSpec((r, width), lambda b, c: (b * nc + c, col_blk + 1)),
                  pl.BlockSpec((kc, width), lambda b, c: (0, 0)), vec(), vec(), vec()],
        out_specs=pl.BlockSpec((r, width), lambda b, c: (b * nc + c, 0)),
        out_shape=jax.ShapeDtypeStruct((t, width), BF16),
        scratch_shapes=[pltpu.VMEM((32 + r, width), F32), pltpu.VMEM((r, width), F32),
                        pltpu.VMEM((7, r + 8, LANES), F32)],
        compiler_params=_cparams(("arbitrary", "arbitrary"), 32),
        name="conf",
    )(pc, pc, dw_w, dw_b.reshape(1, width), ln_g.reshape(1, width), ln_b.reshape(1, width))


def _merge_kernel(hb_ref, f0_ref, f1_ref, f2_ref, g0_ref, g1_ref, g2_ref, gb_ref, p0_ref, p1_ref, p2_ref,
                  o_ref, gs, s0, s1, s2):
    @pl.when(pl.program_id(1) == 0)
    def _():
        gs[0] = g0_ref[...].astype(BF16)
        gs[1] = g1_ref[...].astype(BF16)
        gs[2] = g2_ref[...].astype(BF16)
        s0[...] = p0_ref[...].astype(BF16)
        s1[...] = p1_ref[...].astype(BF16)
        s2[...] = p2_ref[...].astype(BF16)

    hb = hb_ref[...]
    acc = None
    for j, (f_ref, s_ref) in enumerate(((f0_ref, s0), (f1_ref, s1), (f2_ref, s2))):
        gate = _sigmoid(jnp.dot(hb, gs[j], preferred_element_type=F32) + gb_ref[j:j + 1, :])
        term = gate * jnp.dot(f_ref[...], s_ref[...], preferred_element_type=F32)
        acc = term if acc is None else acc + term
    o_ref[...] = acc.astype(BF16)


def _merge(hb, f_ssd, f_pool, f_conf, gate_w, gate_b, ssd_proj, pool_proj, conf_proj, layer):
    t, d = hb.shape
    bn = _pick(d, (512, 256, 128))
    bm = _pick(t, (528, 384, 192, 128, 64, 8))
    nbn = d // bn
    once = pl.Buffered(1)
    rowspec = lambda a: pl.BlockSpec((bm, a.shape[1]), lambda n, m: (m, 0))
    gspec = lambda j: pl.BlockSpec((None, d, bn), lambda n, m: (layer, 0, j * nbn + n), pipeline_mode=once)
    pspec = lambda w: pl.BlockSpec((None, w.shape[1], bn), lambda n, m: (layer, 0, n), pipeline_mode=once)
    gb = gate_b[layer].reshape(3, d)
    return pl.pallas_call(
        _merge_kernel,
        grid=(nbn, t // bm),
        in_specs=[rowspec(hb), rowspec(f_ssd), rowspec(f_pool), rowspec(f_conf),
                  gspec(0), gspec(1), gspec(2), pl.BlockSpec((3, bn), lambda n, m: (0, n)),
                  pspec(ssd_proj), pspec(pool_proj), pspec(conf_proj)],
        out_specs=pl.BlockSpec((bm, bn), lambda n, m: (m, n)),
        out_shape=jax.ShapeDtypeStruct((t, d), BF16),
        scratch_shapes=[pltpu.VMEM((3, d, bn), BF16), pltpu.VMEM((ssd_proj.shape[1], bn), BF16),
                        pltpu.VMEM((pool_proj.shape[1], bn), BF16), pltpu.VMEM((conf_proj.shape[1], bn), BF16)],
        compiler_params=_cparams(("arbitrary", "arbitrary"), 52),
        name="merge",
    )(hb, f_ssd, f_pool, f_conf, gate_w, gate_w, gate_w, gb, ssd_proj, pool_proj, conf_proj)


def _router_kernel(h_ref, w_ref, b_ref, o_ref, cnt_ref, run, *, bm, lp, n_groups, per_group):
    @pl.when(pl.program_id(0) == 0)
    def _():
        run[...] = jnp.zeros(run.shape, F32)

    h = h_ref[...]
    w = w_ref[...]
    h_hi = h.astype(BF16)
    h_lo = (h - h_hi.astype(F32)).astype(BF16)
    w_hi = w.astype(BF16)
    w_lo = (w - w_hi.astype(F32)).astype(BF16)
    logits = (jnp.dot(h_hi, w_hi, preferred_element_type=F32)
              + (jnp.dot(h_hi, w_lo, preferred_element_type=F32)
                 + jnp.dot(h_lo, w_hi, preferred_element_type=F32))) + b_ref[...]
    lane = lax.broadcasted_iota(jnp.int32, logits.shape, 1).astype(F32)

    def first_max(vals):
        m = jnp.max(vals, axis=1, keepdims=True)
        idx = jnp.min(jnp.where(vals == m, lane, float(LANES)), axis=1, keepdims=True)
        return m, idx

    gmask = lane < n_groups
    gmax, gsel = first_max(jnp.where(gmask, logits, NEG_BIG))
    gsum = jnp.sum(jnp.where(gmask, jnp.exp(logits - gmax), 0.0), axis=1, keepdims=True)
    p_group = 1.0 / gsum
    lo = n_groups + per_group * gsel
    el = jnp.where(jnp.logical_and(lane >= lo, lane < lo + per_group), logits, NEG_BIG)
    m1, i1 = first_max(el)
    m2, i2 = first_max(jnp.where(lane == i1, NEG_BIG, el))
    ratio = jnp.exp(m2 - m1)
    w1 = p_group / (1.0 + ratio)
    w2 = w1 * ratio
    real = _rows_in_batch(pl.program_id(0), bm, lp) >= PAD_ROWS
    sentinel = float(n_groups * per_group)
    e1 = jnp.where(real, i1 - n_groups, sentinel)
    e2 = jnp.where(real, i2 - n_groups, sentinel)
    w1 = jnp.where(real, w1, 0.0)
    w2 = jnp.where(real, w2, 0.0)
    hit1 = jnp.logical_and(lane == i1, real)
    hit2 = jnp.logical_and(lane == i2, real)
    onehot = jnp.where(jnp.logical_or(hit1, hit2), 1.0, 0.0)
    li = lax.broadcasted_iota(jnp.int32, (bm, bm), 0)
    si = lax.broadcasted_iota(jnp.int32, (bm, bm), 1)
    before = jnp.where(li > si, 1.0, 0.0).astype(BF16)
    seen = jnp.dot(before, onehot.astype(BF16), preferred_element_type=F32) + run[...]
    r1 = jnp.sum(jnp.where(hit1, seen, 0.0), axis=1, keepdims=True)
    r2 = jnp.sum(jnp.where(hit2, seen, 0.0), axis=1, keepdims=True)
    run[...] = run[...] + jnp.sum(onehot, axis=0, keepdims=True)
    cnt_ref[...] = run[...]
    out = jnp.zeros(logits.shape, F32)
    for k, v in enumerate((w1, w2, e1, e2, r1, r2)):
        out = jnp.where(lane == k, v, out)
    o_ref[...] = out


def _router(h, rw, rb, lp, n_groups, per_group):
    t, d = h.shape
    bm = _pick(lp, (528, 384, 192, 128, 64, 8))
    return pl.pallas_call(
        functools.partial(_router_kernel, bm=bm, lp=lp, n_groups=n_groups, per_group=per_group),
        grid=(t // bm,),
        in_specs=[pl.BlockSpec((bm, d), lambda i: (i, 0)), pl.BlockSpec((d, LANES), lambda i: (0, 0)),
                  pl.BlockSpec((1, LANES), lambda i: (0, 0))],
        out_specs=[pl.BlockSpec((bm, LANES), lambda i: (i, 0)), pl.BlockSpec((1, LANES), lambda i: (0, 0))],
        out_shape=[jax.ShapeDtypeStruct((t, LANES), F32), jax.ShapeDtypeStruct((1, LANES), F32)],
        scratch_shapes=[pltpu.VMEM((1, LANES), F32)],
        compiler_params=_cparams(("arbitrary",), 40),
        name="router",
    )(h, rw, rb)


def _moe_kernel(be_ref, nxt_ref, tot_ref, x_ref, wg_hbm, wu_hbm, wd_hbm, o_ref,
                wg_f, wu_f, wd_f, wg_s, wu_s, wd_s, sems, *, layer):
    i = pl.program_id(0)
    total = tot_ref[0]
    ic = jnp.maximum(jnp.minimum(i, total - 1), 0)
    e = be_ref[ic]
    e_prev = be_ref[jnp.maximum(ic - 1, 0)]
    active = i < total
    first_of_run = jnp.logical_and(active, jnp.logical_or(i == 0, e != e_prev))

    def weight_copies(expert):
        return (pltpu.make_async_copy(wg_hbm.at[layer, expert], wg_f, sems.at[0]),
                pltpu.make_async_copy(wu_hbm.at[layer, expert], wu_f, sems.at[1]),
                pltpu.make_async_copy(wd_hbm.at[layer, expert], wd_f, sems.at[2]))

    @pl.when(jnp.logical_and(active, i == 0))
    def _():
        for cp in weight_copies(e):
            cp.start()

    @pl.when(first_of_run)
    def _():
        for cp, src, dst in zip(weight_copies(e), (wg_f, wu_f, wd_f), (wg_s, wu_s, wd_s)):
            cp.wait()
            dst[...] = src[...].astype(BF16)
        nxt = nxt_ref[ic]

        @pl.when(nxt >= 0)
        def _():
            for cp in weight_copies(nxt):
                cp.start()

    @pl.when(active)
    def _():
        x = x_ref[...].astype(BF16)
        gte = jnp.dot(x, wg_s[...], preferred_element_type=F32)
        up = jnp.dot(x, wu_s[...], preferred_element_type=F32)
        hid = (gte * _sigmoid(gte)) * up
        o_ref[...] = jnp.dot(hid.astype(BF16), wd_s[...], preferred_element_type=F32)

    @pl.when(jnp.logical_not(active))
    def _():
        o_ref[...] = jnp.zeros(o_ref.shape, F32)


def _moe_ffn(xg, block_expert, next_expert, total_blocks, w_gate, w_up, w_down, layer):
    n_slots, d = xg.shape
    hid = w_gate.shape[-1]
    n_blocks = n_slots // MOE_BLOCK

    def rows(i, be, nxt, tot):
        return (jnp.maximum(jnp.minimum(i, tot[0] - 1), 0), 0)

    hbm = pl.BlockSpec(memory_space=pl.ANY)
    grid_spec = pltpu.PrefetchScalarGridSpec(
        num_scalar_prefetch=3,
        grid=(n_blocks,),
        in_specs=[pl.BlockSpec((MOE_BLOCK, d), rows), hbm, hbm, hbm],
        out_specs=pl.BlockSpec((MOE_BLOCK, d), lambda i, be, nxt, tot: (i, 0)),
        scratch_shapes=[pltpu.VMEM((d, hid), F32), pltpu.VMEM((d, hid), F32), pltpu.VMEM((hid, d), F32),
                        pltpu.VMEM((d, hid), BF16), pltpu.VMEM((d, hid), BF16), pltpu.VMEM((hid, d), BF16),
                        pltpu.SemaphoreType.DMA((3,))],
    )
    return pl.pallas_call(
        functools.partial(_moe_kernel, layer=layer),
        grid_spec=grid_spec,
        out_shape=jax.ShapeDtypeStruct((n_slots, d), F32),
        compiler_params=_cparams(("arbitrary",), 48),
        name="moe_ffn",
    )(block_expert, next_expert, total_blocks, xg, w_gate, w_up, w_down)


def _dispatch(ew, cnt, n_groups, n_experts):
    t = ew.shape[0]
    eid = ew[:, 2:4].astype(jnp.int32).reshape(-1)
    rank = ew[:, 4:6].astype(jnp.int32).reshape(-1)
    counts = cnt[0, n_groups:n_groups + n_experts].astype(jnp.int32)
    n_assign = eid.shape[0]
    tok = jnp.repeat(jnp.arange(t, dtype=jnp.int32), 2)
    padded = (counts + MOE_BLOCK - 1) // MOE_BLOCK * MOE_BLOCK
    pend = jnp.cumsum(padded)
    pstart = pend - padded
    n_blocks = n_assign // MOE_BLOCK + n_experts
    n_slots = n_blocks * MOE_BLOCK
    valid = eid < n_experts
    slot = jnp.where(valid, pstart[jnp.minimum(eid, n_experts - 1)] + rank, n_slots)
    slot_tok = (jnp.arange(n_slots, dtype=jnp.int32) % t).at[slot].set(tok, mode="drop")
    block_expert = jnp.minimum(
        jnp.searchsorted(pend, jnp.arange(n_blocks, dtype=jnp.int32) * MOE_BLOCK, side="right"),
        n_experts - 1).astype(jnp.int32)
    total_blocks = (pend[-1] // MOE_BLOCK).astype(jnp.int32).reshape(1)
    ids = jnp.arange(n_experts, dtype=jnp.int32)
    later = jnp.where(counts > 0, ids, n_experts)
    nxt = lax.cummin(jnp.concatenate([later[1:], jnp.full((1,), n_experts, jnp.int32)]), reverse=True)
    nxt = jnp.where(nxt >= n_experts, -1, nxt)
    next_expert = nxt[block_expert]
    slot_c = jnp.where(valid, slot, 0).reshape(t, 2)
    return slot_tok, block_expert, next_expert, total_blocks, slot_c


def kernel(x, meta_tokens, ln_emb_g, ln_emb_b, w_in, ssd_conv_w, ssd_conv_b, ssd_dt_bias, ssd_a_log, ssd_d,
           ssd_norm_g, ssd_proj, pool_w, pool_scale, pool_proj, conf_dw_w, conf_dw_b, conf_ln_g, conf_ln_b,
           conf_proj, gate_w, gate_b, w_out, ln1_g, ln1_b, router_group_w, router_group_b, router_expert_w,
           router_expert_b, exp_w_gate, exp_w_up, exp_w_down, ln2_g, ln2_b):
    nb, seq, d = x.shape
    n_meta = meta_tokens.shape[0]
    depth = w_in.shape[0]
    lp = PAD_ROWS + n_meta + seq
    t = nb * lp
    alpha = (2.0 * depth) ** 0.25
    d_inner = ssd_norm_g.shape[-1]
    xbc_w = ssd_conv_w.shape[-1]
    n_heads = ssd_a_log.shape[-1]
    pool_width = pool_scale.shape[-1]
    conf_width = conf_dw_b.shape[-1]
    n_groups = router_group_w.shape[-1]
    n_experts = router_expert_w.shape[-1]
    per_group = n_experts // n_groups
    assert (PAD_ROWS + n_meta) % SSD_CHUNK == 0 and lp % SSD_CHUNK == 0
    assert n_groups + n_experts <= LANES and n_heads <= LANES

    meta = jnp.broadcast_to(meta_tokens[None].astype(x.dtype), (nb, n_meta, d))
    xcat = jnp.concatenate([jnp.zeros((nb, PAD_ROWS, d), x.dtype), meta, x], axis=1).reshape(t, d)
    h, hb = _embed_ln(xcat, ln_emb_g, ln_emb_b, lp)

    c_dt = d_inner + xbc_w
    c_pc = c_dt + n_heads
    assert c_dt % LANES == 0 and c_dt + LANES <= w_in.shape[-1]
    rw =jnp.concatenate([router_group_w, router_expert_w], axis=-1)
    rw = jnp.pad(rw, ((0, 0), (0, 0), (0, LANES - rw.shape[-1])))
    rb = jnp.concatenate([router_group_b, router_expert_b], axis=-1)
    rb = jnp.pad(rb, ((0, 0), (0, LANES - rb.shape[-1]))).reshape(depth, 1, LANES)

    for i in range(depth):
        z = _mm(hb, w_in, i, 0, d_inner)
        xbc = _mm(hb, w_in, i, d_inner, xbc_w)
        dtr = _mm(hb, w_in, i, c_dt, LANES)
        pc = _mm(hb, w_in, i, c_pc, pool_width + 2 * conf_width)
        f_ssd = _ssd(z, xbc, dtr, ssd_conv_w[i], ssd_conv_b[i], ssd_dt_bias[i], ssd_a_log[i], ssd_d[i],
                     ssd_norm_g[i], nb, lp)
        f_pool = _pool(pc, pool_w[i], pool_scale[i], nb, lp)
        assert pool_width % conf_width == 0
        f_conf = _conf(pc, pool_width // conf_width, conf_dw_w[i], conf_dw_b[i], conf_ln_g[i], conf_ln_b[i], nb, lp)
        merged = _merge(hb, f_ssd, f_pool, f_conf, gate_w, gate_b, ssd_proj, pool_proj, conf_proj, i)
        h = _mm_ln(merged, w_out, i, h, ln1_g[i], ln1_b[i], lp, alpha)

        ew, cnt = _router(h, rw[i], rb[i], lp, n_groups, per_group)
        slot_tok, block_expert, next_expert, total_blocks, slot_c = _dispatch(ew, cnt, n_groups, n_experts)
        xg = h.at[slot_tok].get(mode="promise_in_bounds")
        out = _moe_ffn(xg, block_expert, next_expert, total_blocks, exp_w_gate, exp_w_up, exp_w_down, i)
        g0 = out.at[slot_c[:, 0]].get(mode="promise_in_bounds")
        g1 = out.at[slot_c[:, 1]].get(mode="promise_in_bounds")
        if i + 1 < depth:
            h, hb = _combine_ln(h, g0, g1, ew, ln2_g[i], ln2_b[i], lp, alpha)
        else:
            h = _final_ln(h, g0, g1, ew, ln2_g[i], ln2_b[i], nb, lp, seq, alpha)

    return h.reshape(nb, seq, d)
```

```python
import functools

import jax
import jax.numpy as jnp
from jax import lax
from jax.experimental import pallas as pl
from jax.experimental.pallas import tpu as pltpu

F32 = jnp.float32
BF16 = jnp.bfloat16

PAD_ROWS = 112
SSD_CHUNK = 128
SSD_GROUPS = 4
SSD_HEAD_DIM = 64
POOL_WINDOWS = (2, 4, 8, 16)
MOE_BLOCK = 256
LN_EPS = 1e-5
LANES = 128
NEG_BIG = -1e30
LOG2E = 1.4426950408889634


def _cparams(sem, vmem_mb):
    return pltpu.CompilerParams(dimension_semantics=sem, vmem_limit_bytes=vmem_mb << 20)


def _sigmoid(x):
    return 1.0 / (1.0 + jnp.exp(-x))


def _softplus(x):
    return jnp.maximum(x, 0.0) + jnp.log1p(jnp.exp(-jnp.abs(x)))


def _ln(x, g, b):
    mu = jnp.mean(x, -1, keepdims=True)
    xc = x - mu
    var = jnp.mean(xc * xc, -1, keepdims=True)
    return xc * lax.rsqrt(var + LN_EPS) * g + b


def _rows_in_batch(blk, bm, lp):
    base = lax.rem(blk, lp // bm) * bm
    return base + lax.broadcasted_iota(jnp.int32, (bm, 1), 0)


def _pick(n, cands):
    for c in cands:
        if n % c == 0:
            return c
    raise ValueError(f"no block size for {n} among {cands}")


def _embed_kernel(x_ref, g_ref, b_ref, h_ref, hb_ref, *, bm, lp):
    y = _ln(x_ref[...], g_ref[...], b_ref[...])
    y = jnp.where(_rows_in_batch(pl.program_id(0), bm, lp) >= PAD_ROWS, y, 0.0)
    h_ref[...] = y
    hb_ref[...] = y.astype(BF16)


def _embed_ln(xcat, g, b, lp):
    t, d = xcat.shape
    bm = _pick(lp, (528, 384, 192, 128, 64, 8))
    row = pl.BlockSpec((bm, d), lambda i: (i, 0))
    vec = pl.BlockSpec((1, d), lambda i: (0, 0))
    return pl.pallas_call(
        functools.partial(_embed_kernel, bm=bm, lp=lp),
        grid=(t // bm,),
        in_specs=[row, vec, vec],
        out_specs=[row, row],
        out_shape=[jax.ShapeDtypeStruct((t, d), F32), jax.ShapeDtypeStruct((t, d), BF16)],
        compiler_params=_cparams(("parallel",), 40),
        name="embed_ln",
    )(xcat, g.reshape(1, d), b.reshape(1, d))


def _combine_ln_kernel(h_ref, g0_ref, g1_ref, ew_ref, g_ref, b_ref, o_ref, ob_ref, *, bm, lp, alpha):
    y = ew_ref[:, 0:1] * g0_ref[...] + ew_ref[:, 1:2] * g1_ref[...]
    y = _ln(alpha * h_ref[...] + y, g_ref[...], b_ref[...])
    y = jnp.where(_rows_in_batch(pl.program_id(0), bm, lp) >= PAD_ROWS, y, 0.0)
    o_ref[...] = y
    ob_ref[...] = y.astype(BF16)


def _combine_ln(h, g0, g1, ew, g, b, lp, alpha):
    t, d = h.shape
    bm = _pick(lp, (528, 384, 192, 128, 64, 8))
    row = pl.BlockSpec((bm, d), lambda i: (i, 0))
    vec = pl.BlockSpec((1, d), lambda i: (0, 0))
    return pl.pallas_call(
        functools.partial(_combine_ln_kernel, bm=bm, lp=lp, alpha=alpha),
        grid=(t // bm,),
        in_specs=[row, row, row, pl.BlockSpec((bm, LANES), lambda i: (i, 0)), vec, vec],
        out_specs=[row, row],
        out_shape=[jax.ShapeDtypeStruct((t, d), F32), jax.ShapeDtypeStruct((t, d), BF16)],
        compiler_params=_cparams(("parallel",), 56),
        name="combine_ln",
    )(h, g0, g1, ew, g.reshape(1, d), b.reshape(1, d))


def _final_ln_kernel(h_ref, g0_ref, g1_ref, ew_ref, g_ref, b_ref, o_ref, *, alpha):
    y = ew_ref[:, 0:1] * g0_ref[...] + ew_ref[:, 1:2] * g1_ref[...]
    o_ref[...] = _ln(alpha * h_ref[...] + y, g_ref[...], b_ref[...])


def _final_ln(h, g0, g1, ew, g, b, nb, lp, seq, alpha):
    d = h.shape[1]
    head = lp - seq
    bm = _pick(head, (128, 64, 8))
    assert seq % bm == 0
    per_seq, per_lp, skip = seq // bm, lp // bm, head // bm
    src = lambda i: ((i // per_seq) * per_lp + skip + i % per_seq, 0)
    row = pl.BlockSpec((bm, d), src)
    vec = pl.BlockSpec((1, d), lambda i: (0, 0))
    return pl.pallas_call(
        functools.partial(_final_ln_kernel, alpha=alpha),
        grid=(nb * per_seq,),
        in_specs=[row, row, row, pl.BlockSpec((bm, LANES), src), vec, vec],
        out_specs=pl.BlockSpec((bm, d), lambda i: (i, 0)),
        out_shape=jax.ShapeDtypeStruct((nb * seq, d), F32),
        compiler_params=_cparams(("parallel",), 32),
        name="final_ln",
    )(h, g0, g1, ew, g.reshape(1, d), b.reshape(1, d))


def _mm_kernel(x_ref, w_ref, *rest, shift):
    if shift:
        wnext_ref, o_ref, wbf_ref = rest
    else:
        o_ref, wbf_ref = rest
    bn = w_ref.shape[0]
    step = min(bn, 256)

    @pl.when(pl.program_id(1) == 0)
    def _():
        for c in range(bn // step):
            lo, hi = shift + c * step, shift + (c + 1) * step
            if hi <= bn:
                rows = w_ref[lo:hi, :]
            else:
                rows = jnp.concatenate([w_ref[lo:bn, :], wnext_ref[0:hi - bn, :]], axis=0)
            wbf_ref[:, c * step:(c + 1) * step] = rows.T.astype(BF16)

    o_ref[...] = jnp.dot(x_ref[...], wbf_ref[...], preferred_element_type=F32).astype(o_ref.dtype)


def _mm(x, w_t, layer, col0, n_cols, out_dtype=F32):
    t, k = x.shape
    bn = _pick(n_cols, (1024, 512, 256, 128))
    bm = _pick(t, (1056, 528, 384, 192, 128, 64, 8))
    shift = col0 % LANES
    base = col0 - shift
    assert base % bn == 0 and shift % 8 == 0
    cb0 = base // bn
    in_specs = [pl.BlockSpec((bm, k), lambda n, m: (m, 0)),
                pl.BlockSpec((None, bn, k), lambda n, m: (layer, cb0 + n, 0))]
    args = [x, w_t]
    if shift:
        tiles = bn // LANES
        in_specs.append(pl.BlockSpec((None, LANES, k), lambda n, m: (layer, (cb0 + n + 1) * tiles, 0)))
        args.append(w_t)
    return pl.pallas_call(
        functools.partial(_mm_kernel, shift=shift),
        grid=(n_cols // bn, t // bm),
        in_specs=in_specs,
        out_specs=pl.BlockSpec((bm, bn), lambda n, m: (m, n)),
        out_shape=jax.ShapeDtypeStruct((t, n_cols), out_dtype),
        scratch_shapes=[pltpu.VMEM((k, bn), BF16)],
        compiler_params=_cparams(("arbitrary", "arbitrary"), 48),
        name="mm",
    )(*args)


def _mm_ln_kernel(x_ref, w_ref, h_ref, g_ref, b_ref, o_ref, wbf_ref, *, bm, lp, alpha):
    @pl.when(pl.program_id(0) == 0)
    def _():
        wbf_ref[...] = w_ref[...].astype(BF16)

    y = jnp.dot(x_ref[...], wbf_ref[...], preferred_element_type=F32)
    y = _ln(alpha * h_ref[...] + y, g_ref[...], b_ref[...])
    o_ref[...] = jnp.where(_rows_in_batch(pl.program_id(0), bm, lp) >= PAD_ROWS, y, 0.0)


def _mm_ln(x, w, layer, h, g, b, lp, alpha):
    t, k = x.shape
    d = w.shape[-1]
    bm = _pick(lp, (528, 384, 192, 128, 64, 8))
    row = lambda width: pl.BlockSpec((bm, width), lambda i: (i, 0))
    vec = pl.BlockSpec((1, d), lambda i: (0, 0))
    return pl.pallas_call(
        functools.partial(_mm_ln_kernel, bm=bm, lp=lp, alpha=alpha),
        grid=(t // bm,),
        in_specs=[row(k), pl.BlockSpec((None, k, d), lambda i: (layer, 0, 0), pipeline_mode=pl.Buffered(1)),
                  row(d), vec, vec],
        out_specs=row(d),
        out_shape=jax.ShapeDtypeStruct((t, d), F32),
        scratch_shapes=[pltpu.VMEM((k, d), BF16)],
        compiler_params=_cparams(("arbitrary",), 56),
        name="mm_ln",
    )(x, w, h, g.reshape(1, d), b.reshape(1, d))


def _ssd_kernel(z_ref, xbc_ref, dtr_ref, cw_ref, cb_ref, dtb_ref, alog_ref, dexp_ref, ng_ref, o_ref,
                cbuf, xc, ybuf, st, *, q, d_inner, n_state):
    c = pl.program_id(1)
    xbc_w = cbuf.shape[1]
    kc = cw_ref.shape[0]
    hist = 8

    @pl.when(c == 0)
    def _():
        cbuf[0:hist, :] = jnp.zeros((hist, xbc_w), F32)
        st[...] = jnp.zeros(st.shape, F32)

    cbuf[hist:hist + q, :] = xbc_ref[...]
    cw = 128
    for j in range(xbc_w // cw):
        sl = slice(j * cw, (j + 1) * cw)
        conv = cb_ref[:, sl] + cw_ref[kc - 1:kc, sl] * cbuf[hist:hist + q, sl]
        for k in range(kc - 1):
            off = hist - (kc - 1) + k
            conv = conv + cw_ref[k:k + 1, sl] * cbuf[off:off + q, sl]
        xc[:, sl] = conv * _sigmoid(conv)
    cbuf[0:hist, :] = cbuf[q:q + hist, :]

    row = c * q + lax.broadcasted_iota(jnp.int32, (q, 1), 0)
    dt = _softplus(dtr_ref[...] + dtb_ref[...])
    dt = jnp.where(row >= PAD_ROWS, dt, 0.0)
    adt = dt * (-jnp.exp(alog_ref[...]))
    li = lax.broadcasted_iota(jnp.int32, (q, q), 0)
    si = lax.broadcasted_iota(jnp.int32, (q, q), 1)
    tri = li >= si
    acs = jnp.dot(tri.astype(F32), adt, preferred_element_type=F32,
                  precision=lax.Precision.HIGHEST) * LOG2E
    acs_t = acs.T
    dt_t = dt.T
    e_acs = jnp.exp2(acs)
    last = acs_t[:, q - 1:q]
    w_t = jnp.exp2(last - acs_t) * dt_t
    src_t = acs_t - jnp.log2(dt_t)
    dlast = jnp.exp2(last)
    lane = lax.broadcasted_iota(jnp.int32, (1, LANES), 1)
    lo_half = lane < SSD_HEAD_DIM

    heads_per_group = d_inner // SSD_HEAD_DIM // SSD_GROUPS
    pairs_per_group = heads_per_group // 2
    for g in range(SSD_GROUPS):
        b0 = d_inner + g * n_state
        c0 = d_inner + SSD_GROUPS * n_state + g * n_state
        bg = xc[:, b0:b0 + n_state]
        cg = xc[:, c0:c0 + n_state]
        cbm = lax.dot_general(cg.astype(BF16), bg.astype(BF16), (((1,), (1,)), ((), ())),
                              preferred_element_type=F32)
        bg_t = bg.T
        for j in range(pairs_per_group):
            pair = g * pairs_per_group + j
            h0 = 2 * pair
            xs_pair = xc[:, h0 * SSD_HEAD_DIM:h0 * SSD_HEAD_DIM + LANES]
            st_pair = st[pair]
            lhs_y, lhs_s, rhs_x, rhs_st = [], [], [], []
            for u in range(2):
                h = h0 + u
                col = acs[:, h:h + 1]
                rw = src_t[h:h + 1, :]
                decay = jnp.where(tri, jnp.exp2(col - rw), 0.0)
                lhs_y.append((cbm * decay).astype(BF16))
                lhs_s.append((bg_t * w_t[h:h + 1, :]).astype(BF16))
                keep = lo_half if u == 0 else jnp.logical_not(lo_half)
                rhs_x.append(jnp.where(keep, xs_pair, 0.0).astype(BF16))
                rhs_st.append(jnp.where(keep, st_pair, 0.0).astype(BF16))
            for u in range(2):
                lhs_y.append((e_acs[:, h0 + u:h0 + u + 1] * cg).astype(BF16))
            y_pair = jnp.dot(jnp.concatenate(lhs_y, axis=1), jnp.concatenate(rhs_x + rhs_st, axis=0),
                             preferred_element_type=F32)
            s_new = jnp.dot(jnp.concatenate(lhs_s, axis=1), jnp.concatenate(rhs_x, axis=0),
                            preferred_element_type=F32)
            dpair = jnp.where(lo_half, dlast[h0:h0 + 1, :], dlast[h0 + 1:h0 + 2, :])
            st[pair] = st_pair * dpair + s_new
            ybuf[:, h0 * SSD_HEAD_DIM:h0 * SSD_HEAD_DIM + LANES] = y_pair

    gw = d_inner // SSD_GROUPS
    for g in range(SSD_GROUPS):
        sl = slice(g * gw, (g + 1) * gw)
        y = ybuf[:, sl] + xc[:, sl] * dexp_ref[:, sl]
        zz = z_ref[:, sl]
        y = y * (zz * _sigmoid(zz))
        y = y * lax.rsqrt(jnp.mean(y * y, -1, keepdims=True) + LN_EPS)
        o_ref[:, sl] = (y * ng_ref[:, sl]).astype(BF16)


def _ssd(z, xbc, dtr, conv_w, conv_b, dt_bias, a_log, d_skip, norm_g, nb, lp):
    t, d_inner = z.shape
    xbc_w = xbc.shape[1]
    q = SSD_CHUNK
    n_heads = d_inner // SSD_HEAD_DIM
    n_state = (xbc_w - d_inner) // (2 * SSD_GROUPS)
    assert n_state == LANES and n_heads % (2 * SSD_GROUPS) == 0 and lp % q == 0
    assert xbc_w % 512 == 0 and PAD_ROWS % 8 == 0
    nc = lp // q
    padh = LANES - n_heads
    row = lambda w: pl.BlockSpec((q, w), lambda b, c: (b * nc + c, 0))
    full = lambda a: pl.BlockSpec(a.shape, lambda b, c: (0,) * a.ndim)
    dtb = jnp.pad(dt_bias, (0, padh)).reshape(1, LANES)
    alog = jnp.pad(a_log, (0, padh)).reshape(1, LANES)
    dexp = jnp.repeat(d_skip, SSD_HEAD_DIM).reshape(1, d_inner)
    cb = conv_b.reshape(1, xbc_w)
    ng = norm_g.reshape(1, d_inner)
    return pl.pallas_call(
        functools.partial(_ssd_kernel, q=q, d_inner=d_inner, n_state=n_state),
        grid=(nb, nc),
        in_specs=[row(d_inner), row(xbc_w), row(LANES), full(conv_w), full(cb), full(dtb), full(alog),
                  full(dexp), full(ng)],
        out_specs=row(d_inner),
        out_shape=jax.ShapeDtypeStruct((t, d_inner), BF16),
        scratch_shapes=[pltpu.VMEM((8 + q, xbc_w), F32), pltpu.VMEM((q, xbc_w), F32),
                        pltpu.VMEM((q, d_inner), F32), pltpu.VMEM((n_heads // 2, n_state, LANES), F32)],
        compiler_params=_cparams(("arbitrary", "arbitrary"), 32),
        name="ssd",
    )(z, xbc, dtr, conv_w, cb, dtb, alog, dexp, ng)


def _pool_kernel(u_ref, w_ref, sc_ref, o_ref, buf, *, r, gd):
    c = pl.program_id(1)
    hist = 16
    width = buf.shape[1]

    @pl.when(c == 0)
    def _():
        buf[0:hist, :] = jnp.zeros((hist, width), F32)

    buf[hist:hist + r, :] = u_ref[...]
    pos = c * r + lax.broadcasted_iota(jnp.int32, (r, 1), 0) - PAD_ROWS
    for gi, win in enumerate(POOL_WINDOWS):
        sl = slice(gi * gd, (gi + 1) * gd)
        u = buf[hist:hist + r, sl]
        s = u
        for j in range(1, win):
            s = s + buf[hist - j:hist - j + r, sl]
        cnt = jnp.clip(pos + 1, 1, win).astype(F32)
        mixed = s / cnt - u
        y = jnp.dot(mixed.astype(BF16), w_ref[gi].astype(BF16), preferred_element_type=F32)
        o_ref[:, sl] = (y * sc_ref[:, sl]).astype(BF16)
    buf[0:hist, :] = buf[r:r + hist, :]


def _pool(pc, pool_w, scale, nb, lp):
    t = pc.shape[0]
    ng, gd, _ = pool_w.shape
    width = ng * gd
    assert ng == len(POOL_WINDOWS) and gd % LANES == 0
    r = _pick(lp, (384, 192, 128, 64))
    nc = lp // r
    return pl.pallas_call(
        functools.partial(_pool_kernel, r=r, gd=gd),
        grid=(nb, nc),
        in_specs=[pl.BlockSpec((r, width), lambda b, c: (b * nc + c, 0)),
                  pl.BlockSpec(pool_w.shape, lambda b, c: (0, 0, 0)),
                  pl.BlockSpec((1, width), lambda b, c: (0, 0))],
        out_specs=pl.BlockSpec((r, width), lambda b, c: (b * nc + c, 0)),
        out_shape=jax.ShapeDtypeStruct((t, width), BF16),
        scratch_shapes=[pltpu.VMEM((16 + r, width), F32)],
        compiler_params=_cparams(("arbitrary", "arbitrary"), 32),
        name="pool",
    )(pc, pool_w, scale.reshape(1, width))


def _conf_kernel(a_ref, g_ref, w_ref, b_ref, lg_ref, lb_ref, o_ref, vbuf, cv, ubuf, *, r):
    c = pl.program_id(1)
    hist = 32
    sub = 8
    width = vbuf.shape[1]
    kc = w_ref.shape[0]

    @pl.when(c == 0)
    def _():
        vbuf[0:hist, :] = jnp.zeros((hist, width), F32)

    vbuf[hist:hist + r, :] = a_ref[...] * _sigmoid(g_ref[...])
    ext = r + sub
    cw = ubuf.shape[2]
    for jc in range(width // cw):
        sl = slice(jc * cw, (jc + 1) * cw)
        acc = None
        for rr in range(sub):
            u = None
            for j in range((kc - 1 - rr) // sub + 1):
                k = kc - 1 - (sub * j + rr)
                lo = hist - sub - sub * j
                term = w_ref[k:k + 1, sl] * vbuf[lo:lo + ext, sl]
                u = term if u is None else u + term
            if rr == 0:
                acc = b_ref[:, sl] + u[sub:sub + r]
            else:
                ubuf[rr - 1] = u
                acc = acc + ubuf[rr - 1, sub - rr:sub - rr + r, :]
        cv[:, sl] = acc
    vbuf[0:hist, :] = vbuf[r:r + hist, :]
    v = _ln(cv[...], lg_ref[...], lb_ref[...])
    o_ref[...] = (v * _sigmoid(v)).astype(BF16)


def _conf(pc, col_blk, dw_w, dw_b, ln_g, ln_b, nb, lp):
    t = pc.shape[0]
    kc, width = dw_w.shape
    assert kc <= 33 and width % 256 == 0
    r = _pick(lp, (192, 128, 64))
    nc = lp // r
    vec = lambda: pl.BlockSpec((1, width), lambda b, c: (0, 0))
    return pl.pallas_call(
        functools.partial(_conf_kernel, r=r),
        grid=(nb, nc),
        in_specs=[pl.BlockSpec((r, width), lambda b, c: (b * nc + c, col_blk)),
                  pl.BlockSpec((r, width), lambda b, c: (b * nc + c, col_blk + 1)),
                  pl.BlockSpec((kc, width), lambda b, c: (0, 0)), vec(), vec(), vec()],
        out_specs=pl.BlockSpec((r, width), lambda b, c: (b * nc + c, 0)),
        out_shape=jax.ShapeDtypeStruct((t, width), BF16),
        scratch_shapes=[pltpu.VMEM((32 + r, width), F32), pltpu.VMEM((r, width), F32),
                        pltpu.VMEM((7, r + 8, LANES), F32)],
        compiler_params=_cparams(("arbitrary", "arbitrary"), 32),
        name="conf",
    )(pc, pc, dw_w, dw_b.reshape(1, width), ln_g.reshape(1, width), ln_b.reshape(1, width))


def _merge_kernel(hb_ref, f0_ref, f1_ref, f2_ref, g0_ref, g1_ref, g2_ref, gb_ref, p0_ref, p1_ref, p2_ref,
                  o_ref, gs, s0, s1, s2):
    @pl.when(pl.program_id(1) == 0)
    def _():
        gs[0] = g0_ref[...].astype(BF16)
        gs[1] = g1_ref[...].astype(BF16)
        gs[2] = g2_ref[...].astype(BF16)
        s0[...] = p0_ref[...].astype(BF16)
        s1[...] = p1_ref[...].astype(BF16)
        s2[...] = p2_ref[...].astype(BF16)

    hb = hb_ref[...]
    acc = None
    for j, (f_ref, s_ref) in enumerate(((f0_ref, s0), (f1_ref, s1), (f2_ref, s2))):
        gate = _sigmoid(jnp.dot(hb, gs[j], preferred_element_type=F32) + gb_ref[j:j + 1, :])
        term = gate * jnp.dot(f_ref[...], s_ref[...], preferred_element_type=F32)
        acc = term if acc is None else acc + term
    o_ref[...] = acc.astype(BF16)


def _merge(hb, f_ssd, f_pool, f_conf, gate_w, gate_b, ssd_proj, pool_proj, conf_proj, layer):
    t, d = hb.shape
    bn = _pick(d, (512, 256, 128))
    bm = _pick(t, (528, 384, 192, 128, 64, 8))
    nbn = d // bn
    once = pl.Buffered(1)
    rowspec = lambda a: pl.BlockSpec((bm, a.shape[1]), lambda n, m: (m, 0))
    gspec = lambda j: pl.BlockSpec((None, d, bn), lambda n, m: (layer, 0, j * nbn + n), pipeline_mode=once)
    pspec = lambda w: pl.BlockSpec((None, w.shape[1], bn), lambda n, m: (layer, 0, n), pipeline_mode=once)
    gb = gate_b[layer].reshape(3, d)
    return pl.pallas_call(
        _merge_kernel,
        grid=(nbn, t // bm),
        in_specs=[rowspec(hb), rowspec(f_ssd), rowspec(f_pool), rowspec(f_conf),
                  gspec(0), gspec(1), gspec(2), pl.BlockSpec((3, bn), lambda n, m: (0, n)),
                  pspec(ssd_proj), pspec(pool_proj), pspec(conf_proj)],
        out_specs=pl.BlockSpec((bm, bn), lambda n, m: (m, n)),
        out_shape=jax.ShapeDtypeStruct((t, d), BF16),
        scratch_shapes=[pltpu.VMEM((3, d, bn), BF16), pltpu.VMEM((ssd_proj.shape[1], bn), BF16),
                        pltpu.VMEM((pool_proj.shape[1], bn), BF16), pltpu.VMEM((conf_proj.shape[1], bn), BF16)],
        compiler_params=_cparams(("arbitrary", "arbitrary"), 52),
        name="merge",
    )(hb, f_ssd, f_pool, f_conf, gate_w, gate_w, gate_w, gb, ssd_proj, pool_proj, conf_proj)


def _router_kernel(h_ref, w_ref, b_ref, o_ref, cnt_ref, run, *, bm, lp, n_groups, per_group):
    @pl.when(pl.program_id(0) == 0)
    def _():
        run[...] = jnp.zeros(run.shape, F32)

    h = h_ref[...]
    w = w_ref[...]
    h_hi = h.astype(BF16)
    h_lo = (h - h_hi.astype(F32)).astype(BF16)
    w_hi = w.astype(BF16)
    w_lo = (w - w_hi.astype(F32)).astype(BF16)
    logits = (jnp.dot(h_hi, w_hi, preferred_element_type=F32)
              + (jnp.dot(h_hi, w_lo, preferred_element_type=F32)
                 + jnp.dot(h_lo, w_hi, preferred_element_type=F32))) + b_ref[...]
    lane = lax.broadcasted_iota(jnp.int32, logits.shape, 1).astype(F32)

    def first_max(vals):
        m = jnp.max(vals, axis=1, keepdims=True)
        idx = jnp.min(jnp.where(vals == m, lane, float(LANES)), axis=1, keepdims=True)
        return m, idx

    gmask = lane < n_groups
    gmax, gsel = first_max(jnp.where(gmask, logits, NEG_BIG))
    gsum = jnp.sum(jnp.where(gmask, jnp.exp(logits - gmax), 0.0), axis=1, keepdims=True)
    p_group = 1.0 / gsum
    lo = n_groups + per_group * gsel
    el = jnp.where(jnp.logical_and(lane >= lo, lane < lo + per_group), logits, NEG_BIG)
    m1, i1 = first_max(el)
    m2, i2 = first_max(jnp.where(lane == i1, NEG_BIG, el))
    ratio = jnp.exp(m2 - m1)
    w1 = p_group / (1.0 + ratio)
    w2 = w1 * ratio
    real = _rows_in_batch(pl.program_id(0), bm, lp) >= PAD_ROWS
    sentinel = float(n_groups * per_group)
    e1 = jnp.where(real, i1 - n_groups, sentinel)
    e2 = jnp.where(real, i2 - n_groups, sentinel)
    w1 = jnp.where(real, w1, 0.0)
    w2 = jnp.where(real, w2, 0.0)
    hit1 = jnp.logical_and(lane == i1, real)
    hit2 = jnp.logical_and(lane == i2, real)
    onehot = jnp.where(jnp.logical_or(hit1, hit2), 1.0, 0.0)
    li = lax.broadcasted_iota(jnp.int32, (bm, bm), 0)
    si = lax.broadcasted_iota(jnp.int32, (bm, bm), 1)
    before = jnp.where(li > si, 1.0, 0.0).astype(BF16)
    seen = jnp.dot(before, onehot.astype(BF16), preferred_element_type=F32) + run[...]
    r1 = jnp.sum(jnp.where(hit1, seen, 0.0), axis=1, keepdims=True)
    r2 = jnp.sum(jnp.where(hit2, seen, 0.0), axis=1, keepdims=True)
    run[...] = run[...] + jnp.sum(onehot, axis=0, keepdims=True)
    cnt_ref[...] = run[...]
    out = jnp.zeros(logits.shape, F32)
    for k, v in enumerate((w1, w2, e1, e2, r1, r2)):
        out = jnp.where(lane == k, v, out)
    o_ref[...] = out


def _router(h, rw, rb, lp, n_groups, per_group):
    t, d = h.shape
    bm = _pick(lp, (528, 384, 192, 128, 64, 8))
    return pl.pallas_call(
        functools.partial(_router_kernel, bm=bm, lp=lp, n_groups=n_groups, per_group=per_group),
        grid=(t // bm,),
        in_specs=[pl.BlockSpec((bm, d), lambda i: (i, 0)), pl.BlockSpec((d, LANES), lambda i: (0, 0)),
                  pl.BlockSpec((1, LANES), lambda i: (0, 0))],
        out_specs=[pl.BlockSpec((bm, LANES), lambda i: (i, 0)), pl.BlockSpec((1, LANES), lambda i: (0, 0))],
        out_shape=[jax.ShapeDtypeStruct((t, LANES), F32), jax.ShapeDtypeStruct((1, LANES), F32)],
        scratch_shapes=[pltpu.VMEM((1, LANES), F32)],
        compiler_params=_cparams(("arbitrary",), 40),
        name="router",
    )(h, rw, rb)


def _moe_kernel(be_ref, nxt_ref, tot_ref, x_ref, wg_hbm, wu_hbm, wd_hbm, o_ref,
                wg_f, wu_f, wd_f, wg_s, wu_s, wd_s, sems, *, layer):
    i = pl.program_id(0)
    total = tot_ref[0]
    ic = jnp.maximum(jnp.minimum(i, total - 1), 0)
    e = be_ref[ic]
    e_prev = be_ref[jnp.maximum(ic - 1, 0)]
    active = i < total
    first_of_run = jnp.logical_and(active, jnp.logical_or(i == 0, e != e_prev))

    def weight_copies(expert):
        return (pltpu.make_async_copy(wg_hbm.at[layer, expert], wg_f, sems.at[0]),
                pltpu.make_async_copy(wu_hbm.at[layer, expert], wu_f, sems.at[1]),
                pltpu.make_async_copy(wd_hbm.at[layer, expert], wd_f, sems.at[2]))

    @pl.when(jnp.logical_and(active, i == 0))
    def _():
        for cp in weight_copies(e):
            cp.start()

    @pl.when(first_of_run)
    def _():
        for cp, src, dst in zip(weight_copies(e), (wg_f, wu_f, wd_f), (wg_s, wu_s, wd_s)):
            cp.wait()
            dst[...] = src[...].astype(BF16)
        nxt = nxt_ref[ic]

        @pl.when(nxt >= 0)
        def _():
            for cp in weight_copies(nxt):
                cp.start()

    @pl.when(active)
    def _():
        x = x_ref[...].astype(BF16)
        gte = jnp.dot(x, wg_s[...], preferred_element_type=F32)
        up = jnp.dot(x, wu_s[...], preferred_element_type=F32)
        hid = (gte * _sigmoid(gte)) * up
        o_ref[...] = jnp.dot(hid.astype(BF16), wd_s[...], preferred_element_type=F32)

    @pl.when(jnp.logical_not(active))
    def _():
        o_ref[...] = jnp.zeros(o_ref.shape, F32)


def _moe_ffn(xg, block_expert, next_expert, total_blocks, w_gate, w_up, w_down, layer):
    n_slots, d = xg.shape
    hid = w_gate.shape[-1]
    n_blocks = n_slots // MOE_BLOCK

    def rows(i, be, nxt, tot):
        return (jnp.maximum(jnp.minimum(i, tot[0] - 1), 0), 0)

    hbm = pl.BlockSpec(memory_space=pl.ANY)
    grid_spec = pltpu.PrefetchScalarGridSpec(
        num_scalar_prefetch=3,
        grid=(n_blocks,),
        in_specs=[pl.BlockSpec((MOE_BLOCK, d), rows), hbm, hbm, hbm],
        out_specs=pl.BlockSpec((MOE_BLOCK, d), lambda i, be, nxt, tot: (i, 0)),
        scratch_shapes=[pltpu.VMEM((d, hid), F32), pltpu.VMEM((d, hid), F32), pltpu.VMEM((hid, d), F32),
                        pltpu.VMEM((d, hid), BF16), pltpu.VMEM((d, hid), BF16), pltpu.VMEM((hid, d), BF16),
                        pltpu.SemaphoreType.DMA((3,))],
    )
    return pl.pallas_call(
        functools.partial(_moe_kernel, layer=layer),
        grid_spec=grid_spec,
        out_shape=jax.ShapeDtypeStruct((n_slots, d), F32),
        compiler_params=_cparams(("arbitrary",), 48),
        name="moe_ffn",
    )(block_expert, next_expert, total_blocks, xg, w_gate, w_up, w_down)


def _dispatch(ew, cnt, n_groups, n_experts):
    t = ew.shape[0]
    eid = ew[:, 2:4].astype(jnp.int32).reshape(-1)
    rank = ew[:, 4:6].astype(jnp.int32).reshape(-1)
    counts = cnt[0, n_groups:n_groups + n_experts].astype(jnp.int32)
    n_assign = eid.shape[0]
    tok = jnp.repeat(jnp.arange(t, dtype=jnp.int32), 2)
    padded = (counts + MOE_BLOCK - 1) // MOE_BLOCK * MOE_BLOCK
    pend = jnp.cumsum(padded)
    pstart = pend - padded
    n_blocks = n_assign // MOE_BLOCK + n_experts
    n_slots = n_blocks * MOE_BLOCK
    valid = eid < n_experts
    slot = jnp.where(valid, pstart[jnp.minimum(eid, n_experts - 1)] + rank, n_slots)
    slot_tok = (jnp.arange(n_slots, dtype=jnp.int32) % t).at[slot].set(tok, mode="drop")
    block_expert = jnp.minimum(
        jnp.searchsorted(pend, jnp.arange(n_blocks, dtype=jnp.int32) * MOE_BLOCK, side="right"),
        n_experts - 1).astype(jnp.int32)
    total_blocks = (pend[-1] // MOE_BLOCK).astype(jnp.int32).reshape(1)
    ids = jnp.arange(n_experts, dtype=jnp.int32)
    later = jnp.where(counts > 0, ids, n_experts)
    nxt = lax.cummin(jnp.concatenate([later[1:], jnp.full((1,), n_experts, jnp.int32)]), reverse=True)
    nxt = jnp.where(nxt >= n_experts, -1, nxt)
    next_expert = nxt[block_expert]
    slot_c = jnp.where(valid, slot, 0).reshape(t, 2)
    return slot_tok, block_expert, next_expert, total_blocks, slot_c


def kernel(x, meta_tokens, ln_emb_g, ln_emb_b, w_in, ssd_conv_w, ssd_conv_b, ssd_dt_bias, ssd_a_log, ssd_d,
           ssd_norm_g, ssd_proj, pool_w, pool_scale, pool_proj, conf_dw_w, conf_dw_b, conf_ln_g, conf_ln_b,
           conf_proj, gate_w, gate_b, w_out, ln1_g, ln1_b, router_group_w, router_group_b, router_expert_w,
           router_expert_b, exp_w_gate, exp_w_up, exp_w_down, ln2_g, ln2_b):
    nb, seq, d = x.shape
    n_meta = meta_tokens.shape[0]
    depth = w_in.shape[0]
    lp = PAD_ROWS + n_meta + seq
    t = nb * lp
    alpha = (2.0 * depth) ** 0.25
    d_inner = ssd_norm_g.shape[-1]
    xbc_w = ssd_conv_w.shape[-1]
    n_heads = ssd_a_log.shape[-1]
    pool_width = pool_scale.shape[-1]
    conf_width = conf_dw_b.shape[-1]
    n_groups = router_group_w.shape[-1]
    n_experts = router_expert_w.shape[-1]
    per_group = n_experts // n_groups
    assert (PAD_ROWS + n_meta) % SSD_CHUNK == 0 and lp % SSD_CHUNK == 0
    assert n_groups + n_experts <= LANES and n_heads <= LANES

    meta = jnp.broadcast_to(meta_tokens[None].astype(x.dtype), (nb, n_meta, d))
    xcat = jnp.concatenate([jnp.zeros((nb, PAD_ROWS, d), x.dtype), meta, x], axis=1).reshape(t, d)
    h, hb = _embed_ln(xcat, ln_emb_g, ln_emb_b, lp)

    c_dt = d_inner + xbc_w
    c_pc = c_dt + n_heads
    assert c_dt % LANES == 0 and c_dt + LANES <= w_in.shape[-1]
    w_in_t = jnp.swapaxes(w_in, 1, 2)
    rw = jnp.concatenate([router_group_w, router_expert_w], axis=-1)
    rw = jnp.pad(rw, ((0, 0), (0, 0), (0, LANES - rw.shape[-1])))
    rb = jnp.concatenate([router_group_b, router_expert_b], axis=-1)
    rb = jnp.pad(rb, ((0, 0), (0, LANES - rb.shape[-1]))).reshape(depth, 1, LANES)

    for i in range(depth):
        z = _mm(hb, w_in_t, i, 0, d_inner)
        xbc = _mm(hb, w_in_t, i, d_inner, xbc_w)
        dtr = _mm(hb, w_in_t, i, c_dt, LANES)
        pc = _mm(hb, w_in_t, i, c_pc, pool_width + 2 * conf_width)
        f_ssd = _ssd(z, xbc, dtr, ssd_conv_w[i], ssd_conv_b[i], ssd_dt_bias[i], ssd_a_log[i], ssd_d[i],
                     ssd_norm_g[i], nb, lp)
        f_pool = _pool(pc, pool_w[i], pool_scale[i], nb, lp)
        assert pool_width % conf_width == 0
        f_conf = _conf(pc, pool_width // conf_width, conf_dw_w[i], conf_dw_b[i], conf_ln_g[i], conf_ln_b[i], nb, lp)
        merged = _merge(hb, f_ssd, f_pool, f_conf, gate_w, gate_b, ssd_proj, pool_proj, conf_proj, i)
        h = _mm_ln(merged, w_out, i, h, ln1_g[i], ln1_b[i], lp, alpha)

        ew, cnt = _router(h, rw[i], rb[i], lp, n_groups, per_group)
        slot_tok, block_expert, next_expert, total_blocks, slot_c = _dispatch(ew, cnt, n_groups, n_experts)
        xg = h.at[slot_tok].get(mode="promise_in_bounds")
        out = _moe_ffn(xg, block_expert, next_expert, total_blocks, exp_w_gate, exp_w_up, exp_w_down, i)
        g0 = out.at[slot_c[:, 0]].get(mode="promise_in_bounds")
        g1 = out.at[slot_c[:, 1]].get(mode="promise_in_bounds")
        if i + 1 < depth:
            h, hb = _combine_ln(h, g0, g1, ew, ln2_g[i], ln2_b[i], lp, alpha)
        else:
            h = _final_ln(h, g0, g1, ew, ln2_g[i], ln2_b[i], nb, lp, seq, alpha)

    return h.reshape(nb, seq, d)
```

```python
import functools

import jax
import jax.numpy as jnp
from jax import lax
from jax.experimental import pallas as pl
from jax.experimental.pallas import tpu as pltpu

F32 = jnp.float32
BF16 = jnp.bfloat16

PAD_ROWS = 112
SSD_CHUNK = 128
SSD_GROUPS = 4
SSD_HEAD_DIM = 64
POOL_WINDOWS = (2, 4, 8, 16)
MOE_BLOCK = 256
LN_EPS = 1e-5
LANES = 128
NEG_BIG = -1e30
LOG2E = 1.4426950408889634


def _cparams(sem, vmem_mb):
    return pltpu.CompilerParams(dimension_semantics=sem, vmem_limit_bytes=vmem_mb << 20)


def _sigmoid(x):
    return 1.0 / (1.0 + jnp.exp(-x))


def _softplus(x):
    return jnp.maximum(x, 0.0) + jnp.log1p(jnp.exp(-jnp.abs(x)))


def _ln(x, g, b):
    mu = jnp.mean(x, -1, keepdims=True)
    xc = x - mu
    var = jnp.mean(xc * xc, -1, keepdims=True)
    return xc * lax.rsqrt(var + LN_EPS) * g + b


def _pack_halves(y):
    n = y.shape[1] // 2
    lo = lax.bitcast_convert_type(y[:, :n].astype(BF16).astype(F32), jnp.uint32)
    hi = lax.bitcast_convert_type(y[:, n:].astype(BF16).astype(F32), jnp.uint32)
    return lax.bitcast_convert_type(hi | (lo >> 16), F32)


def _unpack_halves(p):
    u = lax.bitcast_convert_type(p, jnp.uint32)
    lo = lax.bitcast_convert_type(u << 16, F32)
    hi = lax.bitcast_convert_type(u & jnp.uint32(0xFFFF0000), F32)
    return jnp.concatenate([lo, hi], axis=1)


def _rows_in_batch(blk, bm, lp):
    base = lax.rem(blk, lp // bm) * bm
    return base + lax.broadcasted_iota(jnp.int32, (bm, 1), 0)


def _pick(n, cands):
    for c in cands:
        if n % c == 0:
            return c
    raise ValueError(f"no block size for {n} among {cands}")


def _embed_kernel(x_ref, g_ref, b_ref, h_ref, hb_ref, *, bm, lp):
    y = _ln(x_ref[...], g_ref[...], b_ref[...])
    y = jnp.where(_rows_in_batch(pl.program_id(0), bm, lp) >= PAD_ROWS, y, 0.0)
    h_ref[...] = y
    hb_ref[...] = y.astype(BF16)


def _embed_ln(xcat, g, b, lp):
    t, d = xcat.shape
    bm = _pick(lp, (528, 384, 192, 128, 64, 8))
    row = pl.BlockSpec((bm, d), lambda i: (i, 0))
    vec = pl.BlockSpec((1, d), lambda i: (0, 0))
    return pl.pallas_call(
        functools.partial(_embed_kernel, bm=bm, lp=lp),
        grid=(t // bm,),
        in_specs=[row, vec, vec],
        out_specs=[row, row],
        out_shape=[jax.ShapeDtypeStruct((t, d), F32), jax.ShapeDtypeStruct((t, d), BF16)],
        compiler_params=_cparams(("parallel",), 40),
        name="embed_ln",
    )(xcat, g.reshape(1, d), b.reshape(1, d))


def _combine_ln_kernel(h_ref, g0_ref, g1_ref, ew_ref, g_ref, b_ref, o_ref, ob_ref, *, bm, lp, alpha):
    y = ew_ref[:, 0:1] * _unpack_halves(g0_ref[...]) + ew_ref[:, 1:2] * _unpack_halves(g1_ref[...])
    y = _ln(alpha * h_ref[...] + y, g_ref[...], b_ref[...])
    y = jnp.where(_rows_in_batch(pl.program_id(0), bm, lp) >= PAD_ROWS, y, 0.0)
    o_ref[...] = y
    ob_ref[...] = y.astype(BF16)


def _combine_ln(h, g0, g1, ew, g, b, lp, alpha):
    t, d = h.shape
    bm = _pick(lp, (528, 384, 192, 128, 64, 8))
    row = pl.BlockSpec((bm, d), lambda i: (i, 0))
    half = pl.BlockSpec((bm, d // 2), lambda i: (i, 0))
    vec = pl.BlockSpec((1, d), lambda i: (0, 0))
    return pl.pallas_call(
        functools.partial(_combine_ln_kernel, bm=bm, lp=lp, alpha=alpha),
        grid=(t // bm,),
        in_specs=[row, half, half, pl.BlockSpec((bm, LANES), lambda i: (i, 0)), vec, vec],
        out_specs=[row, row],
        out_shape=[jax.ShapeDtypeStruct((t, d), F32), jax.ShapeDtypeStruct((t, d), BF16)],
        compiler_params=_cparams(("parallel",), 56),
        name="combine_ln",
    )(h, g0, g1, ew, g.reshape(1, d), b.reshape(1, d))


def _final_ln_kernel(h_ref, g0_ref, g1_ref, ew_ref, g_ref, b_ref, o_ref, *, alpha):
    y = ew_ref[:, 0:1] * _unpack_halves(g0_ref[...]) + ew_ref[:, 1:2] * _unpack_halves(g1_ref[...])
    o_ref[...] = _ln(alpha * h_ref[...] + y, g_ref[...], b_ref[...])


def _final_ln(h, g0, g1, ew, g, b, nb, lp, seq, alpha):
    d = h.shape[1]
    head = lp - seq
    bm = _pick(head, (128, 64, 8))
    assert seq % bm == 0
    per_seq, per_lp, skip = seq // bm, lp // bm, head // bm
    src = lambda i: ((i // per_seq) * per_lp + skip + i % per_seq, 0)
    row = pl.BlockSpec((bm, d), src)
    half = pl.BlockSpec((bm, d // 2), src)
    vec = pl.BlockSpec((1, d), lambda i: (0, 0))
    return pl.pallas_call(
        functools.partial(_final_ln_kernel, alpha=alpha),
        grid=(nb * per_seq,),
        in_specs=[row, half, half, pl.BlockSpec((bm, LANES), src), vec, vec],
        out_specs=pl.BlockSpec((bm, d), lambda i: (i, 0)),
        out_shape=jax.ShapeDtypeStruct((nb * seq, d), F32),
        compiler_params=_cparams(("parallel",), 32),
        name="final_ln",
    )(h, g0, g1, ew, g.reshape(1, d), b.reshape(1, d))


def _mm_kernel(x_ref, w_ref, *rest, shift):
    if shift:
        wnext_ref, o_ref, wbf_ref = rest
    else:
        o_ref, wbf_ref = rest
    bn = w_ref.shape[0]
    step = min(bn, 256)

    @pl.when(pl.program_id(1) == 0)
    def _():
        for c in range(bn // step):
            lo, hi = shift + c * step, shift + (c + 1) * step
            if hi <= bn:
                rows = w_ref[lo:hi, :]
            else:
                rows = jnp.concatenate([w_ref[lo:bn, :], wnext_ref[0:hi - bn, :]], axis=0)
            wbf_ref[:, c * step:(c + 1) * step] = rows.T.astype(BF16)

    o_ref[...] = jnp.dot(x_ref[...], wbf_ref[...], preferred_element_type=F32).astype(o_ref.dtype)


def _mm(x, w_t, layer, col0, n_cols, out_dtype=F32):
    t, k = x.shape
    bn = _pick(n_cols, (1024, 512, 256, 128))
    bm = _pick(t, (1056, 528, 384, 192, 128, 64, 8))
    shift = col0 % LANES
    base = col0 - shift
    assert base % bn == 0 and shift % 8 == 0
    cb0 = base // bn
    in_specs = [pl.BlockSpec((bm, k), lambda n, m: (m, 0)),
                pl.BlockSpec((None, bn, k), lambda n, m: (layer, cb0 + n, 0))]
    args = [x, w_t]
    if shift:
        tiles = bn // LANES
        in_specs.append(pl.BlockSpec((None, LANES, k), lambda n, m: (layer, (cb0 + n + 1) * tiles, 0)))
        args.append(w_t)
    return pl.pallas_call(
        functools.partial(_mm_kernel, shift=shift),
        grid=(n_cols // bn, t // bm),
        in_specs=in_specs,
        out_specs=pl.BlockSpec((bm, bn), lambda n, m: (m, n)),
        out_shape=jax.ShapeDtypeStruct((t, n_cols), out_dtype),
        scratch_shapes=[pltpu.VMEM((k, bn), BF16)],
        compiler_params=_cparams(("arbitrary", "arbitrary"), 48),
        name="mm",
    )(*args)


def _mm_ln_kernel(x_ref, w_ref, h_ref, g_ref, b_ref, o_ref, op_ref, wbf_ref, *, bm, lp, alpha):
    @pl.when(pl.program_id(0) == 0)
    def _():
        wbf_ref[...] = w_ref[...].astype(BF16)

    y = jnp.dot(x_ref[...], wbf_ref[...], preferred_element_type=F32)
    y = _ln(alpha * h_ref[...] + y, g_ref[...], b_ref[...])
    y = jnp.where(_rows_in_batch(pl.program_id(0), bm, lp) >= PAD_ROWS, y, 0.0)
    o_ref[...] = y
    op_ref[...] = _pack_halves(y)


def _mm_ln(x, w, layer, h, g, b, lp, alpha):
    t, k = x.shape
    d = w.shape[-1]
    bm = _pick(lp, (528, 384, 192, 128, 64, 8))
    row = lambda width: pl.BlockSpec((bm, width), lambda i: (i, 0))
    vec = pl.BlockSpec((1, d), lambda i: (0, 0))
    return pl.pallas_call(
        functools.partial(_mm_ln_kernel, bm=bm, lp=lp, alpha=alpha),
        grid=(t // bm,),
        in_specs=[row(k), pl.BlockSpec((None, k, d), lambda i: (layer, 0, 0), pipeline_mode=pl.Buffered(1)),
                  row(d), vec, vec],
        out_specs=[row(d), row(d // 2)],
        out_shape=[jax.ShapeDtypeStruct((t, d), F32), jax.ShapeDtypeStruct((t, d // 2), F32)],
        scratch_shapes=[pltpu.VMEM((k, d), BF16)],
        compiler_params=_cparams(("arbitrary",), 56),
        name="mm_ln",
    )(x, w, h, g.reshape(1, d), b.reshape(1, d))


def _ssd_kernel(z_ref, xbc_ref, dtr_ref, cw_ref, cb_ref, dtb_ref, alog_ref, dexp_ref, ng_ref, o_ref,
                cbuf, xc, ybuf, st, *, q, d_inner, n_state):
    c = pl.program_id(1)
    xbc_w = cbuf.shape[1]
    kc = cw_ref.shape[0]
    hist = 8

    @pl.when(c == 0)
    def _():
        cbuf[0:hist, :] = jnp.zeros((hist, xbc_w), F32)
        st[...] = jnp.zeros(st.shape, F32)

    cbuf[hist:hist + q, :] = xbc_ref[...]
    cw = 128
    for j in range(xbc_w // cw):
        sl = slice(j * cw, (j + 1) * cw)
        conv = cb_ref[:, sl] + cw_ref[kc - 1:kc, sl] * cbuf[hist:hist + q, sl]
        for k in range(kc - 1):
            off = hist - (kc - 1) + k
            conv = conv + cw_ref[k:k + 1, sl] * cbuf[off:off + q, sl]
        xc[:, sl] = conv * _sigmoid(conv)
    cbuf[0:hist, :] = cbuf[q:q + hist, :]

    row = c * q + lax.broadcasted_iota(jnp.int32, (q, 1), 0)
    dt = _softplus(dtr_ref[...] + dtb_ref[...])
    dt = jnp.where(row >= PAD_ROWS, dt, 0.0)
    adt = dt * (-jnp.exp(alog_ref[...]))
    li = lax.broadcasted_iota(jnp.int32, (q, q), 0)
    si = lax.broadcasted_iota(jnp.int32, (q, q), 1)
    tri = li >= si
    acs = jnp.dot(tri.astype(F32), adt, preferred_element_type=F32,
                  precision=lax.Precision.HIGHEST) * LOG2E
    acs_t = acs.T
    dt_t = dt.T
    e_acs = jnp.exp2(acs)
    last = acs_t[:, q - 1:q]
    w_t = jnp.exp2(last - acs_t) * dt_t
    src_t = acs_t - jnp.log2(dt_t)
    dlast = jnp.exp2(last)
    lane = lax.broadcasted_iota(jnp.int32, (1, LANES), 1)
    lo_half = lane < SSD_HEAD_DIM

    heads_per_group = d_inner // SSD_HEAD_DIM // SSD_GROUPS
    pairs_per_group = heads_per_group // 2
    for g in range(SSD_GROUPS):
        b0 = d_inner + g * n_state
        c0 = d_inner + SSD_GROUPS * n_state + g * n_state
        bg = xc[:, b0:b0 + n_state]
        cg = xc[:, c0:c0 + n_state]
        cbm = lax.dot_general(cg.astype(BF16), bg.astype(BF16), (((1,), (1,)), ((), ())),
                              preferred_element_type=F32)
        bg_t = bg.T
        for j in range(pairs_per_group):
            pair = g * pairs_per_group + j
            h0 = 2 * pair
            xs_pair = xc[:, h0 * SSD_HEAD_DIM:h0 * SSD_HEAD_DIM + LANES]
            st_pair = st[pair]
            lhs_y, lhs_s, rhs_x, rhs_st = [], [], [], []
            for u in range(2):
                h = h0 + u
                col = acs[:, h:h + 1]
                rw = src_t[h:h + 1, :]
                decay = jnp.where(tri, jnp.exp2(col - rw), 0.0)
                lhs_y.append((cbm * decay).astype(BF16))
                lhs_s.append((bg_t * w_t[h:h + 1, :]).astype(BF16))
                keep = lo_half if u == 0 else jnp.logical_not(lo_half)
                rhs_x.append(jnp.where(keep, xs_pair, 0.0).astype(BF16))
                rhs_st.append(jnp.where(keep, st_pair, 0.0).astype(BF16))
            for u in range(2):
                lhs_y.append((e_acs[:, h0 + u:h0 + u + 1] * cg).astype(BF16))
            y_pair = jnp.dot(jnp.concatenate(lhs_y, axis=1), jnp.concatenate(rhs_x + rhs_st, axis=0),
                             preferred_element_type=F32)
            s_new = jnp.dot(jnp.concatenate(lhs_s, axis=1), jnp.concatenate(rhs_x, axis=0),
                            preferred_element_type=F32)
            dpair = jnp.where(lo_half, dlast[h0:h0 + 1, :], dlast[h0 + 1:h0 + 2, :])
            st[pair] = st_pair * dpair + s_new
            ybuf[:, h0 * SSD_HEAD_DIM:h0 * SSD_HEAD_DIM + LANES] = y_pair

    gw = d_inner // SSD_GROUPS
    for g in range(SSD_GROUPS):
        sl = slice(g * gw, (g + 1) * gw)
        y = ybuf[:, sl] + xc[:, sl] * dexp_ref[:, sl]
        zz = z_ref[:, sl]
        y = y * (zz * _sigmoid(zz))
        y = y * lax.rsqrt(jnp.mean(y * y, -1, keepdims=True) + LN_EPS)
        o_ref[:, sl] = (y * ng_ref[:, sl]).astype(BF16)


def _ssd(z, xbc, dtr, conv_w, conv_b, dt_bias, a_log, d_skip, norm_g, nb, lp):
    t, d_inner = z.shape
    xbc_w = xbc.shape[1]
    q = SSD_CHUNK
    n_heads = d_inner // SSD_HEAD_DIM
    n_state = (xbc_w - d_inner) // (2 * SSD_GROUPS)
    assert n_state == LANES and n_heads % (2 * SSD_GROUPS) == 0 and lp % q == 0
    assert xbc_w % 512 == 0 and PAD_ROWS % 8 == 0
    nc = lp // q
    padh = LANES - n_heads
    row = lambda w: pl.BlockSpec((q, w), lambda b, c: (b * nc + c, 0))
    full = lambda a: pl.BlockSpec(a.shape, lambda b, c: (0,) * a.ndim)
    dtb = jnp.pad(dt_bias, (0, padh)).reshape(1, LANES)
    alog = jnp.pad(a_log, (0, padh)).reshape(1, LANES)
    dexp = jnp.repeat(d_skip, SSD_HEAD_DIM).reshape(1, d_inner)
    cb = conv_b.reshape(1, xbc_w)
    ng = norm_g.reshape(1, d_inner)
    return pl.pallas_call(
        functools.partial(_ssd_kernel, q=q, d_inner=d_inner, n_state=n_state),
        grid=(nb, nc),
        in_specs=[row(d_inner), row(xbc_w), row(LANES), full(conv_w), full(cb), full(dtb), full(alog),
                  full(dexp), full(ng)],
        out_specs=row(d_inner),
        out_shape=jax.ShapeDtypeStruct((t, d_inner), BF16),
        scratch_shapes=[pltpu.VMEM((8 + q, xbc_w), F32), pltpu.VMEM((q, xbc_w), F32),
                        pltpu.VMEM((q, d_inner), F32), pltpu.VMEM((n_heads // 2, n_state, LANES), F32)],
        compiler_params=_cparams(("arbitrary", "arbitrary"), 32),
        name="ssd",
    )(z, xbc, dtr, conv_w, cb, dtb, alog, dexp, ng)


def _pool_kernel(u_ref, w_ref, sc_ref, o_ref, buf, *, r, gd):
    c = pl.program_id(1)
    hist = 16
    width = buf.shape[1]

    @pl.when(c == 0)
    def _():
        buf[0:hist, :] = jnp.zeros((hist, width), F32)

    buf[hist:hist + r, :] = u_ref[...]
    pos = c * r + lax.broadcasted_iota(jnp.int32, (r, 1), 0) - PAD_ROWS
    for gi, win in enumerate(POOL_WINDOWS):
        sl = slice(gi * gd, (gi + 1) * gd)
        u = buf[hist:hist + r, sl]
        s = u
        for j in range(1, win):
            s = s + buf[hist - j:hist - j + r, sl]
        cnt = jnp.clip(pos + 1, 1, win).astype(F32)
        mixed = s / cnt - u
        y = jnp.dot(mixed.astype(BF16), w_ref[gi].astype(BF16), preferred_element_type=F32)
        o_ref[:, sl] = (y * sc_ref[:, sl]).astype(BF16)
    buf[0:hist, :] = buf[r:r + hist, :]


def _pool(pc, pool_w, scale, nb, lp):
    t = pc.shape[0]
    ng, gd, _ = pool_w.shape
    width = ng * gd
    assert ng == len(POOL_WINDOWS) and gd % LANES == 0
    r = _pick(lp, (384, 192, 128, 64))
    nc = lp // r
    return pl.pallas_call(
        functools.partial(_pool_kernel, r=r, gd=gd),
        grid=(nb, nc),
        in_specs=[pl.BlockSpec((r, width), lambda b, c: (b * nc + c, 0)),
                  pl.BlockSpec(pool_w.shape, lambda b, c: (0, 0, 0)),
                  pl.BlockSpec((1, width), lambda b, c: (0, 0))],
        out_specs=pl.BlockSpec((r, width), lambda b, c: (b * nc + c, 0)),
        out_shape=jax.ShapeDtypeStruct((t, width), BF16),
        scratch_shapes=[pltpu.VMEM((16 + r, width), F32)],
        compiler_params=_cparams(("arbitrary", "arbitrary"), 32),
        name="pool",
    )(pc, pool_w, scale.reshape(1, width))


def _conf_kernel(a_ref, g_ref, w_ref, b_ref, lg_ref, lb_ref, o_ref, vbuf, cv, ubuf, *, r):
    c = pl.program_id(1)
    hist = 32
    sub = 8
    width = vbuf.shape[1]
    kc = w_ref.shape[0]

    @pl.when(c == 0)
    def _():
        vbuf[0:hist, :] = jnp.zeros((hist, width), F32)

    vbuf[hist:hist + r, :] = a_ref[...] * _sigmoid(g_ref[...])
    ext = r + sub
    cw = ubuf.shape[2]
    for jc in range(width // cw):
        sl = slice(jc * cw, (jc + 1) * cw)
        acc = None
        for rr in range(sub):
            u = None
            for j in range((kc - 1 - rr) // sub + 1):
                k = kc - 1 - (sub * j + rr)
                lo = hist - sub - sub * j
                term = w_ref[k:k + 1, sl] * vbuf[lo:lo + ext, sl]
                u = term if u is None else u + term
            if rr == 0:
                acc = b_ref[:, sl] + u[sub:sub + r]
            else:
                ubuf[rr - 1] = u
                acc = acc + ubuf[rr - 1, sub - rr:sub - rr + r, :]
        cv[:, sl] = acc
    vbuf[0:hist, :] = vbuf[r:r + hist, :]
    v = _ln(cv[...], lg_ref[...], lb_ref[...])
    o_ref[...] = (v * _sigmoid(v)).astype(BF16)


def _conf(pc, col_blk, dw_w, dw_b, ln_g, ln_b, nb, lp):
    t = pc.shape[0]
    kc, width = dw_w.shape
    assert kc <= 33 and width % 256 == 0
    r = _pick(lp, (192, 128, 64))
    nc = lp // r
    vec = lambda: pl.BlockSpec((1, width), lambda b, c: (0, 0))
    return pl.pallas_call(
        functools.partial(_conf_kernel, r=r),
        grid=(nb, nc),
        in_specs=[pl.BlockSpec((r, width), lambda b, c: (b * nc + c, col_blk)),
                  pl.BlockSpec((r, width), lambda b, c: (b * nc + c, col_blk + 1)),
                  pl.BlockSpec((kc, width), lambda b, c: (0, 0)), vec(), vec(), vec()],
        out_specs=pl.BlockSpec((r, width), lambda b, c: (b * nc + c, 0)),
        out_shape=jax.ShapeDtypeStruct((t, width), BF16),
        scratch_shapes=[pltpu.VMEM((32 + r, width), F32), pltpu.VMEM((r, width), F32),
                        pltpu.VMEM((7, r + 8, LANES), F32)],
        compiler_params=_cparams(("arbitrary", "arbitrary"), 32),
        name="conf",
    )(pc, pc, dw_w, dw_b.reshape(1, width), ln_g.reshape(1, width), ln_b.reshape(1, width))


def _merge_kernel(hb_ref, f0_ref, f1_ref, f2_ref, g0_ref, g1_ref, g2_ref, gb_ref, p0_ref, p1_ref, p2_ref,
                  o_ref, gs, s0, s1, s2):
    @pl.when(pl.program_id(1) == 0)
    def _():
        gs[0] = g0_ref[...].astype(BF16)
        gs[1] = g1_ref[...].astype(BF16)
        gs[2] = g2_ref[...].astype(BF16)
        s0[...] = p0_ref[...].astype(BF16)
        s1[...] = p1_ref[...].astype(BF16)
        s2[...] = p2_ref[...].astype(BF16)

    hb = hb_ref[...]
    acc = None
    for j, (f_ref, s_ref) in enumerate(((f0_ref, s0), (f1_ref, s1), (f2_ref, s2))):
        gate = _sigmoid(jnp.dot(hb, gs[j], preferred_element_type=F32) + gb_ref[j:j + 1, :])
        term = gate * jnp.dot(f_ref[...], s_ref[...], preferred_element_type=F32)
        acc = term if acc is None else acc + term
    o_ref[...] = acc.astype(BF16)


def _merge(hb, f_ssd, f_pool, f_conf, gate_w, gate_b, ssd_proj, pool_proj, conf_proj, layer):
    t, d = hb.shape
    bn = _pick(d, (512, 256, 128))
    bm = _pick(t, (528, 384, 192, 128, 64, 8))
    nbn = d // bn
    once = pl.Buffered(1)
    rowspec = lambda a: pl.BlockSpec((bm, a.shape[1]), lambda n, m: (m, 0))
    gspec = lambda j: pl.BlockSpec((None, d, bn), lambda n, m: (layer, 0, j * nbn + n), pipeline_mode=once)
    pspec = lambda w: pl.BlockSpec((None, w.shape[1], bn), lambda n, m: (layer, 0, n), pipeline_mode=once)
    gb = gate_b[layer].reshape(3, d)
    return pl.pallas_call(
        _merge_kernel,
        grid=(nbn, t // bm),
        in_specs=[rowspec(hb), rowspec(f_ssd), rowspec(f_pool), rowspec(f_conf),
                  gspec(0), gspec(1), gspec(2), pl.BlockSpec((3, bn), lambda n, m: (0, n)),
                  pspec(ssd_proj), pspec(pool_proj), pspec(conf_proj)],
        out_specs=pl.BlockSpec((bm, bn), lambda n, m: (m, n)),
        out_shape=jax.ShapeDtypeStruct((t, d), BF16),
        scratch_shapes=[pltpu.VMEM((3, d, bn), BF16), pltpu.VMEM((ssd_proj.shape[1], bn), BF16),
                        pltpu.VMEM((pool_proj.shape[1], bn), BF16), pltpu.VMEM((conf_proj.shape[1], bn), BF16)],
        compiler_params=_cparams(("arbitrary", "arbitrary"), 52),
        name="merge",
    )(hb, f_ssd, f_pool, f_conf, gate_w, gate_w, gate_w, gb, ssd_proj, pool_proj, conf_proj)


def _router_kernel(h_ref, w_ref, b_ref, o_ref, cnt_ref, run, *, bm, lp, n_groups, per_group):
    @pl.when(pl.program_id(0) == 0)
    def _():
        run[...] = jnp.zeros(run.shape, F32)

    h = h_ref[...]
    w = w_ref[...]
    h_hi = h.astype(BF16)
    h_lo = (h - h_hi.astype(F32)).astype(BF16)
    w_hi = w.astype(BF16)
    w_lo = (w - w_hi.astype(F32)).astype(BF16)
    logits = (jnp.dot(h_hi, w_hi, preferred_element_type=F32)
              + (jnp.dot(h_hi, w_lo, preferred_element_type=F32)
                 + jnp.dot(h_lo, w_hi, preferred_element_type=F32))) + b_ref[...]
    lane = lax.broadcasted_iota(jnp.int32, logits.shape, 1).astype(F32)

    def first_max(vals):
        m = jnp.max(vals, axis=1, keepdims=True)
        idx = jnp.min(jnp.where(vals == m, lane, float(LANES)), axis=1, keepdims=True)
        return m, idx

    gmask = lane < n_groups
    gmax, gsel = first_max(jnp.where(gmask, logits, NEG_BIG))
    gsum = jnp.sum(jnp.where(gmask, jnp.exp(logits - gmax), 0.0), axis=1, keepdims=True)
    p_group = 1.0 / gsum
    lo = n_groups + per_group * gsel
    el = jnp.where(jnp.logical_and(lane >= lo, lane < lo + per_group), logits, NEG_BIG)
    m1, i1 = first_max(el)
    m2, i2 = first_max(jnp.where(lane == i1, NEG_BIG, el))
    ratio = jnp.exp(m2 - m1)
    w1 = p_group / (1.0 + ratio)
    w2 = w1 * ratio
    real = _rows_in_batch(pl.program_id(0), bm, lp) >= PAD_ROWS
    sentinel = float(n_groups * per_group)
    e1 = jnp.where(real, i1 - n_groups, sentinel)
    e2 = jnp.where(real, i2 - n_groups, sentinel)
    w1 = jnp.where(real, w1, 0.0)
    w2 = jnp.where(real, w2, 0.0)
    hit1 = jnp.logical_and(lane == i1, real)
    hit2 = jnp.logical_and(lane == i2, real)
    onehot = jnp.where(jnp.logical_or(hit1, hit2), 1.0, 0.0)
    li = lax.broadcasted_iota(jnp.int32, (bm, bm), 0)
    si = lax.broadcasted_iota(jnp.int32, (bm, bm), 1)
    before = jnp.where(li > si, 1.0, 0.0).astype(BF16)
    seen = jnp.dot(before, onehot.astype(BF16), preferred_element_type=F32) + run[...]
    r1 = jnp.sum(jnp.where(hit1, seen, 0.0), axis=1, keepdims=True)
    r2 = jnp.sum(jnp.where(hit2, seen, 0.0), axis=1, keepdims=True)
    run[...] = run[...] + jnp.sum(onehot, axis=0, keepdims=True)
    cnt_ref[...] = run[...]
    out = jnp.zeros(logits.shape, F32)
    for k, v in enumerate((w1, w2, e1, e2, r1, r2)):
        out = jnp.where(lane == k, v, out)
    o_ref[...] = out


def _router(h, rw, rb, lp, n_groups, per_group):
    t, d = h.shape
    bm = _pick(lp, (528, 384, 192, 128, 64, 8))
    return pl.pallas_call(
        functools.partial(_router_kernel, bm=bm, lp=lp, n_groups=n_groups, per_group=per_group),
        grid=(t // bm,),
        in_specs=[pl.BlockSpec((bm, d), lambda i: (i, 0)), pl.BlockSpec((d, LANES), lambda i: (0, 0)),
                  pl.BlockSpec((1, LANES), lambda i: (0, 0))],
        out_specs=[pl.BlockSpec((bm, LANES), lambda i: (i, 0)), pl.BlockSpec((1, LANES), lambda i: (0, 0))],
        out_shape=[jax.ShapeDtypeStruct((t, LANES), F32), jax.ShapeDtypeStruct((1, LANES), F32)],
        scratch_shapes=[pltpu.VMEM((1, LANES), F32)],
        compiler_params=_cparams(("arbitrary",), 40),
        name="router",
    )(h, rw, rb)


def _moe_kernel(be_ref, nxt_ref, tot_ref, x_ref, wg_hbm, wu_hbm, wd_hbm, o_ref,
                wg_f, wu_f, wd_f, wg_s, wu_s, wd_s, sems, *, layer):
    i = pl.program_id(0)
    total = tot_ref[0]
    ic = jnp.maximum(jnp.minimum(i, total - 1), 0)
    e = be_ref[ic]
    e_prev = be_ref[jnp.maximum(ic - 1, 0)]
    active = i < total
    first_of_run = jnp.logical_and(active, jnp.logical_or(i == 0, e != e_prev))

    def weight_copies(expert):
        return (pltpu.make_async_copy(wg_hbm.at[layer, expert], wg_f, sems.at[0]),
                pltpu.make_async_copy(wu_hbm.at[layer, expert], wu_f, sems.at[1]),
                pltpu.make_async_copy(wd_hbm.at[layer, expert], wd_f, sems.at[2]))

    @pl.when(jnp.logical_and(active, i == 0))
    def _():
        for cp in weight_copies(e):
            cp.start()

    @pl.when(first_of_run)
    def _():
        for cp, src, dst in zip(weight_copies(e), (wg_f, wu_f, wd_f), (wg_s, wu_s, wd_s)):
            cp.wait()
            dst[...] = src[...].astype(BF16)
        nxt = nxt_ref[ic]

        @pl.when(nxt >= 0)
        def _():
            for cp in weight_copies(nxt):
                cp.start()

    @pl.when(active)
    def _():
        x = _unpack_halves(x_ref[...]).astype(BF16)
        gte = jnp.dot(x, wg_s[...], preferred_element_type=F32)
        up = jnp.dot(x, wu_s[...], preferred_element_type=F32)
        hid = (gte * _sigmoid(gte)) * up
        o_ref[...] = _pack_halves(jnp.dot(hid.astype(BF16), wd_s[...], preferred_element_type=F32))

    @pl.when(jnp.logical_not(active))
    def _():
        o_ref[...] = jnp.zeros(o_ref.shape, F32)


def _moe_ffn(xg, block_expert, next_expert, total_blocks, w_gate, w_up, w_down, layer):
    n_slots, d = xg.shape
    hid = w_gate.shape[-1]
    assert w_gate.shape[-2] == 2 * d
    n_blocks = n_slots // MOE_BLOCK

    def rows(i, be, nxt, tot):
        return (jnp.maximum(jnp.minimum(i, tot[0] - 1), 0), 0)

    hbm = pl.BlockSpec(memory_space=pl.ANY)
    grid_spec = pltpu.PrefetchScalarGridSpec(
        num_scalar_prefetch=3,
        grid=(n_blocks,),
        in_specs=[pl.BlockSpec((MOE_BLOCK, d), rows), hbm, hbm, hbm],
        out_specs=pl.BlockSpec((MOE_BLOCK, d), lambda i, be, nxt, tot: (i, 0)),
        scratch_shapes=[pltpu.VMEM((2 * d, hid), F32), pltpu.VMEM((2 * d, hid), F32), pltpu.VMEM((hid, 2 * d), F32),
                        pltpu.VMEM((2 * d, hid), BF16), pltpu.VMEM((2 * d, hid), BF16),
                        pltpu.VMEM((hid, 2 * d), BF16), pltpu.SemaphoreType.DMA((3,))],
    )
    return pl.pallas_call(
        functools.partial(_moe_kernel, layer=layer),
        grid_spec=grid_spec,
        out_shape=jax.ShapeDtypeStruct((n_slots, d), F32),
        compiler_params=_cparams(("arbitrary",), 48),
        name="moe_ffn",
    )(block_expert, next_expert, total_blocks, xg, w_gate, w_up, w_down)


def _dispatch(ew, cnt, n_groups, n_experts):
    t = ew.shape[0]
    eid = ew[:, 2:4].astype(jnp.int32).reshape(-1)
    rank = ew[:, 4:6].astype(jnp.int32).reshape(-1)
    counts = cnt[0, n_groups:n_groups + n_experts].astype(jnp.int32)
    n_assign = eid.shape[0]
    tok = jnp.repeat(jnp.arange(t, dtype=jnp.int32), 2)
    padded = (counts + MOE_BLOCK - 1) // MOE_BLOCK * MOE_BLOCK
    pend = jnp.cumsum(padded)
    pstart = pend - padded
    n_blocks = n_assign // MOE_BLOCK + n_experts
    n_slots = n_blocks * MOE_BLOCK
    valid = eid < n_experts
    slot = jnp.where(valid, pstart[jnp.minimum(eid, n_experts - 1)] + rank, n_slots)
    slot_tok = (jnp.arange(n_slots, dtype=jnp.int32) % t).at[slot].set(tok, mode="drop")
    block_expert = jnp.minimum(
        jnp.searchsorted(pend, jnp.arange(n_blocks, dtype=jnp.int32) * MOE_BLOCK, side="right"),
        n_experts - 1).astype(jnp.int32)
    total_blocks = (pend[-1] // MOE_BLOCK).astype(jnp.int32).reshape(1)
    ids = jnp.arange(n_experts, dtype=jnp.int32)
    later = jnp.where(counts > 0, ids, n_experts)
    nxt = lax.cummin(jnp.concatenate([later[1:], jnp.full((1,), n_experts, jnp.int32)]), reverse=True)
    nxt = jnp.where(nxt >= n_experts, -1, nxt)
    next_expert = nxt[block_expert]
    slot_c = jnp.where(valid, slot, 0).reshape(t, 2)
    return slot_tok, block_expert, next_expert, total_blocks, slot_c


def kernel(x, meta_tokens, ln_emb_g, ln_emb_b, w_in, ssd_conv_w, ssd_conv_b, ssd_dt_bias, ssd_a_log, ssd_d,
           ssd_norm_g, ssd_proj, pool_w, pool_scale, pool_proj, conf_dw_w, conf_dw_b, conf_ln_g, conf_ln_b,
           conf_proj, gate_w, gate_b, w_out, ln1_g, ln1_b, router_group_w, router_group_b, router_expert_w,
           router_expert_b, exp_w_gate, exp_w_up, exp_w_down, ln2_g, ln2_b):
    nb, seq, d = x.shape
    n_meta = meta_tokens.shape[0]
    depth = w_in.shape[0]
    lp = PAD_ROWS + n_meta + seq
    t = nb * lp
    alpha = (2.0 * depth) ** 0.25
    d_inner = ssd_norm_g.shape[-1]
    xbc_w = ssd_conv_w.shape[-1]
    n_heads = ssd_a_log.shape[-1]
    pool_width = pool_scale.shape[-1]
    conf_width = conf_dw_b.shape[-1]
    n_groups = router_group_w.shape[-1]
    n_experts = router_expert_w.shape[-1]
    per_group = n_experts // n_groups
    assert (PAD_ROWS + n_meta) % SSD_CHUNK == 0 and lp % SSD_CHUNK == 0
    assert n_groups + n_experts <= LANES and n_heads <= LANES

    meta = jnp.broadcast_to(meta_tokens[None].astype(x.dtype), (nb, n_meta, d))
    xcat = jnp.concatenate([jnp.zeros((nb, PAD_ROWS, d), x.dtype), meta, x], axis=1).reshape(t, d)
    h, hb = _embed_ln(xcat, ln_emb_g, ln_emb_b, lp)

    c_dt = d_inner + xbc_w
    c_pc = c_dt + n_heads
    assert c_dt % LANES == 0 and c_dt + LANES <= w_in.shape[-1]
    w_in_t = jnp.swapaxes(w_in, 1, 2)
    rw = jnp.concatenate([router_group_w, router_expert_w], axis=-1)
    rw = jnp.pad(rw, ((0, 0), (0, 0), (0, LANES - rw.shape[-1])))
    rb = jnp.concatenate([router_group_b, router_expert_b], axis=-1)
    rb = jnp.pad(rb, ((0, 0), (0, LANES - rb.shape[-1]))).reshape(depth, 1, LANES)

    for i in range(depth):
        z = _mm(hb, w_in_t, i, 0, d_inner)
        xbc = _mm(hb, w_in_t, i, d_inner, xbc_w)
        dtr = _mm(hb, w_in_t, i, c_dt, LANES)
        pc = _mm(hb, w_in_t, i, c_pc, pool_width + 2 * conf_width)
        f_ssd = _ssd(z, xbc, dtr, ssd_conv_w[i], ssd_conv_b[i], ssd_dt_bias[i], ssd_a_log[i], ssd_d[i],
                     ssd_norm_g[i], nb, lp)
        f_pool = _pool(pc, pool_w[i], pool_scale[i], nb, lp)
        assert pool_width % conf_width == 0
        f_conf = _conf(pc, pool_width // conf_width, conf_dw_w[i], conf_dw_b[i], conf_ln_g[i], conf_ln_b[i], nb, lp)
        merged = _merge(hb, f_ssd, f_pool, f_conf, gate_w, gate_b, ssd_proj, pool_proj, conf_proj, i)
        h, h_packed = _mm_ln(merged, w_out, i, h, ln1_g[i], ln1_b[i], lp, alpha)

        ew, cnt = _router(h, rw[i], rb[i], lp, n_groups, per_group)
        slot_tok, block_expert, next_expert, total_blocks, slot_c = _dispatch(ew, cnt, n_groups, n_experts)
        xg = h_packed.at[slot_tok].get(mode="promise_in_bounds")
        out = _moe_ffn(xg, block_expert, next_expert, total_blocks, exp_w_gate, exp_w_up, exp_w_down, i)
        g0 = out.at[slot_c[:, 0]].get(mode="promise_in_bounds")
        g1 = out.at[slot_c[:, 1]].get(mode="promise_in_bounds")
        if i + 1 < depth:
            h, hb = _combine_ln(h, g0, g1, ew, ln2_g[i], ln2_b[i], lp, alpha)
        else:
            h = _final_ln(h, g0, g1, ew, ln2_g[i], ln2_b[i], nb, lp, seq, alpha)

    return h.reshape(nb, seq, d)
```

```python
import functools

import jax
import jax.numpy as jnp
from jax import lax
from jax.experimental import pallas as pl
from jax.experimental.pallas import tpu as pltpu

F32 = jnp.float32
BF16 = jnp.bfloat16

PAD_ROWS = 112
SSD_CHUNK = 128
SSD_GROUPS = 4
SSD_HEAD_DIM = 64
POOL_WINDOWS = (2, 4, 8, 16)
MOE_BLOCK = 256
LN_EPS = 1e-5
LANES = 128
NEG_BIG = -1e30
LOG2E = 1.4426950408889634


def _cparams(sem, vmem_mb):
    return pltpu.CompilerParams(dimension_semantics=sem, vmem_limit_bytes=vmem_mb << 20)


def _sigmoid(x):
    return 1.0 / (1.0 + jnp.exp(-x))


def _softplus(x):
    return jnp.maximum(x, 0.0) + jnp.log1p(jnp.exp(-jnp.abs(x)))


def _ln(x, g, b):
    mu = jnp.mean(x, -1, keepdims=True)
    xc = x - mu
    var = jnp.mean(xc * xc, -1, keepdims=True)
    return xc * lax.rsqrt(var + LN_EPS) * g + b


def _pack_halves(y):
    n = y.shape[1] // 2
    lo = lax.bitcast_convert_type(y[:, :n].astype(BF16).astype(F32), jnp.uint32)
    hi = lax.bitcast_convert_type(y[:, n:].astype(BF16).astype(F32), jnp.uint32)
    return lax.bitcast_convert_type(hi | (lo >> 16), F32)


def _unpack_halves(p):
    u = lax.bitcast_convert_type(p, jnp.uint32)
    lo = lax.bitcast_convert_type(u << 16, F32)
    hi = lax.bitcast_convert_type(u & jnp.uint32(0xFFFF0000), F32)
    return jnp.concatenate([lo, hi], axis=1)


def _rows_in_batch(blk, bm, lp):
    base = lax.rem(blk, lp // bm) * bm
    return base + lax.broadcasted_iota(jnp.int32, (bm, 1), 0)


def _pick(n, cands):
    for c in cands:
        if n % c == 0:
            return c
    raise ValueError(f"no block size for {n} among {cands}")


def _embed_kernel(x_ref, g_ref, b_ref, h_ref, hb_ref, *, bm, lp):
    y = _ln(x_ref[...], g_ref[...], b_ref[...])
    y = jnp.where(_rows_in_batch(pl.program_id(0), bm, lp) >= PAD_ROWS, y, 0.0)
    h_ref[...] = y
    hb_ref[...] = y.astype(BF16)


def _embed_ln(xcat, g, b, lp):
    t, d = xcat.shape
    bm = _pick(lp, (528, 384, 192, 128, 64, 8))
    row = pl.BlockSpec((bm, d), lambda i: (i, 0))
    vec = pl.BlockSpec((1, d), lambda i: (0, 0))
    return pl.pallas_call(
        functools.partial(_embed_kernel, bm=bm, lp=lp),
        grid=(t // bm,),
        in_specs=[row, vec, vec],
        out_specs=[row, row],
        out_shape=[jax.ShapeDtypeStruct((t, d), F32), jax.ShapeDtypeStruct((t, d), BF16)],
        compiler_params=_cparams(("parallel",), 40),
        name="embed_ln",
    )(xcat, g.reshape(1, d), b.reshape(1, d))


def _combine_ln_kernel(h_ref, g0_ref, g1_ref, ew_ref, g_ref, b_ref, o_ref, ob_ref, *, bm, lp, alpha):
    y = ew_ref[:, 0:1] * _unpack_halves(g0_ref[...]) + ew_ref[:, 1:2] * _unpack_halves(g1_ref[...])
    y = _ln(alpha * h_ref[...] + y, g_ref[...], b_ref[...])
    y = jnp.where(_rows_in_batch(pl.program_id(0), bm, lp) >= PAD_ROWS, y, 0.0)
    o_ref[...] = y
    ob_ref[...] = y.astype(BF16)


def _combine_ln(h, g0, g1, ew, g, b, lp, alpha):
    t, d = h.shape
    bm = _pick(lp, (528, 384, 192, 128, 64, 8))
    row = pl.BlockSpec((bm, d), lambda i: (i, 0))
    half = pl.BlockSpec((bm, d // 2), lambda i: (i, 0))
    vec = pl.BlockSpec((1, d), lambda i: (0, 0))
    return pl.pallas_call(
        functools.partial(_combine_ln_kernel, bm=bm, lp=lp, alpha=alpha),
        grid=(t // bm,),
        in_specs=[row, half, half, pl.BlockSpec((bm, LANES), lambda i: (i, 0)), vec, vec],
        out_specs=[row, row],
        out_shape=[jax.ShapeDtypeStruct((t, d), F32), jax.ShapeDtypeStruct((t, d), BF16)],
        compiler_params=_cparams(("parallel",), 56),
        name="combine_ln",
    )(h, g0, g1, ew, g.reshape(1, d), b.reshape(1, d))


def _final_ln_kernel(h_ref, g0_ref, g1_ref, ew_ref, g_ref, b_ref, o_ref, *, alpha):
    y = ew_ref[:, 0:1] * _unpack_halves(g0_ref[...]) + ew_ref[:, 1:2] * _unpack_halves(g1_ref[...])
    o_ref[...] = _ln(alpha * h_ref[...] + y, g_ref[...], b_ref[...])


def _final_ln(h, g0, g1, ew, g, b, nb, lp, seq, alpha):
    d = h.shape[1]
    head = lp - seq
    bm = _pick(head, (128, 64, 8))
    assert seq % bm == 0
    per_seq, per_lp, skip = seq // bm, lp // bm, head // bm
    src = lambda i: ((i // per_seq) * per_lp + skip + i % per_seq, 0)
    row = pl.BlockSpec((bm, d), src)
    half = pl.BlockSpec((bm, d // 2), src)
    vec = pl.BlockSpec((1, d), lambda i: (0, 0))
    return pl.pallas_call(
        functools.partial(_final_ln_kernel, alpha=alpha),
        grid=(nb * per_seq,),
        in_specs=[row, half, half, pl.BlockSpec((bm, LANES), src), vec, vec],
        out_specs=pl.BlockSpec((bm, d), lambda i: (i, 0)),
        out_shape=jax.ShapeDtypeStruct((nb * seq, d), F32),
        compiler_params=_cparams(("parallel",), 32),
        name="final_ln",
    )(h, g0, g1, ew, g.reshape(1, d), b.reshape(1, d))


def _mm_kernel(x_ref, w_ref, *rest, shift):
    if shift:
        wnext_ref, o_ref, wbf_ref = rest
    else:
        o_ref, wbf_ref = rest
    bn = w_ref.shape[0]
    step = min(bn, 256)

    @pl.when(pl.program_id(1) == 0)
    def _():
        for c in range(bn // step):
            lo, hi = shift + c * step, shift + (c + 1) * step
            if hi <= bn:
                rows = w_ref[lo:hi, :]
            else:
                rows = jnp.concatenate([w_ref[lo:bn, :], wnext_ref[0:hi - bn, :]], axis=0)
            wbf_ref[:, c * step:(c + 1) * step] = rows.T.astype(BF16)

    o_ref[...] = jnp.dot(x_ref[...], wbf_ref[...], preferred_element_type=F32).astype(o_ref.dtype)


def _mm(x, w_t, layer, col0, n_cols, out_dtype=F32):
    t, k = x.shape
    bn = _pick(n_cols, (1024, 512, 256, 128))
    bm = _pick(t, (1056, 528, 384, 192, 128, 64, 8))
    shift = col0 % LANES
    base = col0 - shift
    assert base % bn == 0 and shift % 8 == 0
    cb0 = base // bn
    in_specs = [pl.BlockSpec((bm, k), lambda n, m: (m, 0)),
                pl.BlockSpec((None, bn, k), lambda n, m: (layer, cb0 + n, 0))]
    args = [x, w_t]
    if shift:
        tiles = bn // LANES
        in_specs.append(pl.BlockSpec((None, LANES, k), lambda n, m: (layer, (cb0 + n + 1) * tiles, 0)))
        args.append(w_t)
    return pl.pallas_call(
        functools.partial(_mm_kernel, shift=shift),
        grid=(n_cols // bn, t // bm),
        in_specs=in_specs,
        out_specs=pl.BlockSpec((bm, bn), lambda n, m: (m, n)),
        out_shape=jax.ShapeDtypeStruct((t, n_cols), out_dtype),
        scratch_shapes=[pltpu.VMEM((k, bn), BF16)],
        compiler_params=_cparams(("arbitrary", "arbitrary"), 48),
        name="mm",
    )(*args)


def _mm_ln_kernel(x_ref, w_ref, h_ref, g_ref, b_ref, o_ref, op_ref, wbf_ref, *, bm, lp, alpha):
    @pl.when(pl.program_id(0) == 0)
    def _():
        wbf_ref[...] = w_ref[...].astype(BF16)

    y = jnp.dot(x_ref[...], wbf_ref[...], preferred_element_type=F32)
    y = _ln(alpha * h_ref[...] + y, g_ref[...], b_ref[...])
    y = jnp.where(_rows_in_batch(pl.program_id(0), bm, lp) >= PAD_ROWS, y, 0.0)
    o_ref[...] = y
    op_ref[...] = _pack_halves(y)


def _mm_ln(x, w, layer, h, g, b, lp, alpha):
    t, k = x.shape
    d = w.shape[-1]
    bm = _pick(lp, (528, 384, 192, 128, 64, 8))
    row = lambda width: pl.BlockSpec((bm, width), lambda i: (i, 0))
    vec = pl.BlockSpec((1, d), lambda i: (0, 0))
    return pl.pallas_call(
        functools.partial(_mm_ln_kernel, bm=bm, lp=lp, alpha=alpha),
        grid=(t // bm,),
        in_specs=[row(k), pl.BlockSpec((None, k, d), lambda i: (layer, 0, 0), pipeline_mode=pl.Buffered(1)),
                  row(d), vec, vec],
        out_specs=[row(d), row(d // 2)],
        out_shape=[jax.ShapeDtypeStruct((t, d), F32), jax.ShapeDtypeStruct((t, d // 2), F32)],
        scratch_shapes=[pltpu.VMEM((k, d), BF16)],
        compiler_params=_cparams(("arbitrary",), 56),
        name="mm_ln",
    )(x, w, h, g.reshape(1, d), b.reshape(1, d))


def _ssd_kernel(z_ref, xbc_ref, dtr_ref, cw_ref, cb_ref, dtb_ref, alog_ref, dexp_ref, ng_ref, o_ref,
                cbuf, xc, ybuf, st, *, q, d_inner, n_state):
    c = pl.program_id(1)
    xbc_w = cbuf.shape[1]
    kc = cw_ref.shape[0]
    hist = 8

    @pl.when(c == 0)
    def _():
        cbuf[0:hist, :] = jnp.zeros((hist, xbc_w), F32)
        st[...] = jnp.zeros(st.shape, F32)

    cbuf[hist:hist + q, :] = xbc_ref[...]
    cw = 128
    for j in range(xbc_w // cw):
        sl = slice(j * cw, (j + 1) * cw)
        conv = cb_ref[:, sl] + cw_ref[kc - 1:kc, sl] * cbuf[hist:hist + q, sl]
        for k in range(kc - 1):
            off = hist - (kc - 1) + k
            conv = conv + cw_ref[k:k + 1, sl] * cbuf[off:off + q, sl]
        xc[:, sl] = conv * _sigmoid(conv)
    cbuf[0:hist, :] = cbuf[q:q + hist, :]

    row = c * q + lax.broadcasted_iota(jnp.int32, (q, 1), 0)
    dt = _softplus(dtr_ref[...] + dtb_ref[...])
    dt = jnp.where(row >= PAD_ROWS, dt, 0.0)
    adt = dt * (-jnp.exp(alog_ref[...]))
    li = lax.broadcasted_iota(jnp.int32, (q, q), 0)
    si = lax.broadcasted_iota(jnp.int32, (q, q), 1)
    tri = li >= si
    acs = jnp.dot(tri.astype(F32), adt, preferred_element_type=F32,
                  precision=lax.Precision.HIGHEST) * LOG2E
    acs_t = acs.T
    dt_t = dt.T
    e_acs = jnp.exp2(acs)
    last = acs_t[:, q - 1:q]
    w_t = jnp.exp2(last - acs_t) * dt_t
    src_t = acs_t - jnp.log2(dt_t)
    dlast = jnp.exp2(last)
    lane = lax.broadcasted_iota(jnp.int32, (1, LANES), 1)
    lo_half = lane < SSD_HEAD_DIM

    heads_per_group = d_inner // SSD_HEAD_DIM // SSD_GROUPS
    pairs_per_group = heads_per_group // 2
    for g in range(SSD_GROUPS):
        b0 = d_inner + g * n_state
        c0 = d_inner + SSD_GROUPS * n_state + g * n_state
        bg = xc[:, b0:b0 + n_state]
        cg = xc[:, c0:c0 + n_state]
        cbm = lax.dot_general(cg.astype(BF16), bg.astype(BF16), (((1,), (1,)), ((), ())),
                              preferred_element_type=F32)
        bg_t = bg.T
        for j in range(pairs_per_group):
            pair = g * pairs_per_group + j
            h0 = 2 * pair
            xs_pair = xc[:, h0 * SSD_HEAD_DIM:h0 * SSD_HEAD_DIM + LANES]
            st_pair = st[pair]
            lhs_y, lhs_s, rhs_x, rhs_st = [], [], [], []
            for u in range(2):
                h = h0 + u
                col = acs[:, h:h + 1]
                rw = src_t[h:h + 1, :]
                decay = jnp.where(tri, jnp.exp2(col - rw), 0.0)
                lhs_y.append((cbm * decay).astype(BF16))
                lhs_s.append((bg_t * w_t[h:h + 1, :]).astype(BF16))
                keep = lo_half if u == 0 else jnp.logical_not(lo_half)
                rhs_x.append(jnp.where(keep, xs_pair, 0.0).astype(BF16))
                rhs_st.append(jnp.where(keep, st_pair, 0.0).astype(BF16))
            for u in range(2):
                lhs_y.append((e_acs[:, h0 + u:h0 + u + 1] * cg).astype(BF16))
            y_pair = jnp.dot(jnp.concatenate(lhs_y, axis=1), jnp.concatenate(rhs_x + rhs_st, axis=0),
                             preferred_element_type=F32)
            s_new = jnp.dot(jnp.concatenate(lhs_s, axis=1), jnp.concatenate(rhs_x, axis=0),
                            preferred_element_type=F32)
            dpair = jnp.where(lo_half, dlast[h0:h0 + 1, :], dlast[h0 + 1:h0 + 2, :])
            st[pair] = st_pair * dpair + s_new
            ybuf[:, h0 * SSD_HEAD_DIM:h0 * SSD_HEAD_DIM + LANES] = y_pair

    gw = d_inner // SSD_GROUPS
    for g in range(SSD_GROUPS):
        sl = slice(g * gw, (g + 1) * gw)
        y = ybuf[:, sl] + xc[:, sl] * dexp_ref[:, sl]
        zz = z_ref[:, sl]
        y = y * (zz * _sigmoid(zz))
        y = y * lax.rsqrt(jnp.mean(y * y, -1, keepdims=True) + LN_EPS)
        o_ref[:, sl] = (y * ng_ref[:, sl]).astype(BF16)


def _ssd(z, xbc, dtr, conv_w, conv_b, dt_bias, a_log, d_skip, norm_g, nb, lp):
    t, d_inner = z.shape
    xbc_w = xbc.shape[1]
    q = SSD_CHUNK
    n_heads = d_inner // SSD_HEAD_DIM
    n_state = (xbc_w - d_inner) // (2 * SSD_GROUPS)
    assert n_state == LANES and n_heads % (2 * SSD_GROUPS) == 0 and lp % q == 0
    assert xbc_w % 512 == 0 and PAD_ROWS % 8 == 0
    nc = lp // q
    padh = LANES - n_heads
    row = lambda w: pl.BlockSpec((q, w), lambda b, c: (b * nc + c, 0))
    full = lambda a: pl.BlockSpec(a.shape, lambda b, c: (0,) * a.ndim)
    dtb = jnp.pad(dt_bias, (0, padh)).reshape(1, LANES)
    alog = jnp.pad(a_log, (0, padh)).reshape(1, LANES)
    dexp = jnp.repeat(d_skip, SSD_HEAD_DIM).reshape(1, d_inner)
    cb = conv_b.reshape(1, xbc_w)
    ng = norm_g.reshape(1, d_inner)
    return pl.pallas_call(
        functools.partial(_ssd_kernel, q=q, d_inner=d_inner, n_state=n_state),
        grid=(nb, nc),
        in_specs=[row(d_inner), row(xbc_w), row(LANES), full(conv_w), full(cb), full(dtb), full(alog),
                  full(dexp), full(ng)],
        out_specs=row(d_inner),
        out_shape=jax.ShapeDtypeStruct((t, d_inner), BF16),
        scratch_shapes=[pltpu.VMEM((8 + q, xbc_w), F32), pltpu.VMEM((q, xbc_w), F32),
                        pltpu.VMEM((q, d_inner), F32), pltpu.VMEM((n_heads // 2, n_state, LANES), F32)],
        compiler_params=_cparams(("arbitrary", "arbitrary"), 32),
        name="ssd",
    )(z, xbc, dtr, conv_w, cb, dtb, alog, dexp, ng)


def _pool_kernel(u_ref, w_ref, sc_ref, o_ref, buf, *, r, gd):
    c = pl.program_id(1)
    hist = 16
    width = buf.shape[1]

    @pl.when(c == 0)
    def _():
        buf[0:hist, :] = jnp.zeros((hist, width), F32)

    buf[hist:hist + r, :] = u_ref[...]
    pos = c * r + lax.broadcasted_iota(jnp.int32, (r, 1), 0) - PAD_ROWS
    for gi, win in enumerate(POOL_WINDOWS):
        sl = slice(gi * gd, (gi + 1) * gd)
        u = buf[hist:hist + r, sl]
        s = u
        for j in range(1, win):
            s = s + buf[hist - j:hist - j + r, sl]
        cnt = jnp.clip(pos + 1, 1, win).astype(F32)
        mixed = s / cnt - u
        y = jnp.dot(mixed.astype(BF16), w_ref[gi].astype(BF16), preferred_element_type=F32)
        o_ref[:, sl] = (y * sc_ref[:, sl]).astype(BF16)
    buf[0:hist, :] = buf[r:r + hist, :]


def _pool(pc, pool_w, scale, nb, lp):
    t = pc.shape[0]
    ng, gd, _ = pool_w.shape
    width = ng * gd
    assert ng == len(POOL_WINDOWS) and gd % LANES == 0
    r = _pick(lp, (384, 192, 128, 64))
    nc = lp // r
    return pl.pallas_call(
        functools.partial(_pool_kernel, r=r, gd=gd),
        grid=(nb, nc),
        in_specs=[pl.BlockSpec((r, width), lambda b, c: (b * nc + c, 0)),
                  pl.BlockSpec(pool_w.shape, lambda b, c: (0, 0, 0)),
                  pl.BlockSpec((1, width), lambda b, c: (0, 0))],
        out_specs=pl.BlockSpec((r, width), lambda b, c: (b * nc + c, 0)),
        out_shape=jax.ShapeDtypeStruct((t, width), BF16),
        scratch_shapes=[pltpu.VMEM((16 + r, width), F32)],
        compiler_params=_cparams(("arbitrary", "arbitrary"), 32),
        name="pool",
    )(pc, pool_w, scale.reshape(1, width))


def _conf_kernel(a_ref, g_ref, w_ref, b_ref, lg_ref, lb_ref, o_ref, vbuf, cv, ubuf, *, r):
    c = pl.program_id(1)
    hist = 32
    sub = 8
    width = vbuf.shape[1]
    kc = w_ref.shape[0]

    @pl.when(c == 0)
    def _():
        vbuf[0:hist, :] = jnp.zeros((hist, width), F32)

    vbuf[hist:hist + r, :] = a_ref[...] * _sigmoid(g_ref[...])
    ext = r + sub
    cw = ubuf.shape[2]
    for jc in range(width // cw):
        sl = slice(jc * cw, (jc + 1) * cw)
        acc = None
        for rr in range(sub):
            u = None
            for j in range((kc - 1 - rr) // sub + 1):
                k = kc - 1 - (sub * j + rr)
                lo = hist - sub - sub * j
                term = w_ref[k:k + 1, sl] * vbuf[lo:lo + ext, sl]
                u = term if u is None else u + term
            if rr == 0:
                acc = b_ref[:, sl] + u[sub:sub + r]
            else:
                ubuf[rr - 1] = u
                acc = acc + ubuf[rr - 1, sub - rr:sub - rr + r, :]
        cv[:, sl] = acc
    vbuf[0:hist, :] = vbuf[r:r + hist, :]
    v = _ln(cv[...], lg_ref[...], lb_ref[...])
    o_ref[...] = (v * _sigmoid(v)).astype(BF16)


def _conf(pc, col_blk, dw_w, dw_b, ln_g, ln_b, nb, lp):
    t = pc.shape[0]
    kc, width = dw_w.shape
    assert kc <= 33 and width % 256 == 0
    r = _pick(lp, (192, 128, 64))
    nc = lp // r
    vec = lambda: pl.BlockSpec((1, width), lambda b, c: (0, 0))
    return pl.pallas_call(
        functools.partial(_conf_kernel, r=r),
        grid=(nb, nc),
        in_specs=[pl.BlockSpec((r, width), lambda b, c: (b * nc + c, col_blk)),
                  pl.BlockSpec((r, width), lambda b, c: (b * nc + c, col_blk + 1)),
                  pl.BlockSpec((kc, width), lambda b, c: (0, 0)), vec(), vec(), vec()],
        out_specs=pl.BlockSpec((r, width), lambda b, c: (b * nc + c, 0)),
        out_shape=jax.ShapeDtypeStruct((t, width), BF16),
        scratch_shapes=[pltpu.VMEM((32 + r, width), F32), pltpu.VMEM((r, width), F32),
                        pltpu.VMEM((7, r + 8, LANES), F32)],
        compiler_params=_cparams(("arbitrary", "arbitrary"), 32),
        name="conf",
    )(pc, pc, dw_w, dw_b.reshape(1, width), ln_g.reshape(1, width), ln_b.reshape(1, width))


def _merge_kernel(hb_ref, f0_ref, f1_ref, f2_ref, g0_ref, g1_ref, g2_ref, gb_ref, p0_ref, p1_ref, p2_ref,
                  o_ref, gs, s0, s1, s2):
    @pl.when(pl.program_id(1) == 0)
    def _():
        gs[0] = g0_ref[...].astype(BF16)
        gs[1] = g1_ref[...].astype(BF16)
        gs[2] = g2_ref[...].astype(BF16)
        s0[...] = p0_ref[...].astype(BF16)
        s1[...] = p1_ref[...].astype(BF16)
        s2[...] = p2_ref[...].astype(BF16)

    hb = hb_ref[...]
    acc = None
    for j, (f_ref, s_ref) in enumerate(((f0_ref, s0), (f1_ref, s1), (f2_ref, s2))):
        gate = _sigmoid(jnp.dot(hb, gs[j], preferred_element_type=F32) + gb_ref[j:j + 1, :])
        term = gate * jnp.dot(f_ref[...], s_ref[...], preferred_element_type=F32)
        acc = term if acc is None else acc + term
    o_ref[...] = acc.astype(BF16)


def _merge(hb, f_ssd, f_pool, f_conf, gate_w, gate_b, ssd_proj, pool_proj, conf_proj, layer):
    t, d = hb.shape
    bn = _pick(d, (512, 256, 128))
    bm = _pick(t, (528, 384, 192, 128, 64, 8))
    nbn = d // bn
    once = pl.Buffered(1)
    rowspec = lambda a: pl.BlockSpec((bm, a.shape[1]), lambda n, m: (m, 0))
    gspec = lambda j: pl.BlockSpec((None, d, bn), lambda n, m: (layer, 0, j * nbn + n), pipeline_mode=once)
    pspec = lambda w: pl.BlockSpec((None, w.shape[1], bn), lambda n, m: (layer, 0, n), pipeline_mode=once)
    gb = gate_b[layer].reshape(3, d)
    return pl.pallas_call(
        _merge_kernel,
        grid=(nbn, t // bm),
        in_specs=[rowspec(hb), rowspec(f_ssd), rowspec(f_pool), rowspec(f_conf),
                  gspec(0), gspec(1), gspec(2), pl.BlockSpec((3, bn), lambda n, m: (0, n)),
                  pspec(ssd_proj), pspec(pool_proj), pspec(conf_proj)],
        out_specs=pl.BlockSpec((bm, bn), lambda n, m: (m, n)),
        out_shape=jax.ShapeDtypeStruct((t, d), BF16),
        scratch_shapes=[pltpu.VMEM((3, d, bn), BF16), pltpu.VMEM((ssd_proj.shape[1], bn), BF16),
                        pltpu.VMEM((pool_proj.shape[1], bn), BF16), pltpu.VMEM((conf_proj.shape[1], bn), BF16)],
        compiler_params=_cparams(("arbitrary", "arbitrary"), 52),
        name="merge",
    )(hb, f_ssd, f_pool, f_conf, gate_w, gate_w, gate_w, gb, ssd_proj, pool_proj, conf_proj)


def _router_kernel(h_ref, w_ref, b_ref, o_ref, ot_ref, cnt_ref, run, *, bm, lp, n_groups, per_group):
    @pl.when(pl.program_id(0) == 0)
    def _():
        run[...] = jnp.zeros(run.shape, F32)

    h = h_ref[...]
    w = w_ref[...]
    h_hi = h.astype(BF16)
    h_lo = (h - h_hi.astype(F32)).astype(BF16)
    w_hi = w.astype(BF16)
    w_lo = (w - w_hi.astype(F32)).astype(BF16)
    logits = (jnp.dot(h_hi, w_hi, preferred_element_type=F32)
              + (jnp.dot(h_hi, w_lo, preferred_element_type=F32)
                 + jnp.dot(h_lo, w_hi, preferred_element_type=F32))) + b_ref[...]
    lane = lax.broadcasted_iota(jnp.int32, logits.shape, 1).astype(F32)

    def first_max(vals):
        m = jnp.max(vals, axis=1, keepdims=True)
        idx = jnp.min(jnp.where(vals == m, lane, float(LANES)), axis=1, keepdims=True)
        return m, idx

    gmask = lane < n_groups
    gmax, gsel = first_max(jnp.where(gmask, logits, NEG_BIG))
    gsum = jnp.sum(jnp.where(gmask, jnp.exp(logits - gmax), 0.0), axis=1, keepdims=True)
    p_group = 1.0 / gsum
    lo = n_groups + per_group * gsel
    el = jnp.where(jnp.logical_and(lane >= lo, lane < lo + per_group), logits, NEG_BIG)
    m1, i1 = first_max(el)
    m2, i2 = first_max(jnp.where(lane == i1, NEG_BIG, el))
    ratio = jnp.exp(m2 - m1)
    w1 = p_group / (1.0 + ratio)
    w2 = w1 * ratio
    real = _rows_in_batch(pl.program_id(0), bm, lp) >= PAD_ROWS
    sentinel = float(n_groups * per_group)
    e1 = jnp.where(real, i1 - n_groups, sentinel)
    e2 = jnp.where(real, i2 - n_groups, sentinel)
    w1 = jnp.where(real, w1, 0.0)
    w2 = jnp.where(real, w2, 0.0)
    hit1 = jnp.logical_and(lane == i1, real)
    hit2 = jnp.logical_and(lane == i2, real)
    onehot = jnp.where(jnp.logical_or(hit1, hit2), 1.0, 0.0)
    li = lax.broadcasted_iota(jnp.int32, (bm, bm), 0)
    si = lax.broadcasted_iota(jnp.int32, (bm, bm), 1)
    before = jnp.where(li > si, 1.0, 0.0).astype(BF16)
    seen = jnp.dot(before, onehot.astype(BF16), preferred_element_type=F32) + run[...]
    r1 = jnp.sum(jnp.where(hit1, seen, 0.0), axis=1, keepdims=True)
    r2 = jnp.sum(jnp.where(hit2, seen, 0.0), axis=1, keepdims=True)
    run[...] = run[...] + jnp.sum(onehot, axis=0, keepdims=True)
    cnt_ref[...] = run[...]
    out = jnp.zeros(logits.shape, F32)
    for k, v in enumerate((w1, w2, e1, e2, r1, r2)):
        out = jnp.where(lane == k, v, out)
    o_ref[...] = out
    ot_ref[...] = out.T[0:8, :]


def _router(h, rw, rb, lp, n_groups, per_group):
    t, d = h.shape
    bm = _pick(lp, (384, 128))
    return pl.pallas_call(
        functools.partial(_router_kernel, bm=bm, lp=lp, n_groups=n_groups, per_group=per_group),
        grid=(t // bm,),
        in_specs=[pl.BlockSpec((bm, d), lambda i: (i, 0)), pl.BlockSpec((d, LANES), lambda i: (0, 0)),
                  pl.BlockSpec((1, LANES), lambda i: (0, 0))],
        out_specs=[pl.BlockSpec((bm, LANES), lambda i: (i, 0)), pl.BlockSpec((8, bm), lambda i: (0, i)),
                   pl.BlockSpec((1, LANES), lambda i: (0, 0))],
        out_shape=[jax.ShapeDtypeStruct((t, LANES), F32), jax.ShapeDtypeStruct((8, t), F32),
                   jax.ShapeDtypeStruct((1, LANES), F32)],
        scratch_shapes=[pltpu.VMEM((1, LANES), F32)],
        compiler_params=_cparams(("arbitrary",), 40),
        name="router",
    )(h, rw, rb)


def _moe_kernel(be_ref, nxt1_ref, nxt2_ref, par_ref, tot_ref, x_ref, wg_hbm, wu_hbm, wd_hbm, o_ref,
                wg_f, wu_f, wd_f, wg_s, wu_s, wd_s, sems, *, layer):
    i = pl.program_id(0)
    total = tot_ref[0]
    ic = jnp.maximum(jnp.minimum(i, total - 1), 0)
    e = be_ref[ic]
    e_prev = be_ref[jnp.maximum(ic - 1, 0)]
    active = i < total
    first_of_run = jnp.logical_and(active, jnp.logical_or(i == 0, e != e_prev))
    par = par_ref[ic]

    def weight_copies(expert, st):
        return (pltpu.make_async_copy(wg_hbm.at[layer, expert], wg_f.at[st], sems.at[st, 0]),
                pltpu.make_async_copy(wu_hbm.at[layer, expert], wu_f.at[st], sems.at[st, 1]),
                pltpu.make_async_copy(wd_hbm.at[layer, expert], wd_f.at[st], sems.at[st, 2]))

    @pl.when(jnp.logical_and(active, i == 0))
    def _():
        for cp in weight_copies(e, 0):
            cp.start()
        nxt1 = nxt1_ref[ic]

        @pl.when(nxt1 >= 0)
        def _():
            for cp in weight_copies(nxt1, 1):
                cp.start()

    @pl.when(first_of_run)
    def _():
        for cp, src, dst in zip(weight_copies(e, par), (wg_f, wu_f, wd_f), (wg_s, wu_s, wd_s)):
            cp.wait()
            dst[...] = src[par].astype(BF16)
        nxt2 = nxt2_ref[ic]

        @pl.when(nxt2 >= 0)
        def _():
            for cp in weight_copies(nxt2, par):
                cp.start()

    @pl.when(active)
    def _():
        x = _unpack_halves(x_ref[...]).astype(BF16)
        gte = jnp.dot(x, wg_s[...], preferred_element_type=F32)
        up = jnp.dot(x, wu_s[...], preferred_element_type=F32)
        hid = (gte * _sigmoid(gte)) * up
        o_ref[...] = _pack_halves(jnp.dot(hid.astype(BF16), wd_s[...], preferred_element_type=F32))

    @pl.when(jnp.logical_not(active))
    def _():
        o_ref[...] = jnp.zeros(o_ref.shape, F32)


def _moe_ffn(xg, plan, w_gate, w_up, w_down, layer):
    n_slots, d = xg.shape
    hid = w_gate.shape[-1]
    assert w_gate.shape[-2] == 2 * d
    n_blocks = n_slots // MOE_BLOCK

    def rows(i, be, nxt1, nxt2, par, tot):
        return (jnp.maximum(jnp.minimum(i, tot[0] - 1), 0), 0)

    hbm = pl.BlockSpec(memory_space=pl.ANY)
    grid_spec = pltpu.PrefetchScalarGridSpec(
        num_scalar_prefetch=5,
        grid=(n_blocks,),
        in_specs=[pl.BlockSpec((MOE_BLOCK, d), rows), hbm, hbm, hbm],
        out_specs=pl.BlockSpec((MOE_BLOCK, d), lambda i, *_: (i, 0)),
        scratch_shapes=[pltpu.VMEM((2, 2 * d, hid), F32), pltpu.VMEM((2, 2 * d, hid), F32),
                        pltpu.VMEM((2, hid, 2 * d), F32),
                        pltpu.VMEM((2 * d, hid), BF16), pltpu.VMEM((2 * d, hid), BF16),
                        pltpu.VMEM((hid, 2 * d), BF16), pltpu.SemaphoreType.DMA((2, 3))],
    )
    return pl.pallas_call(
        functools.partial(_moe_kernel, layer=layer),
        grid_spec=grid_spec,
        out_shape=jax.ShapeDtypeStruct((n_slots, d), F32),
        compiler_params=_cparams(("arbitrary",), 58),
        name="moe_ffn",
    )(*plan, xg, w_gate, w_up, w_down)


def _dispatch(ew_t, cnt, n_groups, n_experts):
    t = ew_t.shape[1]
    eid = ew_t[2:4].astype(jnp.int32).reshape(-1)
    rank = ew_t[4:6].astype(jnp.int32).reshape(-1)
    counts = cnt[0, n_groups:n_groups + n_experts].astype(jnp.int32)
    n_assign = eid.shape[0]
    tok = jnp.tile(jnp.arange(t, dtype=jnp.int32), 2)
    padded = (counts + MOE_BLOCK - 1) // MOE_BLOCK * MOE_BLOCK
    pend = jnp.cumsum(padded)
    pstart = pend - padded
    n_blocks = n_assign // MOE_BLOCK + n_experts
    n_slots = n_blocks * MOE_BLOCK
    valid = eid < n_experts
    slot = jnp.where(valid, pstart[jnp.minimum(eid, n_experts - 1)] + rank, n_slots)
    slot_tok = (jnp.arange(n_slots, dtype=jnp.int32) % t).at[slot].set(tok, mode="drop")
    block_expert = jnp.minimum(
        jnp.searchsorted(pend, jnp.arange(n_blocks, dtype=jnp.int32) * MOE_BLOCK, side="right"),
        n_experts - 1).astype(jnp.int32)
    total_blocks = (pend[-1] // MOE_BLOCK).astype(jnp.int32).reshape(1)
    ids = jnp.arange(n_experts, dtype=jnp.int32)
    has = counts > 0
    later = jnp.where(has, ids, n_experts)
    nxt1 = lax.cummin(jnp.concatenate([later[1:], jnp.full((1,), n_experts, jnp.int32)]), reverse=True)
    nxt2 = jnp.concatenate([nxt1, jnp.full((1,), n_experts, jnp.int32)])[nxt1]
    nxt1 = jnp.where(nxt1 >= n_experts, -1, nxt1)
    nxt2 = jnp.where(nxt2 >= n_experts, -1, nxt2)
    parity = (jnp.cumsum(has.astype(jnp.int32)) - 1) % 2
    plan = (block_expert, nxt1[block_expert], nxt2[block_expert], parity[block_expert], total_blocks)
    slot_c = jnp.where(valid, slot, 0).reshape(2, t)
    return slot_tok, plan, slot_c


def kernel(x, meta_tokens, ln_emb_g, ln_emb_b, w_in, ssd_conv_w, ssd_conv_b, ssd_dt_bias, ssd_a_log, ssd_d,
           ssd_norm_g, ssd_proj, pool_w, pool_scale, pool_proj, conf_dw_w, conf_dw_b, conf_ln_g, conf_ln_b,
           conf_proj, gate_w, gate_b, w_out, ln1_g, ln1_b, router_group_w, router_group_b, router_expert_w,
           router_expert_b, exp_w_gate, exp_w_up, exp_w_down, ln2_g, ln2_b):
    nb, seq, d = x.shape
    n_meta = meta_tokens.shape[0]
    depth = w_in.shape[0]
    lp = PAD_ROWS + n_meta + seq
    t = nb * lp
    alpha = (2.0 * depth) ** 0.25
    d_inner = ssd_norm_g.shape[-1]
    xbc_w = ssd_conv_w.shape[-1]
    n_heads = ssd_a_log.shape[-1]
    pool_width = pool_scale.shape[-1]
    conf_width = conf_dw_b.shape[-1]
    n_groups = router_group_w.shape[-1]
    n_experts = router_expert_w.shape[-1]
    per_group = n_experts // n_groups
    assert (PAD_ROWS + n_meta) % SSD_CHUNK == 0 and lp % SSD_CHUNK == 0
    assert n_groups + n_experts <= LANES and n_heads <= LANES

    meta = jnp.broadcast_to(meta_tokens[None].astype(x.dtype), (nb, n_meta, d))
    xcat = jnp.concatenate([jnp.zeros((nb, PAD_ROWS, d), x.dtype), meta, x], axis=1).reshape(t, d)
    h, hb = _embed_ln(xcat, ln_emb_g, ln_emb_b, lp)

    c_dt = d_inner + xbc_w
    c_pc = c_dt + n_heads
    assert c_dt % LANES == 0 and c_dt + LANES <= w_in.shape[-1]
    w_in_t = jnp.swapaxes(w_in, 1, 2)
    rw = jnp.concatenate([router_group_w, router_expert_w], axis=-1)
    rw = jnp.pad(rw, ((0, 0), (0, 0), (0, LANES - rw.shape[-1])))
    rb = jnp.concatenate([router_group_b, router_expert_b], axis=-1)
    rb = jnp.pad(rb, ((0, 0), (0, LANES - rb.shape[-1]))).reshape(depth, 1, LANES)

    for i in range(depth):
        z = _mm(hb, w_in_t, i, 0, d_inner)
        xbc = _mm(hb, w_in_t, i, d_inner, xbc_w)
        dtr = _mm(hb, w_in_t, i, c_dt, LANES)
        pc = _mm(hb, w_in_t, i, c_pc, pool_width + 2 * conf_width)
        f_ssd = _ssd(z, xbc, dtr, ssd_conv_w[i], ssd_conv_b[i], ssd_dt_bias[i], ssd_a_log[i], ssd_d[i],
                     ssd_norm_g[i], nb, lp)
        f_pool = _pool(pc, pool_w[i], pool_scale[i], nb, lp)
        assert pool_width % conf_width == 0
        f_conf = _conf(pc, pool_width // conf_width, conf_dw_w[i], conf_dw_b[i], conf_ln_g[i], conf_ln_b[i], nb, lp)
        merged = _merge(hb, f_ssd, f_pool, f_conf, gate_w, gate_b, ssd_proj, pool_proj, conf_proj, i)
        h, h_packed = _mm_ln(merged, w_out, i, h, ln1_g[i], ln1_b[i], lp, alpha)

        ew, ew_t, cnt = _router(h, rw[i], rb[i], lp, n_groups, per_group)
        slot_tok, plan, slot_c = _dispatch(ew_t, cnt, n_groups, n_experts)
        xg = h_packed.at[slot_tok].get(mode="promise_in_bounds")
        out = _moe_ffn(xg, plan, exp_w_gate, exp_w_up, exp_w_down, i)
        g0 = out.at[slot_c[0]].get(mode="promise_in_bounds")
        g1 = out.at[slot_c[1]].get(mode="promise_in_bounds")
        if i + 1 < depth:
            h, hb = _combine_ln(h, g0, g1, ew, ln2_g[i], ln2_b[i], lp, alpha)
        else:
            h = _final_ln(h, g0, g1, ew, ln2_g[i], ln2_b[i], nb, lp, seq, alpha)

    return h.reshape(nb, seq, d)
```

```python
import functools

import jax
import jax.numpy as jnp
from jax import lax
from jax.experimental import pallas as pl
from jax.experimental.pallas import tpu as pltpu

F32 = jnp.float32
BF16 = jnp.bfloat16

PAD_ROWS = 112
SSD_CHUNK = 128
SSD_GROUPS = 4
SSD_HEAD_DIM = 64
POOL_WINDOWS = (2, 4, 8, 16)
MOE_BLOCK = 256
LN_EPS = 1e-5
LANES = 128
NEG_BIG = -1e30
LOG2E = 1.4426950408889634


def _cparams(sem, vmem_mb):
    return pltpu.CompilerParams(dimension_semantics=sem, vmem_limit_bytes=vmem_mb << 20)


def _sigmoid(x):
    return 1.0 / (1.0 + jnp.exp(-x))


def _softplus(x):
    return jnp.maximum(x, 0.0) + jnp.log1p(jnp.exp(-jnp.abs(x)))


def _ln(x, g, b):
    mu = jnp.mean(x, -1, keepdims=True)
    xc = x - mu
    var = jnp.mean(xc * xc, -1, keepdims=True)
    return xc * lax.rsqrt(var + LN_EPS) * g + b


def _pack_halves(y):
    n = y.shape[1] // 2
    lo = lax.bitcast_convert_type(y[:, :n].astype(BF16).astype(F32), jnp.uint32)
    hi = lax.bitcast_convert_type(y[:, n:].astype(BF16).astype(F32), jnp.uint32)
    return lax.bitcast_convert_type(hi | (lo >> 16), F32)


def _unpack_halves(p):
    u = lax.bitcast_convert_type(p, jnp.uint32)
    lo = lax.bitcast_convert_type(u << 16, F32)
    hi = lax.bitcast_convert_type(u & jnp.uint32(0xFFFF0000), F32)
    return jnp.concatenate([lo, hi], axis=1)


def _rows_in_batch(blk, bm, lp):
    base = lax.rem(blk, lp // bm) * bm
    return base + lax.broadcasted_iota(jnp.int32, (bm, 1), 0)


def _pick(n, cands):
    for c in cands:
        if n % c == 0:
            return c
    raise ValueError(f"no block size for {n} among {cands}")


def _embed_kernel(x_ref, g_ref, b_ref, h_ref, hb_ref, *, bm, lp):
    y = _ln(x_ref[...], g_ref[...], b_ref[...])
    y = jnp.where(_rows_in_batch(pl.program_id(0), bm, lp) >= PAD_ROWS, y, 0.0)
    h_ref[...] = y
    hb_ref[...] = y.astype(BF16)


def _embed_ln(xcat, g, b, lp):
    t, d = xcat.shape
    bm = _pick(lp, (528, 384, 192, 128, 64, 8))
    row = pl.BlockSpec((bm, d), lambda i: (i, 0))
    vec = pl.BlockSpec((1, d), lambda i: (0, 0))
    return pl.pallas_call(
        functools.partial(_embed_kernel, bm=bm, lp=lp),
        grid=(t // bm,),
        in_specs=[row, vec, vec],
        out_specs=[row, row],
        out_shape=[jax.ShapeDtypeStruct((t, d), F32), jax.ShapeDtypeStruct((t, d), BF16)],
        compiler_params=_cparams(("parallel",), 40),
        name="embed_ln",
    )(xcat, g.reshape(1, d), b.reshape(1, d))


def _combine_ln_kernel(h_ref, g0_ref, g1_ref, ew_ref, g_ref, b_ref, o_ref, ob_ref, *, bm, lp, alpha):
    y = ew_ref[:, 0:1] * _unpack_halves(g0_ref[...]) + ew_ref[:, 1:2] * _unpack_halves(g1_ref[...])
    y = _ln(alpha * h_ref[...] + y, g_ref[...], b_ref[...])
    y = jnp.where(_rows_in_batch(pl.program_id(0), bm, lp) >= PAD_ROWS, y, 0.0)
    o_ref[...] = y
    ob_ref[...] = y.astype(BF16)


def _combine_ln(h, g0, g1, ew, g, b, lp, alpha):
    t, d = h.shape
    bm = _pick(lp, (528, 384, 192, 128, 64, 8))
    row = pl.BlockSpec((bm, d), lambda i: (i, 0))
    half = pl.BlockSpec((bm, d // 2), lambda i: (i, 0))
    vec = pl.BlockSpec((1, d), lambda i: (0, 0))
    return pl.pallas_call(
        functools.partial(_combine_ln_kernel, bm=bm, lp=lp, alpha=alpha),
        grid=(t // bm,),
        in_specs=[row, half, half, pl.BlockSpec((bm, LANES), lambda i: (i, 0)), vec, vec],
        out_specs=[row, row],
        out_shape=[jax.ShapeDtypeStruct((t, d), F32), jax.ShapeDtypeStruct((t, d), BF16)],
        compiler_params=_cparams(("parallel",), 56),
        name="combine_ln",
    )(h, g0, g1, ew, g.reshape(1, d), b.reshape(1, d))


def _final_ln_kernel(h_ref, g0_ref, g1_ref, ew_ref, g_ref, b_ref, o_ref, *, alpha):
    y = ew_ref[:, 0:1] * _unpack_halves(g0_ref[...]) + ew_ref[:, 1:2] * _unpack_halves(g1_ref[...])
    o_ref[...] = _ln(alpha * h_ref[...] + y, g_ref[...], b_ref[...])


def _final_ln(h, g0, g1, ew, g, b, nb, lp, seq, alpha):
    d = h.shape[1]
    head = lp - seq
    bm = _pick(head, (128, 64, 8))
    assert seq % bm == 0
    per_seq, per_lp, skip = seq // bm, lp // bm, head // bm
    src = lambda i: ((i // per_seq) * per_lp + skip + i % per_seq, 0)
    row = pl.BlockSpec((bm, d), src)
    half = pl.BlockSpec((bm, d // 2), src)
    vec = pl.BlockSpec((1, d), lambda i: (0, 0))
    return pl.pallas_call(
        functools.partial(_final_ln_kernel, alpha=alpha),
        grid=(nb * per_seq,),
        in_specs=[row, half, half, pl.BlockSpec((bm, LANES), src), vec, vec],
        out_specs=pl.BlockSpec((bm, d), lambda i: (i, 0)),
        out_shape=jax.ShapeDtypeStruct((nb * seq, d), F32),
        compiler_params=_cparams(("parallel",), 32),
        name="final_ln",
    )(h, g0, g1, ew, g.reshape(1, d), b.reshape(1, d))


def _mm_kernel(x_ref, w_ref, *rest, shift):
    if shift:
        wnext_ref, o_ref, wbf_ref = rest
    else:
        o_ref, wbf_ref = rest
    bn = w_ref.shape[0]
    step = min(bn, 256)

    @pl.when(pl.program_id(1) == 0)
    def _():
        for c in range(bn // step):
            lo, hi = shift + c * step, shift + (c + 1) * step
            if hi <= bn:
                rows = w_ref[lo:hi, :]
            else:
                rows = jnp.concatenate([w_ref[lo:bn, :], wnext_ref[0:hi - bn, :]], axis=0)
            wbf_ref[:, c * step:(c + 1) * step] = rows.T.astype(BF16)

    o_ref[...] = jnp.dot(x_ref[...], wbf_ref[...], preferred_element_type=F32).astype(o_ref.dtype)


def _mm(x, w_t, layer, col0, n_cols, out_dtype=F32):
    t, k = x.shape
    bn = _pick(n_cols, (1024, 512, 256, 128))
    bm = _pick(t, (1056, 528, 384, 192, 128, 64, 8))
    shift = col0 % LANES
    base = col0 - shift
    assert base % bn == 0 and shift % 8 == 0
    cb0 = base // bn
    in_specs = [pl.BlockSpec((bm, k), lambda n, m: (m, 0)),
                pl.BlockSpec((None, bn, k), lambda n, m: (layer, cb0 + n, 0))]
    args = [x, w_t]
    if shift:
        tiles = bn // LANES
        in_specs.append(pl.BlockSpec((None, LANES, k), lambda n, m: (layer, (cb0 + n + 1) * tiles, 0)))
        args.append(w_t)
    return pl.pallas_call(
        functools.partial(_mm_kernel, shift=shift),
        grid=(n_cols // bn, t // bm),
        in_specs=in_specs,
        out_specs=pl.BlockSpec((bm, bn), lambda n, m: (m, n)),
        out_shape=jax.ShapeDtypeStruct((t, n_cols), out_dtype),
        scratch_shapes=[pltpu.VMEM((k, bn), BF16)],
        compiler_params=_cparams(("arbitrary", "arbitrary"), 48),
        name="mm",
    )(*args)


def _mm_ln_kernel(x_ref, w_ref, h_ref, g_ref, b_ref, rw_ref, rb_ref, o_ref, op_ref, ew_ref, ewt_ref, cnt_ref,
                  wbf_ref, run, *, bm, lp, alpha, n_groups, per_group):
    @pl.when(pl.program_id(0) == 0)
    def _():
        wbf_ref[...] = w_ref[...].astype(BF16)
        run[...] = jnp.zeros(run.shape, F32)

    y = jnp.dot(x_ref[...], wbf_ref[...], preferred_element_type=F32)
    y = _ln(alpha * h_ref[...] + y, g_ref[...], b_ref[...])
    real = _rows_in_batch(pl.program_id(0), bm, lp) >= PAD_ROWS
    y = jnp.where(real, y, 0.0)
    o_ref[...] = y
    op_ref[...] = _pack_halves(y)
    routed = _route_rows(y, rw_ref[...], rb_ref[...], run, real, n_groups, per_group)
    ew_ref[...] = routed
    ewt_ref[...] = routed.T[0:8, :]
    cnt_ref[...] = run[...]


def _mm_ln(x, w, layer, h, g, b, rw, rb, lp, alpha, n_groups, per_group):
    t, k = x.shape
    d = w.shape[-1]
    bm = _pick(lp, (384, 128))
    row = lambda width: pl.BlockSpec((bm, width), lambda i: (i, 0))
    vec = pl.BlockSpec((1, d), lambda i: (0, 0))
    lanes = pl.BlockSpec((1, LANES), lambda i: (0, 0))
    return pl.pallas_call(
        functools.partial(_mm_ln_kernel, bm=bm, lp=lp, alpha=alpha, n_groups=n_groups, per_group=per_group),
        grid=(t // bm,),
        in_specs=[row(k), pl.BlockSpec((None, k, d), lambda i: (layer, 0, 0), pipeline_mode=pl.Buffered(1)),
                  row(d), vec, vec, pl.BlockSpec((d, LANES), lambda i: (0, 0)), lanes],
        out_specs=[row(d), row(d // 2), row(LANES), pl.BlockSpec((8, bm), lambda i: (0, i)), lanes],
        out_shape=[jax.ShapeDtypeStruct((t, d), F32), jax.ShapeDtypeStruct((t, d // 2), F32),
                   jax.ShapeDtypeStruct((t, LANES), F32), jax.ShapeDtypeStruct((8, t), F32),
                   jax.ShapeDtypeStruct((1, LANES), F32)],
        scratch_shapes=[pltpu.VMEM((k, d), BF16), pltpu.VMEM((1, LANES), F32)],
        compiler_params=_cparams(("arbitrary",), 56),
        name="mm_ln",
    )(x, w, h, g.reshape(1, d), b.reshape(1, d), rw, rb)


def _ssd_kernel(z_ref, xbc_ref, dtr_ref, cw_ref, cb_ref, dtb_ref, alog_ref, dexp_ref, ng_ref, o_ref,
                cbuf, xc, ybuf, st, *, q, d_inner, n_state):
    c = pl.program_id(1)
    xbc_w = cbuf.shape[1]
    kc = cw_ref.shape[0]
    hist = 8

    @pl.when(c == 0)
    def _():
        cbuf[0:hist, :] = jnp.zeros((hist, xbc_w), F32)
        st[...] = jnp.zeros(st.shape, F32)

    cbuf[hist:hist + q, :] = xbc_ref[...]
    cw = 128
    for j in range(xbc_w // cw):
        sl = slice(j * cw, (j + 1) * cw)
        conv = cb_ref[:, sl] + cw_ref[kc - 1:kc, sl] * cbuf[hist:hist + q, sl]
        for k in range(kc - 1):
            off = hist - (kc - 1) + k
            conv = conv + cw_ref[k:k + 1, sl] * cbuf[off:off + q, sl]
        xc[:, sl] = conv * _sigmoid(conv)
    cbuf[0:hist, :] = cbuf[q:q + hist, :]

    row = c * q + lax.broadcasted_iota(jnp.int32, (q, 1), 0)
    dt = _softplus(dtr_ref[...] + dtb_ref[...])
    dt = jnp.where(row >= PAD_ROWS, dt, 0.0)
    adt = dt * (-jnp.exp(alog_ref[...]))
    li = lax.broadcasted_iota(jnp.int32, (q, q), 0)
    si = lax.broadcasted_iota(jnp.int32, (q, q), 1)
    tri = li >= si
    acs = jnp.dot(tri.astype(F32), adt, preferred_element_type=F32,
                  precision=lax.Precision.HIGHEST) * LOG2E
    acs_t = acs.T
    dt_t = dt.T
    e_acs = jnp.exp2(acs)
    last = acs_t[:, q - 1:q]
    w_t = jnp.exp2(last - acs_t) * dt_t
    src_t = acs_t - jnp.log2(dt_t)
    dlast = jnp.exp2(last)
    lane = lax.broadcasted_iota(jnp.int32, (1, LANES), 1)
    lo_half = lane < SSD_HEAD_DIM

    heads_per_group = d_inner // SSD_HEAD_DIM // SSD_GROUPS
    pairs_per_group = heads_per_group // 2
    for g in range(SSD_GROUPS):
        b0 = d_inner + g * n_state
        c0 = d_inner + SSD_GROUPS * n_state + g * n_state
        bg = xc[:, b0:b0 + n_state]
        cg = xc[:, c0:c0 + n_state]
        cbm = lax.dot_general(cg.astype(BF16), bg.astype(BF16), (((1,), (1,)), ((), ())),
                              preferred_element_type=F32)
        bg_t = bg.T
        for j in range(pairs_per_group):
            pair = g * pairs_per_group + j
            h0 = 2 * pair
            xs_pair = xc[:, h0 * SSD_HEAD_DIM:h0 * SSD_HEAD_DIM + LANES]
            st_pair = st[pair]
            lhs_y, lhs_s, rhs_x, rhs_st = [], [], [], []
            for u in range(2):
                h = h0 + u
                col = acs[:, h:h + 1]
                rw = src_t[h:h + 1, :]
                decay = jnp.where(tri, jnp.exp2(col - rw), 0.0)
                lhs_y.append((cbm * decay).astype(BF16))
                lhs_s.append((bg_t * w_t[h:h + 1, :]).astype(BF16))
                keep = lo_half if u == 0 else jnp.logical_not(lo_half)
                rhs_x.append(jnp.where(keep, xs_pair, 0.0).astype(BF16))
                rhs_st.append(jnp.where(keep, st_pair, 0.0).astype(BF16))
            for u in range(2):
                lhs_y.append((e_acs[:, h0 + u:h0 + u + 1] * cg).astype(BF16))
            y_pair = jnp.dot(jnp.concatenate(lhs_y, axis=1), jnp.concatenate(rhs_x + rhs_st, axis=0),
                             preferred_element_type=F32)
            s_new = jnp.dot(jnp.concatenate(lhs_s, axis=1), jnp.concatenate(rhs_x, axis=0),
                            preferred_element_type=F32)
            dpair = jnp.where(lo_half, dlast[h0:h0 + 1, :], dlast[h0 + 1:h0 + 2, :])
            st[pair] = st_pair * dpair + s_new
            ybuf[:, h0 * SSD_HEAD_DIM:h0 * SSD_HEAD_DIM + LANES] = y_pair

    gw = d_inner // SSD_GROUPS
    for g in range(SSD_GROUPS):
        sl = slice(g * gw, (g + 1) * gw)
        y = ybuf[:, sl] + xc[:, sl] * dexp_ref[:, sl]
        zz = z_ref[:, sl]
        y = y * (zz * _sigmoid(zz))
        y = y * lax.rsqrt(jnp.mean(y * y, -1, keepdims=True) + LN_EPS)
        o_ref[:, sl] = (y * ng_ref[:, sl]).astype(BF16)


def _ssd(z, xbc, dtr, conv_w, conv_b, dt_bias, a_log, d_skip, norm_g, nb, lp):
    t, d_inner = z.shape
    xbc_w = xbc.shape[1]
    q = SSD_CHUNK
    n_heads = d_inner // SSD_HEAD_DIM
    n_state = (xbc_w - d_inner) // (2 * SSD_GROUPS)
    assert n_state == LANES and n_heads % (2 * SSD_GROUPS) == 0 and lp % q == 0
    assert xbc_w % 512 == 0 and PAD_ROWS % 8 == 0
    nc = lp // q
    padh = LANES - n_heads
    row = lambda w: pl.BlockSpec((q, w), lambda b, c: (b * nc + c, 0))
    full = lambda a: pl.BlockSpec(a.shape, lambda b, c: (0,) * a.ndim)
    dtb = jnp.pad(dt_bias, (0, padh)).reshape(1, LANES)
    alog = jnp.pad(a_log, (0, padh)).reshape(1, LANES)
    dexp = jnp.repeat(d_skip, SSD_HEAD_DIM).reshape(1, d_inner)
    cb = conv_b.reshape(1, xbc_w)
    ng = norm_g.reshape(1, d_inner)
    return pl.pallas_call(
        functools.partial(_ssd_kernel, q=q, d_inner=d_inner, n_state=n_state),
        grid=(nb, nc),
        in_specs=[row(d_inner), row(xbc_w), row(LANES), full(conv_w), full(cb), full(dtb), full(alog),
                  full(dexp), full(ng)],
        out_specs=row(d_inner),
        out_shape=jax.ShapeDtypeStruct((t, d_inner), BF16),
        scratch_shapes=[pltpu.VMEM((8 + q, xbc_w), F32), pltpu.VMEM((q, xbc_w), F32),
                        pltpu.VMEM((q, d_inner), F32), pltpu.VMEM((n_heads // 2, n_state, LANES), F32)],
        compiler_params=_cparams(("arbitrary", "arbitrary"), 32),
        name="ssd",
    )(z, xbc, dtr, conv_w, cb, dtb, alog, dexp, ng)


def _pool_kernel(u_ref, w_ref, sc_ref, o_ref, buf, *, r, gd):
    c = pl.program_id(1)
    hist = 16
    width = buf.shape[1]

    @pl.when(c == 0)
    def _():
        buf[0:hist, :] = jnp.zeros((hist, width), F32)

    buf[hist:hist + r, :] = u_ref[...]
    pos = c * r + lax.broadcasted_iota(jnp.int32, (r, 1), 0) - PAD_ROWS
    for gi, win in enumerate(POOL_WINDOWS):
        sl = slice(gi * gd, (gi + 1) * gd)
        u = buf[hist:hist + r, sl]
        s = u
        for j in range(1, win):
            s = s + buf[hist - j:hist - j + r, sl]
        cnt = jnp.clip(pos + 1, 1, win).astype(F32)
        mixed = s / cnt - u
        y = jnp.dot(mixed.astype(BF16), w_ref[gi].astype(BF16), preferred_element_type=F32)
        o_ref[:, sl] = (y * sc_ref[:, sl]).astype(BF16)
    buf[0:hist, :] = buf[r:r + hist, :]


def _pool(pc, pool_w, scale, nb, lp):
    t = pc.shape[0]
    ng, gd, _ = pool_w.shape
    width = ng * gd
    assert ng == len(POOL_WINDOWS) and gd % LANES == 0
    r = _pick(lp, (384, 192, 128, 64))
    nc = lp // r
    return pl.pallas_call(
        functools.partial(_pool_kernel, r=r, gd=gd),
        grid=(nb, nc),
        in_specs=[pl.BlockSpec((r, width), lambda b, c: (b * nc + c, 0)),
                  pl.BlockSpec(pool_w.shape, lambda b, c: (0, 0, 0)),
                  pl.BlockSpec((1, width), lambda b, c: (0, 0))],
        out_specs=pl.BlockSpec((r, width), lambda b, c: (b * nc + c, 0)),
        out_shape=jax.ShapeDtypeStruct((t, width), BF16),
        scratch_shapes=[pltpu.VMEM((16 + r, width), F32)],
        compiler_params=_cparams(("arbitrary", "arbitrary"), 32),
        name="pool",
    )(pc, pool_w, scale.reshape(1, width))


def _conf_kernel(a_ref, g_ref, w_ref, b_ref, lg_ref, lb_ref, o_ref, vbuf, cv, ubuf, *, r):
    c = pl.program_id(1)
    hist = 32
    sub = 8
    width = vbuf.shape[1]
    kc = w_ref.shape[0]

    @pl.when(c == 0)
    def _():
        vbuf[0:hist, :] = jnp.zeros((hist, width), F32)

    vbuf[hist:hist + r, :] = a_ref[...] * _sigmoid(g_ref[...])
    ext = r + sub
    cw = ubuf.shape[2]
    for jc in range(width // cw):
        sl = slice(jc * cw, (jc + 1) * cw)
        acc = None
        for rr in range(sub):
            u = None
            for j in range((kc - 1 - rr) // sub + 1):
                k = kc - 1 - (sub * j + rr)
                lo = hist - sub - sub * j
                term = w_ref[k:k + 1, sl] * vbuf[lo:lo + ext, sl]
                u = term if u is None else u + term
            if rr == 0:
                acc = b_ref[:, sl] + u[sub:sub + r]
            else:
                ubuf[rr - 1] = u
                acc = acc + ubuf[rr - 1, sub - rr:sub - rr + r, :]
        cv[:, sl] = acc
    vbuf[0:hist, :] = vbuf[r:r + hist, :]
    v = _ln(cv[...], lg_ref[...], lb_ref[...])
    o_ref[...] = (v * _sigmoid(v)).astype(BF16)


def _conf(pc, col_blk, dw_w, dw_b, ln_g, ln_b, nb, lp):
    t = pc.shape[0]
    kc, width = dw_w.shape
    assert kc <= 33 and width % 256 == 0
    r = _pick(lp, (192, 128, 64))
    nc = lp // r
    vec = lambda: pl.BlockSpec((1, width), lambda b, c: (0, 0))
    return pl.pallas_call(
        functools.partial(_conf_kernel, r=r),
        grid=(nb, nc),
        in_specs=[pl.BlockSpec((r, width), lambda b, c: (b * nc + c, col_blk)),
                  pl.BlockSpec((r, width), lambda b, c: (b * nc + c, col_blk + 1)),
                  pl.BlockSpec((kc, width), lambda b, c: (0, 0)), vec(), vec(), vec()],
        out_specs=pl.BlockSpec((r, width), lambda b, c: (b * nc + c, 0)),
        out_shape=jax.ShapeDtypeStruct((t, width), BF16),
        scratch_shapes=[pltpu.VMEM((32 + r, width), F32), pltpu.VMEM((r, width), F32),
                        pltpu.VMEM((7, r + 8, LANES), F32)],
        compiler_params=_cparams(("arbitrary", "arbitrary"), 32),
        name="conf",
    )(pc, pc, dw_w, dw_b.reshape(1, width), ln_g.reshape(1, width), ln_b.reshape(1, width))


def _merge_kernel(hb_ref, f0_ref, f1_ref, f2_ref, g0_ref, g1_ref, g2_ref, gb_ref, p0_ref, p1_ref, p2_ref,
                  o_ref, gs, s0, s1, s2):
    @pl.when(pl.program_id(1) == 0)
    def _():
        gs[0] = g0_ref[...].astype(BF16)
        gs[1] = g1_ref[...].astype(BF16)
        gs[2] = g2_ref[...].astype(BF16)
        s0[...] = p0_ref[...].astype(BF16)
        s1[...] = p1_ref[...].astype(BF16)
        s2[...] = p2_ref[...].astype(BF16)

    hb = hb_ref[...]
    acc = None
    for j, (f_ref, s_ref) in enumerate(((f0_ref, s0), (f1_ref, s1), (f2_ref, s2))):
        gate = _sigmoid(jnp.dot(hb, gs[j], preferred_element_type=F32) + gb_ref[j:j + 1, :])
        term = gate * jnp.dot(f_ref[...], s_ref[...], preferred_element_type=F32)
        acc = term if acc is None else acc + term
    o_ref[...] = acc.astype(BF16)


def _merge(hb, f_ssd, f_pool, f_conf, gate_w, gate_b, ssd_proj, pool_proj, conf_proj, layer):
    t, d = hb.shape
    bn = _pick(d, (512, 256, 128))
    bm = _pick(t, (528, 384, 192, 128, 64, 8))
    nbn = d // bn
    once = pl.Buffered(1)
    rowspec = lambda a: pl.BlockSpec((bm, a.shape[1]), lambda n, m: (m, 0))
    gspec = lambda j: pl.BlockSpec((None, d, bn), lambda n, m: (layer, 0, j * nbn + n), pipeline_mode=once)
    pspec = lambda w: pl.BlockSpec((None, w.shape[1], bn), lambda n, m: (layer, 0, n), pipeline_mode=once)
    gb = gate_b[layer].reshape(3, d)
    return pl.pallas_call(
        _merge_kernel,
        grid=(nbn, t // bm),
        in_specs=[rowspec(hb), rowspec(f_ssd), rowspec(f_pool), rowspec(f_conf),
                  gspec(0), gspec(1), gspec(2), pl.BlockSpec((3, bn), lambda n, m: (0, n)),
                  pspec(ssd_proj), pspec(pool_proj), pspec(conf_proj)],
        out_specs=pl.BlockSpec((bm, bn), lambda n, m: (m, n)),
        out_shape=jax.ShapeDtypeStruct((t, d), BF16),
        scratch_shapes=[pltpu.VMEM((3, d, bn), BF16), pltpu.VMEM((ssd_proj.shape[1], bn), BF16),
                        pltpu.VMEM((pool_proj.shape[1], bn), BF16), pltpu.VMEM((conf_proj.shape[1], bn), BF16)],
        compiler_params=_cparams(("arbitrary", "arbitrary"), 52),
        name="merge",
    )(hb, f_ssd, f_pool, f_conf, gate_w, gate_w, gate_w, gb, ssd_proj, pool_proj, conf_proj)


def _route_rows(h, w, bias, run, real, n_groups, per_group):
    bm = h.shape[0]
    h_hi = h.astype(BF16)
    h_lo = (h - h_hi.astype(F32)).astype(BF16)
    w_hi = w.astype(BF16)
    w_lo = (w - w_hi.astype(F32)).astype(BF16)
    logits = (jnp.dot(h_hi, w_hi, preferred_element_type=F32)
              + (jnp.dot(h_hi, w_lo, preferred_element_type=F32)
                 + jnp.dot(h_lo, w_hi, preferred_element_type=F32))) + bias
    lane = lax.broadcasted_iota(jnp.int32, logits.shape, 1).astype(F32)

    def first_max(vals):
        m = jnp.max(vals, axis=1, keepdims=True)
        idx = jnp.min(jnp.where(vals == m, lane, float(LANES)), axis=1, keepdims=True)
        return m, idx

    gmask = lane < n_groups
    gmax, gsel = first_max(jnp.where(gmask, logits, NEG_BIG))
    gsum = jnp.sum(jnp.where(gmask, jnp.exp(logits - gmax), 0.0), axis=1, keepdims=True)
    p_group = 1.0 / gsum
    lo = n_groups + per_group * gsel
    el = jnp.where(jnp.logical_and(lane >= lo, lane < lo + per_group), logits, NEG_BIG)
    m1, i1 = first_max(el)
    m2, i2 = first_max(jnp.where(lane == i1, NEG_BIG, el))
    ratio = jnp.exp(m2 - m1)
    w1 = p_group / (1.0 + ratio)
    w2 = w1 * ratio
    sentinel = float(n_groups * per_group)
    e1 = jnp.where(real, i1 - n_groups, sentinel)
    e2 = jnp.where(real, i2 - n_groups, sentinel)
    w1 = jnp.where(real, w1, 0.0)
    w2 = jnp.where(real, w2, 0.0)
    hit1 = jnp.logical_and(lane == i1, real)
    hit2 = jnp.logical_and(lane == i2, real)
    onehot = jnp.where(jnp.logical_or(hit1, hit2), 1.0, 0.0)
    li = lax.broadcasted_iota(jnp.int32, (bm, bm), 0)
    si = lax.broadcasted_iota(jnp.int32, (bm, bm), 1)
    before = jnp.where(li > si, 1.0, 0.0).astype(BF16)
    seen = jnp.dot(before, onehot.astype(BF16), preferred_element_type=F32) + run[...]
    r1 = jnp.sum(jnp.where(hit1, seen, 0.0), axis=1, keepdims=True)
    r2 = jnp.sum(jnp.where(hit2, seen, 0.0), axis=1, keepdims=True)
    run[...] = run[...] + jnp.sum(onehot, axis=0, keepdims=True)
    out = jnp.zeros(logits.shape, F32)
    for k, v in enumerate((w1, w2, e1, e2, r1, r2)):
        out = jnp.where(lane == k, v, out)
    return out


def _moe_kernel(be_ref, nxt1_ref, nxt2_ref, par_ref, tot_ref, x_ref, wg_hbm, wu_hbm, wd_hbm, o_ref,
                wg_f, wu_f, wd_f, wg_s, wu_s, wd_s, sems, *, layer):
    i = pl.program_id(0)
    total = tot_ref[0]
    ic = jnp.maximum(jnp.minimum(i, total - 1), 0)
    e = be_ref[ic]
    e_prev = be_ref[jnp.maximum(ic - 1, 0)]
    active = i < total
    first_of_run = jnp.logical_and(active, jnp.logical_or(i == 0, e != e_prev))
    par = par_ref[ic]

    def weight_copies(expert, st):
        return (pltpu.make_async_copy(wg_hbm.at[layer, expert], wg_f.at[st], sems.at[st, 0]),
                pltpu.make_async_copy(wu_hbm.at[layer, expert], wu_f.at[st], sems.at[st, 1]),
                pltpu.make_async_copy(wd_hbm.at[layer, expert], wd_f.at[st], sems.at[st, 2]))

    @pl.when(jnp.logical_and(active, i == 0))
    def _():
        for cp in weight_copies(e, 0):
            cp.start()
        nxt1 = nxt1_ref[ic]

        @pl.when(nxt1 >= 0)
        def _():
            for cp in weight_copies(nxt1, 1):
                cp.start()

    @pl.when(first_of_run)
    def _():
        for cp, src, dst in zip(weight_copies(e, par), (wg_f, wu_f, wd_f), (wg_s, wu_s, wd_s)):
            cp.wait()
            dst[...] = src[par].astype(BF16)
        nxt2 = nxt2_ref[ic]

        @pl.when(nxt2 >= 0)
        def _():
            for cp in weight_copies(nxt2, par):
                cp.start()

    @pl.when(active)
    def _():
        x = _unpack_halves(x_ref[...]).astype(BF16)
        gte = jnp.dot(x, wg_s[...], preferred_element_type=F32)
        up = jnp.dot(x, wu_s[...], preferred_element_type=F32)
        hid = (gte * _sigmoid(gte)) * up
        o_ref[...] = _pack_halves(jnp.dot(hid.astype(BF16), wd_s[...], preferred_element_type=F32))

    @pl.when(jnp.logical_not(active))
    def _():
        o_ref[...] = jnp.zeros(o_ref.shape, F32)


def _moe_ffn(xg, plan, w_gate, w_up, w_down, layer):
    n_slots, d = xg.shape
    hid = w_gate.shape[-1]
    assert w_gate.shape[-2] == 2 * d
    n_blocks = n_slots // MOE_BLOCK

    def rows(i, be, nxt1, nxt2, par, tot):
        return (jnp.maximum(jnp.minimum(i, tot[0] - 1), 0), 0)

    hbm = pl.BlockSpec(memory_space=pl.ANY)
    grid_spec = pltpu.PrefetchScalarGridSpec(
        num_scalar_prefetch=5,
        grid=(n_blocks,),
        in_specs=[pl.BlockSpec((MOE_BLOCK, d), rows), hbm, hbm, hbm],
        out_specs=pl.BlockSpec((MOE_BLOCK, d), lambda i, *_: (i, 0)),
        scratch_shapes=[pltpu.VMEM((2, 2 * d, hid), F32), pltpu.VMEM((2, 2 * d, hid), F32),
                        pltpu.VMEM((2, hid, 2 * d), F32),
                        pltpu.VMEM((2 * d, hid), BF16), pltpu.VMEM((2 * d, hid), BF16),
                        pltpu.VMEM((hid, 2 * d), BF16), pltpu.SemaphoreType.DMA((2, 3))],
    )
    return pl.pallas_call(
        functools.partial(_moe_kernel, layer=layer),
        grid_spec=grid_spec,
        out_shape=jax.ShapeDtypeStruct((n_slots, d), F32),
        compiler_params=_cparams(("arbitrary",), 58),
        name="moe_ffn",
    )(*plan, xg, w_gate, w_up, w_down)


def _dispatch(ew_t, cnt, n_groups, n_experts):
    t = ew_t.shape[1]
    eid = ew_t[2:4].astype(jnp.int32).reshape(-1)
    rank = ew_t[4:6].astype(jnp.int32).reshape(-1)
    counts = cnt[0, n_groups:n_groups + n_experts].astype(jnp.int32)
    n_assign = eid.shape[0]
    tok = jnp.tile(jnp.arange(t, dtype=jnp.int32), 2)
    padded = (counts + MOE_BLOCK - 1) // MOE_BLOCK * MOE_BLOCK
    pend = jnp.cumsum(padded)
    pstart = pend - padded
    n_blocks = n_assign // MOE_BLOCK + n_experts
    n_slots = n_blocks * MOE_BLOCK
    valid = eid < n_experts
    slot = jnp.where(valid, pstart[jnp.minimum(eid, n_experts - 1)] + rank, n_slots)
    slot_tok = (jnp.arange(n_slots, dtype=jnp.int32) % t).at[slot].set(tok, mode="drop")
    block_expert = jnp.minimum(
        jnp.searchsorted(pend, jnp.arange(n_blocks, dtype=jnp.int32) * MOE_BLOCK, side="right"),
        n_experts - 1).astype(jnp.int32)
    total_blocks = (pend[-1] // MOE_BLOCK).astype(jnp.int32).reshape(1)
    ids = jnp.arange(n_experts, dtype=jnp.int32)
    has = counts > 0
    later = jnp.where(has, ids, n_experts)
    nxt1 = lax.cummin(jnp.concatenate([later[1:], jnp.full((1,), n_experts, jnp.int32)]), reverse=True)
    nxt2 = jnp.concatenate([nxt1, jnp.full((1,), n_experts, jnp.int32)])[nxt1]
    nxt1 = jnp.where(nxt1 >= n_experts, -1, nxt1)
    nxt2 = jnp.where(nxt2 >= n_experts, -1, nxt2)
    parity = (jnp.cumsum(has.astype(jnp.int32)) - 1) % 2
    plan = (block_expert, nxt1[block_expert], nxt2[block_expert], parity[block_expert], total_blocks)
    slot_c = jnp.where(valid, slot, 0).reshape(2, t)
    return slot_tok, plan, slot_c


def kernel(x, meta_tokens, ln_emb_g, ln_emb_b, w_in, ssd_conv_w, ssd_conv_b, ssd_dt_bias, ssd_a_log, ssd_d,
           ssd_norm_g, ssd_proj, pool_w, pool_scale, pool_proj, conf_dw_w, conf_dw_b, conf_ln_g, conf_ln_b,
           conf_proj, gate_w, gate_b, w_out, ln1_g, ln1_b, router_group_w, router_group_b, router_expert_w,
           router_expert_b, exp_w_gate, exp_w_up, exp_w_down, ln2_g, ln2_b):
    nb, seq, d = x.shape
    n_meta = meta_tokens.shape[0]
    depth = w_in.shape[0]
    lp = PAD_ROWS + n_meta + seq
    t = nb * lp
    alpha = (2.0 * depth) ** 0.25
    d_inner = ssd_norm_g.shape[-1]
    xbc_w = ssd_conv_w.shape[-1]
    n_heads = ssd_a_log.shape[-1]
    pool_width = pool_scale.shape[-1]
    conf_width = conf_dw_b.shape[-1]
    n_groups = router_group_w.shape[-1]
    n_experts = router_expert_w.shape[-1]
    per_group = n_experts // n_groups
    assert (PAD_ROWS + n_meta) % SSD_CHUNK == 0 and lp % SSD_CHUNK == 0
    assert n_groups + n_experts <= LANES and n_heads <= LANES

    meta = jnp.broadcast_to(meta_tokens[None].astype(x.dtype), (nb, n_meta, d))
    xcat = jnp.concatenate([jnp.zeros((nb, PAD_ROWS, d), x.dtype), meta, x], axis=1).reshape(t, d)
    h, hb = _embed_ln(xcat, ln_emb_g, ln_emb_b, lp)

    c_dt = d_inner + xbc_w
    c_pc = c_dt + n_heads
    assert c_dt % LANES == 0 and c_dt + LANES <= w_in.shape[-1]
    w_in_t = jnp.swapaxes(w_in, 1, 2)
    rw = jnp.concatenate([router_group_w, router_expert_w], axis=-1)
    rw = jnp.pad(rw, ((0, 0), (0, 0), (0, LANES - rw.shape[-1])))
    rb = jnp.concatenate([router_group_b, router_expert_b], axis=-1)
    rb = jnp.pad(rb, ((0, 0), (0, LANES - rb.shape[-1]))).reshape(depth, 1, LANES)

    for i in range(depth):
        z = _mm(hb, w_in_t, i, 0, d_inner)
        xbc = _mm(hb, w_in_t, i, d_inner, xbc_w)
        dtr = _mm(hb, w_in_t, i, c_dt, LANES)
        pc = _mm(hb, w_in_t, i, c_pc, pool_width + 2 * conf_width)
        f_ssd = _ssd(z, xbc, dtr, ssd_conv_w[i], ssd_conv_b[i], ssd_dt_bias[i], ssd_a_log[i], ssd_d[i],
                     ssd_norm_g[i], nb, lp)
        f_pool = _pool(pc, pool_w[i], pool_scale[i], nb, lp)
        assert pool_width % conf_width == 0
        f_conf = _conf(pc, pool_width // conf_width, conf_dw_w[i], conf_dw_b[i], conf_ln_g[i], conf_ln_b[i], nb, lp)
        merged = _merge(hb, f_ssd, f_pool, f_conf, gate_w, gate_b, ssd_proj, pool_proj, conf_proj, i)
        h, h_packed, ew, ew_t, cnt = _mm_ln(merged, w_out, i, h, ln1_g[i], ln1_b[i], rw[i], rb[i], lp, alpha,
                                            n_groups, per_group)
        slot_tok, plan, slot_c = _dispatch(ew_t, cnt, n_groups, n_experts)
        xg = h_packed.at[slot_tok].get(mode="promise_in_bounds")
        out = _moe_ffn(xg, plan, exp_w_gate, exp_w_up, exp_w_down, i)
        g0 = out.at[slot_c[0]].get(mode="promise_in_bounds")
        g1 = out.at[slot_c[1]].get(mode="promise_in_bounds")
        if i + 1 < depth:
            h, hb = _combine_ln(h, g0, g1, ew, ln2_g[i], ln2_b[i], lp, alpha)
        else:
            h = _final_ln(h, g0, g1, ew, ln2_g[i], ln2_b[i], nb, lp, seq, alpha)

    return h.reshape(nb, seq, d)
```

```python
import functools

import jax
import jax.numpy as jnp
from jax import lax
from jax.experimental import pallas as pl
from jax.experimental.pallas import tpu as pltpu

F32 = jnp.float32
BF16 = jnp.bfloat16

PAD_ROWS = 112
SSD_CHUNK = 128
SSD_GROUPS = 4
SSD_HEAD_DIM = 64
POOL_WINDOWS = (2, 4, 8, 16)
MOE_BLOCK = 256
LN_EPS = 1e-5
LANES = 128
NEG_BIG = -1e30
LOG2E = 1.4426950408889634


def _cparams(sem, vmem_mb):
    return pltpu.CompilerParams(dimension_semantics=sem, vmem_limit_bytes=vmem_mb << 20)


def _sigmoid(x):
    return 1.0 / (1.0 + jnp.exp(-x))


def _softplus(x):
    return jnp.maximum(x, 0.0) + jnp.log1p(jnp.exp(-jnp.abs(x)))


def _ln(x, g, b):
    mu = jnp.mean(x, -1, keepdims=True)
    xc = x - mu
    var = jnp.mean(xc * xc, -1, keepdims=True)
    return xc * lax.rsqrt(var + LN_EPS) * g + b


def _pack_halves(y):
    n = y.shape[1] // 2
    lo = lax.bitcast_convert_type(y[:, :n].astype(BF16).astype(F32), jnp.uint32)
    hi = lax.bitcast_convert_type(y[:, n:].astype(BF16).astype(F32), jnp.uint32)
    return lax.bitcast_convert_type(hi | (lo >> 16), F32)


def _unpack_halves(p):
    u = lax.bitcast_convert_type(p, jnp.uint32)
    lo = lax.bitcast_convert_type(u << 16, F32)
    hi = lax.bitcast_convert_type(u & jnp.uint32(0xFFFF0000), F32)
    return jnp.concatenate([lo, hi], axis=1)


def _rows_in_batch(blk, bm, lp):
    base = lax.rem(blk, lp // bm) * bm
    return base + lax.broadcasted_iota(jnp.int32, (bm, 1), 0)


def _pick(n, cands):
    for c in cands:
        if n % c == 0:
            return c
    raise ValueError(f"no block size for {n} among {cands}")


def _embed_kernel(x_ref, g_ref, b_ref, h_ref, hb_ref, *, bm, lp):
    y = _ln(x_ref[...], g_ref[...], b_ref[...])
    y = jnp.where(_rows_in_batch(pl.program_id(0), bm, lp) >= PAD_ROWS, y, 0.0)
    h_ref[...] = y
    hb_ref[...] = y.astype(BF16)


def _embed_ln(xcat, g, b, lp):
    t, d = xcat.shape
    bm = _pick(lp, (528, 384, 192, 128, 64, 8))
    row = pl.BlockSpec((bm, d), lambda i: (i, 0))
    vec = pl.BlockSpec((1, d), lambda i: (0, 0))
    return pl.pallas_call(
        functools.partial(_embed_kernel, bm=bm, lp=lp),
        grid=(t // bm,),
        in_specs=[row, vec, vec],
        out_specs=[row, row],
        out_shape=[jax.ShapeDtypeStruct((t, d), F32), jax.ShapeDtypeStruct((t, d), BF16)],
        compiler_params=_cparams(("parallel",), 40),
        name="embed_ln",
    )(xcat, g.reshape(1, d), b.reshape(1, d))


def _combine_ln_kernel(h_ref, g0_ref, g1_ref, ew_ref, g_ref, b_ref, o_ref, ob_ref, *, bm, lp, alpha):
    y = ew_ref[:, 0:1] * _unpack_halves(g0_ref[...]) + ew_ref[:, 1:2] * _unpack_halves(g1_ref[...])
    y = _ln(alpha * h_ref[...] + y, g_ref[...], b_ref[...])
    y = jnp.where(_rows_in_batch(pl.program_id(0), bm, lp) >= PAD_ROWS, y, 0.0)
    o_ref[...] = y
    ob_ref[...] = y.astype(BF16)


def _combine_ln(h, g0, g1, ew, g, b, lp, alpha):
    t, d = h.shape
    bm = _pick(lp, (528, 384, 192, 128, 64, 8))
    row = pl.BlockSpec((bm, d), lambda i: (i, 0))
    half = pl.BlockSpec((bm, d // 2), lambda i: (i, 0))
    vec = pl.BlockSpec((1, d), lambda i: (0, 0))
    return pl.pallas_call(
        functools.partial(_combine_ln_kernel, bm=bm, lp=lp, alpha=alpha),
        grid=(t // bm,),
        in_specs=[row, half, half, pl.BlockSpec((bm, LANES), lambda i: (i, 0)), vec, vec],
        out_specs=[row, row],
        out_shape=[jax.ShapeDtypeStruct((t, d), F32), jax.ShapeDtypeStruct((t, d), BF16)],
        compiler_params=_cparams(("parallel",), 56),
        name="combine_ln",
    )(h, g0, g1, ew, g.reshape(1, d), b.reshape(1, d))


def _final_ln_kernel(h_ref, g0_ref, g1_ref, ew_ref, g_ref, b_ref, o_ref, *, alpha):
    y = ew_ref[:, 0:1] * _unpack_halves(g0_ref[...]) + ew_ref[:, 1:2] * _unpack_halves(g1_ref[...])
    o_ref[...] = _ln(alpha * h_ref[...] + y, g_ref[...], b_ref[...])


def _final_ln(h, g0, g1, ew, g, b, nb, lp, seq, alpha):
    d = h.shape[1]
    head = lp - seq
    bm = _pick(head, (128, 64, 8))
    assert seq % bm == 0
    per_seq, per_lp, skip = seq // bm, lp // bm, head // bm
    src = lambda i: ((i // per_seq) * per_lp + skip + i % per_seq, 0)
    row = pl.BlockSpec((bm, d), src)
    half = pl.BlockSpec((bm, d // 2), src)
    vec = pl.BlockSpec((1, d), lambda i: (0, 0))
    return pl.pallas_call(
        functools.partial(_final_ln_kernel, alpha=alpha),
        grid=(nb * per_seq,),
        in_specs=[row, half, half, pl.BlockSpec((bm, LANES), src), vec, vec],
        out_specs=pl.BlockSpec((bm, d), lambda i: (i, 0)),
        out_shape=jax.ShapeDtypeStruct((nb * seq, d), F32),
        compiler_params=_cparams(("parallel",), 32),
        name="final_ln",
    )(h, g0, g1, ew, g.reshape(1, d), b.reshape(1, d))


def _mm_kernel(x_ref, w_ref, *rest, shift):
    if shift:
        wnext_ref, o_ref, wbf_ref = rest
    else:
        o_ref, wbf_ref = rest
    bn = w_ref.shape[0]
    step = min(bn, 256)

    @pl.when(pl.program_id(1) == 0)
    def _():
        for c in range(bn // step):
            lo, hi = shift + c * step, shift + (c + 1) * step
            if hi <= bn:
                rows = w_ref[lo:hi, :]
            else:
                rows = jnp.concatenate([w_ref[lo:bn, :], wnext_ref[0:hi - bn, :]], axis=0)
            wbf_ref[:, c * step:(c + 1) * step] = rows.T.astype(BF16)

    o_ref[...] = jnp.dot(x_ref[...], wbf_ref[...], preferred_element_type=F32).astype(o_ref.dtype)


def _mm(x, w_t, layer, col0, n_cols, out_dtype=F32):
    t, k = x.shape
    bn = _pick(n_cols, (1024, 512, 256, 128))
    bm = _pick(t, (1056, 528, 384, 192, 128, 64, 8))
    shift = col0 % LANES
    base = col0 - shift
    assert base % bn == 0 and shift % 8 == 0
    cb0 = base // bn
    in_specs = [pl.BlockSpec((bm, k), lambda n, m: (m, 0)),
                pl.BlockSpec((None, bn, k), lambda n, m: (layer, cb0 + n, 0))]
    args = [x, w_t]
    if shift:
        tiles = bn // LANES
        in_specs.append(pl.BlockSpec((None, LANES, k), lambda n, m: (layer, (cb0 + n + 1) * tiles, 0)))
        args.append(w_t)
    return pl.pallas_call(
        functools.partial(_mm_kernel, shift=shift),
        grid=(n_cols // bn, t // bm),
        in_specs=in_specs,
        out_specs=pl.BlockSpec((bm, bn), lambda n, m: (m, n)),
        out_shape=jax.ShapeDtypeStruct((t, n_cols), out_dtype),
        scratch_shapes=[pltpu.VMEM((k, bn), BF16)],
        compiler_params=_cparams(("arbitrary", "arbitrary"), 48),
        name="mm",
    )(*args)


def _mm_ln_kernel(x_ref, w_ref, h_ref, g_ref, b_ref, rw_ref, rb_ref, o_ref, op_ref, ew_ref, ewt_ref, cnt_ref,
                  wbf_ref, run, *, bm, lp, alpha, n_groups, per_group):
    @pl.when(pl.program_id(0) == 0)
    def _():
        wbf_ref[...] = w_ref[...].astype(BF16)
        run[...] = jnp.zeros(run.shape, F32)

    y = jnp.dot(x_ref[...], wbf_ref[...], preferred_element_type=F32)
    y = _ln(alpha * h_ref[...] + y, g_ref[...], b_ref[...])
    real = _rows_in_batch(pl.program_id(0), bm, lp) >= PAD_ROWS
    y = jnp.where(real, y, 0.0)
    o_ref[...] = y
    op_ref[...] = _pack_halves(y)
    routed = _route_rows(y, rw_ref[...], rb_ref[...], run, real, n_groups, per_group)
    ew_ref[...] = routed
    ewt_ref[...] = routed.T[0:8, :]
    cnt_ref[...] = run[...]


def _mm_ln(x, w, layer, h, g, b, rw, rb, lp, alpha, n_groups, per_group):
    t, k = x.shape
    d = w.shape[-1]
    bm = _pick(lp, (384, 128))
    row = lambda width: pl.BlockSpec((bm, width), lambda i: (i, 0))
    vec = pl.BlockSpec((1, d), lambda i: (0, 0))
    lanes = pl.BlockSpec((1, LANES), lambda i: (0, 0))
    return pl.pallas_call(
        functools.partial(_mm_ln_kernel, bm=bm, lp=lp, alpha=alpha, n_groups=n_groups, per_group=per_group),
        grid=(t // bm,),
        in_specs=[row(k), pl.BlockSpec((None, k, d), lambda i: (layer, 0, 0), pipeline_mode=pl.Buffered(1)),
                  row(d), vec, vec, pl.BlockSpec((d, LANES), lambda i: (0, 0)), lanes],
        out_specs=[row(d), row(d // 2), row(LANES), pl.BlockSpec((8, bm), lambda i: (0, i)), lanes],
        out_shape=[jax.ShapeDtypeStruct((t, d), F32), jax.ShapeDtypeStruct((t, d // 2), F32),
                   jax.ShapeDtypeStruct((t, LANES), F32), jax.ShapeDtypeStruct((8, t), F32),
                   jax.ShapeDtypeStruct((1, LANES), F32)],
        scratch_shapes=[pltpu.VMEM((k, d), BF16), pltpu.VMEM((1, LANES), F32)],
        compiler_params=_cparams(("arbitrary",), 56),
        name="mm_ln",
    )(x, w, h, g.reshape(1, d), b.reshape(1, d), rw, rb)


def _ssd_kernel(z_ref, xbc_ref, dtr_ref, cw_ref, cb_ref, dtb_ref, alog_ref, dexp_ref, ng_ref, o_ref,
                cbuf, xc, ybuf, st, *, q, d_inner, n_state):
    c = pl.program_id(1)
    xbc_w = cbuf.shape[1]
    kc = cw_ref.shape[0]
    hist = 8

    @pl.when(c == 0)
    def _():
        cbuf[0:hist, :] = jnp.zeros((hist, xbc_w), F32)
        st[...] = jnp.zeros(st.shape, F32)

    cbuf[hist:hist + q, :] = xbc_ref[...]
    cw = 128
    for j in range(xbc_w // cw):
        sl = slice(j * cw, (j + 1) * cw)
        conv = cb_ref[:, sl] + cw_ref[kc - 1:kc, sl] * cbuf[hist:hist + q, sl]
        for k in range(kc - 1):
            off = hist - (kc - 1) + k
            conv = conv + cw_ref[k:k + 1, sl] * cbuf[off:off + q, sl]
        xc[:, sl] = conv * _sigmoid(conv)
    cbuf[0:hist, :] = cbuf[q:q + hist, :]

    row = c * q + lax.broadcasted_iota(jnp.int32, (q, 1), 0)
    dt = _softplus(dtr_ref[...] + dtb_ref[...])
    dt = jnp.where(row >= PAD_ROWS, dt, 0.0)
    adt = dt * (-jnp.exp(alog_ref[...]))
    li = lax.broadcasted_iota(jnp.int32, (q, q), 0)
    si = lax.broadcasted_iota(jnp.int32, (q, q), 1)
    tri = li >= si
    acs = jnp.dot(tri.astype(F32), adt, preferred_element_type=F32,
                  precision=lax.Precision.HIGHEST) * LOG2E
    acs_t = acs.T
    dt_t = dt.T
    e_acs = jnp.exp2(acs)
    last = acs_t[:, q - 1:q]
    w_t = jnp.exp2(last - acs_t) * dt_t
    src_t = acs_t - jnp.log2(dt_t)
    dlast = jnp.exp2(last)
    lane = lax.broadcasted_iota(jnp.int32, (1, LANES), 1)
    lo_half = lane < SSD_HEAD_DIM

    heads_per_group = d_inner // SSD_HEAD_DIM // SSD_GROUPS
    pairs_per_group = heads_per_group // 2
    for g in range(SSD_GROUPS):
        b0 = d_inner + g * n_state
        c0 = d_inner + SSD_GROUPS * n_state + g * n_state
        bg = xc[:, b0:b0 + n_state]
        cg = xc[:, c0:c0 + n_state]
        cbm = lax.dot_general(cg.astype(BF16), bg.astype(BF16), (((1,), (1,)), ((), ())),
                              preferred_element_type=F32)
        bg_t = bg.T
        for j in range(pairs_per_group):
            pair = g * pairs_per_group + j
            h0 = 2 * pair
            xs_pair = xc[:, h0 * SSD_HEAD_DIM:h0 * SSD_HEAD_DIM + LANES]
            st_pair = st[pair]
            lhs_y, lhs_s, rhs_x, rhs_st = [], [], [], []
            for u in range(2):
                h = h0 + u
                col = acs[:, h:h + 1]
                rw = src_t[h:h + 1, :]
                decay = jnp.where(tri, jnp.exp2(col - rw), 0.0)
                lhs_y.append((cbm * decay).astype(BF16))
                lhs_s.append((bg_t * w_t[h:h + 1, :]).astype(BF16))
                keep = lo_half if u == 0 else jnp.logical_not(lo_half)
                rhs_x.append(jnp.where(keep, xs_pair, 0.0).astype(BF16))
                rhs_st.append(jnp.where(keep, st_pair, 0.0).astype(BF16))
            for u in range(2):
                lhs_y.append((e_acs[:, h0 + u:h0 + u + 1] * cg).astype(BF16))
            y_pair = jnp.dot(jnp.concatenate(lhs_y, axis=1), jnp.concatenate(rhs_x + rhs_st, axis=0),
                             preferred_element_type=F32)
            s_new = jnp.dot(jnp.concatenate(lhs_s, axis=1), jnp.concatenate(rhs_x, axis=0),
                            preferred_element_type=F32)
            dpair = jnp.where(lo_half, dlast[h0:h0 + 1, :], dlast[h0 + 1:h0 + 2, :])
            st[pair] = st_pair * dpair + s_new
            ybuf[:, h0 * SSD_HEAD_DIM:h0 * SSD_HEAD_DIM + LANES] = y_pair

    gw = d_inner // SSD_GROUPS
    for g in range(SSD_GROUPS):
        sl = slice(g * gw, (g + 1) * gw)
        y = ybuf[:, sl] + xc[:, sl] * dexp_ref[:, sl]
        zz = z_ref[:, sl]
        y = y * (zz * _sigmoid(zz))
        y = y * lax.rsqrt(jnp.mean(y * y, -1, keepdims=True) + LN_EPS)
        o_ref[:, sl] = (y * ng_ref[:, sl]).astype(BF16)


def _ssd(z, xbc, dtr, conv_w, conv_b, dt_bias, a_log, d_skip, norm_g, nb, lp):
    t, d_inner = z.shape
    xbc_w = xbc.shape[1]
    q = SSD_CHUNK
    n_heads = d_inner // SSD_HEAD_DIM
    n_state = (xbc_w - d_inner) // (2 * SSD_GROUPS)
    assert n_state == LANES and n_heads % (2 * SSD_GROUPS) == 0 and lp % q == 0
    assert xbc_w % 512 == 0 and PAD_ROWS % 8 == 0
    nc = lp // q
    padh = LANES - n_heads
    row = lambda w: pl.BlockSpec((q, w), lambda b, c: (b * nc + c, 0))
    full = lambda a: pl.BlockSpec(a.shape, lambda b, c: (0,) * a.ndim)
    dtb = jnp.pad(dt_bias, (0, padh)).reshape(1, LANES)
    alog = jnp.pad(a_log, (0, padh)).reshape(1, LANES)
    dexp = jnp.repeat(d_skip, SSD_HEAD_DIM).reshape(1, d_inner)
    cb = conv_b.reshape(1, xbc_w)
    ng = norm_g.reshape(1, d_inner)
    return pl.pallas_call(
        functools.partial(_ssd_kernel, q=q, d_inner=d_inner, n_state=n_state),
        grid=(nb, nc),
        in_specs=[row(d_inner), row(xbc_w), row(LANES), full(conv_w), full(cb), full(dtb), full(alog),
                  full(dexp), full(ng)],
        out_specs=row(d_inner),
        out_shape=jax.ShapeDtypeStruct((t, d_inner), BF16),
        scratch_shapes=[pltpu.VMEM((8 + q, xbc_w), F32), pltpu.VMEM((q, xbc_w), F32),
                        pltpu.VMEM((q, d_inner), F32), pltpu.VMEM((n_heads // 2, n_state, LANES), F32)],
        compiler_params=_cparams(("arbitrary", "arbitrary"), 32),
        name="ssd",
    )(z, xbc, dtr, conv_w, cb, dtb, alog, dexp, ng)


def _pool_kernel(u_ref, w_ref, sc_ref, o_ref, buf, *, r, gd):
    c = pl.program_id(1)
    hist = 16
    width = buf.shape[1]

    @pl.when(c == 0)
    def _():
        buf[0:hist, :] = jnp.zeros((hist, width), F32)

    buf[hist:hist + r, :] = u_ref[...]
    pos = c * r + lax.broadcasted_iota(jnp.int32, (r, 1), 0) - PAD_ROWS
    for gi, win in enumerate(POOL_WINDOWS):
        sl = slice(gi * gd, (gi + 1) * gd)
        u = buf[hist:hist + r, sl]
        s = u
        for j in range(1, win):
            s = s + buf[hist - j:hist - j + r, sl]
        cnt = jnp.clip(pos + 1, 1, win).astype(F32)
        mixed = s / cnt - u
        y = jnp.dot(mixed.astype(BF16), w_ref[gi].astype(BF16), preferred_element_type=F32)
        o_ref[:, sl] = (y * sc_ref[:, sl]).astype(BF16)
    buf[0:hist, :] = buf[r:r + hist, :]


def _pool(pc, pool_w, scale, nb, lp):
    t = pc.shape[0]
    ng, gd, _ = pool_w.shape
    width = ng * gd
    assert ng == len(POOL_WINDOWS) and gd % LANES == 0
    r = _pick(lp, (384, 192, 128, 64))
    nc = lp // r
    return pl.pallas_call(
        functools.partial(_pool_kernel, r=r, gd=gd),
        grid=(nb, nc),
        in_specs=[pl.BlockSpec((r, width), lambda b, c: (b * nc + c, 0)),
                  pl.BlockSpec(pool_w.shape, lambda b, c: (0, 0, 0)),
                  pl.BlockSpec((1, width), lambda b, c: (0, 0))],
        out_specs=pl.BlockSpec((r, width), lambda b, c: (b * nc + c, 0)),
        out_shape=jax.ShapeDtypeStruct((t, width), BF16),
        scratch_shapes=[pltpu.VMEM((16 + r, width), F32)],
        compiler_params=_cparams(("arbitrary", "arbitrary"), 32),
        name="pool",
    )(pc, pool_w, scale.reshape(1, width))


def _conf_kernel(a_ref, g_ref, w_ref, b_ref, lg_ref, lb_ref, o_ref, vbuf, cv, ubuf, *, r):
    c = pl.program_id(1)
    hist = 32
    sub = 8
    width = vbuf.shape[1]
    kc = w_ref.shape[0]

    @pl.when(c == 0)
    def _():
        vbuf[0:hist, :] = jnp.zeros((hist, width), F32)

    vbuf[hist:hist + r, :] = a_ref[...] * _sigmoid(g_ref[...])
    ext = r + sub
    cw = ubuf.shape[2]
    for jc in range(width // cw):
        sl = slice(jc * cw, (jc + 1) * cw)
        acc = None
        for rr in range(sub):
            u = None
            for j in range((kc - 1 - rr) // sub + 1):
                k = kc - 1 - (sub * j + rr)
                lo = hist - sub - sub * j
                term = w_ref[k:k + 1, sl] * vbuf[lo:lo + ext, sl]
                u = term if u is None else u + term
            if rr == 0:
                acc = b_ref[:, sl] + u[sub:sub + r]
            else:
                ubuf[rr - 1] = u
                acc = acc + ubuf[rr - 1, sub - rr:sub - rr + r, :]
        cv[:, sl] = acc
    vbuf[0:hist, :] = vbuf[r:r + hist, :]
    v = _ln(cv[...], lg_ref[...], lb_ref[...])
    o_ref[...] = (v * _sigmoid(v)).astype(BF16)


def _conf(pc, col_blk, dw_w, dw_b, ln_g, ln_b, nb, lp):
    t = pc.shape[0]
    kc, width = dw_w.shape
    assert kc <= 33 and width % 256 == 0
    r = _pick(lp, (192, 128, 64))
    nc = lp // r
    vec = lambda: pl.BlockSpec((1, width), lambda b, c: (0, 0))
    return pl.pallas_call(
        functools.partial(_conf_kernel, r=r),
        grid=(nb, nc),
        in_specs=[pl.BlockSpec((r, width), lambda b, c: (b * nc + c, col_blk)),
                  pl.BlockSpec((r, width), lambda b, c: (b * nc + c, col_blk + 1)),
                  pl.BlockSpec((kc, width), lambda b, c: (0, 0)), vec(), vec(), vec()],
        out_specs=pl.BlockSpec((r, width), lambda b, c: (b * nc + c, 0)),
        out_shape=jax.ShapeDtypeStruct((t, width), BF16),
        scratch_shapes=[pltpu.VMEM((32 + r, width), F32), pltpu.VMEM((r, width), F32),
                        pltpu.VMEM((7, r + 8, LANES), F32)],
        compiler_params=_cparams(("arbitrary", "arbitrary"), 32),
        name="conf",
    )(pc, pc, dw_w, dw_b.reshape(1, width), ln_g.reshape(1, width), ln_b.reshape(1, width))


def _merge_kernel(hb_ref, f0_ref, f1_ref, f2_ref, g0_ref, g1_ref, g2_ref, gb_ref, p0_ref, p1_ref, p2_ref,
                  o_ref, gs, s0, s1, s2):
    @pl.when(pl.program_id(1) == 0)
    def _():
        gs[0] = g0_ref[...].astype(BF16)
        gs[1] = g1_ref[...].astype(BF16)
        gs[2] = g2_ref[...].astype(BF16)
        s0[...] = p0_ref[...].astype(BF16)
        s1[...] = p1_ref[...].astype(BF16)
        s2[...] = p2_ref[...].astype(BF16)

    hb = hb_ref[...]
    acc = None
    for j, (f_ref, s_ref) in enumerate(((f0_ref, s0), (f1_ref, s1), (f2_ref, s2))):
        gate = _sigmoid(jnp.dot(hb, gs[j], preferred_element_type=F32) + gb_ref[j:j + 1, :])
        term = gate * jnp.dot(f_ref[...], s_ref[...], preferred_element_type=F32)
        acc = term if acc is None else acc + term
    o_ref[...] = acc.astype(BF16)


def _merge(hb, f_ssd, f_pool, f_conf, gate_w, gate_b, ssd_proj, pool_proj, conf_proj, layer):
    t, d = hb.shape
    bn = _pick(d, (512, 256, 128))
    bm = _pick(t, (528, 384, 192, 128, 64, 8))
    nbn = d // bn
    once = pl.Buffered(1)
    rowspec = lambda a: pl.BlockSpec((bm, a.shape[1]), lambda n, m: (m, 0))
    gspec = lambda j: pl.BlockSpec((None, d, bn), lambda n, m: (layer, 0, j * nbn + n), pipeline_mode=once)
    pspec = lambda w: pl.BlockSpec((None, w.shape[1], bn), lambda n, m: (layer, 0, n), pipeline_mode=once)
    gb = gate_b[layer].reshape(3, d)
    return pl.pallas_call(
        _merge_kernel,
        grid=(nbn, t // bm),
        in_specs=[rowspec(hb), rowspec(f_ssd), rowspec(f_pool), rowspec(f_conf),
                  gspec(0), gspec(1), gspec(2), pl.BlockSpec((3, bn), lambda n, m: (0, n)),
                  pspec(ssd_proj), pspec(pool_proj), pspec(conf_proj)],
        out_specs=pl.BlockSpec((bm, bn), lambda n, m: (m, n)),
        out_shape=jax.ShapeDtypeStruct((t, d), BF16),
        scratch_shapes=[pltpu.VMEM((3, d, bn), BF16), pltpu.VMEM((ssd_proj.shape[1], bn), BF16),
                        pltpu.VMEM((pool_proj.shape[1], bn), BF16), pltpu.VMEM((conf_proj.shape[1], bn), BF16)],
        compiler_params=_cparams(("arbitrary", "arbitrary"), 52),
        name="merge",
    )(hb, f_ssd, f_pool, f_conf, gate_w, gate_w, gate_w, gb, ssd_proj, pool_proj, conf_proj)


def _route_rows(h, w, bias, run, real, n_groups, per_group):
    bm = h.shape[0]
    h_hi = h.astype(BF16)
    h_lo = (h - h_hi.astype(F32)).astype(BF16)
    w_hi = w.astype(BF16)
    w_lo = (w - w_hi.astype(F32)).astype(BF16)
    logits = (jnp.dot(h_hi, w_hi, preferred_element_type=F32)
              + (jnp.dot(h_hi, w_lo, preferred_element_type=F32)
                 + jnp.dot(h_lo, w_hi, preferred_element_type=F32))) + bias
    lane = lax.broadcasted_iota(jnp.int32, logits.shape, 1).astype(F32)

    def first_max(vals):
        m = jnp.max(vals, axis=1, keepdims=True)
        idx = jnp.min(jnp.where(vals == m, lane, float(LANES)), axis=1, keepdims=True)
        return m, idx

    gmask = lane < n_groups
    gmax, gsel = first_max(jnp.where(gmask, logits, NEG_BIG))
    gsum = jnp.sum(jnp.where(gmask, jnp.exp(logits - gmax), 0.0), axis=1, keepdims=True)
    p_group = 1.0 / gsum
    lo = n_groups + per_group * gsel
    el = jnp.where(jnp.logical_and(lane >= lo, lane < lo + per_group), logits, NEG_BIG)
    m1, i1 = first_max(el)
    m2, i2 = first_max(jnp.where(lane == i1, NEG_BIG, el))
    ratio = jnp.exp(m2 - m1)
    w1 = p_group / (1.0 + ratio)
    w2 = w1 * ratio
    sentinel = float(n_groups * per_group)
    e1 = jnp.where(real, i1 - n_groups, sentinel)
    e2 = jnp.where(real, i2 - n_groups, sentinel)
    w1 = jnp.where(real, w1, 0.0)
    w2 = jnp.where(real, w2, 0.0)
    hit1 = jnp.logical_and(lane == i1, real)
    hit2 = jnp.logical_and(lane == i2, real)
    onehot = jnp.where(jnp.logical_or(hit1, hit2), 1.0, 0.0)
    li = lax.broadcasted_iota(jnp.int32, (bm, bm), 0)
    si = lax.broadcasted_iota(jnp.int32, (bm, bm), 1)
    before = jnp.where(li > si, 1.0, 0.0).astype(BF16)
    seen = jnp.dot(before, onehot.astype(BF16), preferred_element_type=F32) + run[...]
    r1 = jnp.sum(jnp.where(hit1, seen, 0.0), axis=1, keepdims=True)
    r2 = jnp.sum(jnp.where(hit2, seen, 0.0), axis=1, keepdims=True)
    run[...] = run[...] + jnp.sum(onehot, axis=0, keepdims=True)
    out = jnp.zeros(logits.shape, F32)
    for k, v in enumerate((w1, w2, e1, e2, r1, r2)):
        out = jnp.where(lane == k, v, out)
    return out


def _moe_kernel(be_ref, nxt1_ref, nxt2_ref, par_ref, tot_ref, x_ref, wg_hbm, wu_hbm, wd_hbm, o_ref,
                wg_f, wu_f, wd_f, wg_s, wu_s, wd_s, sems, *, layer):
    i = pl.program_id(0)
    total = tot_ref[0]
    ic = jnp.maximum(jnp.minimum(i, total - 1), 0)
    e = be_ref[ic]
    e_prev = be_ref[jnp.maximum(ic - 1, 0)]
    active = i < total
    first_of_run = jnp.logical_and(active, jnp.logical_or(i == 0, e != e_prev))
    par = par_ref[ic]

    def weight_copies(expert, st):
        return (pltpu.make_async_copy(wg_hbm.at[layer, expert], wg_f.at[st], sems.at[st, 0]),
                pltpu.make_async_copy(wu_hbm.at[layer, expert], wu_f.at[st], sems.at[st, 1]),
                pltpu.make_async_copy(wd_hbm.at[layer, expert], wd_f.at[st], sems.at[st, 2]))

    @pl.when(jnp.logical_and(active, i == 0))
    def _():
        for cp in weight_copies(e, 0):
            cp.start()
        nxt1 = nxt1_ref[ic]

        @pl.when(nxt1 >= 0)
        def _():
            for cp in weight_copies(nxt1, 1):
                cp.start()

    def ffn(cast_from):
        x = _unpack_halves(x_ref[...]).astype(BF16)
        if cast_from is not None:
            wg_s[...] = wg_f[cast_from].astype(BF16)
        gte = jnp.dot(x, wg_s[...], preferred_element_type=F32)
        if cast_from is not None:
            wu_s[...] = wu_f[cast_from].astype(BF16)
        up = jnp.dot(x, wu_s[...], preferred_element_type=F32)
        if cast_from is not None:
            wd_s[...] = wd_f[cast_from].astype(BF16)
        hid = (gte * _sigmoid(gte)) * up
        o_ref[...] = _pack_halves(jnp.dot(hid.astype(BF16), wd_s[...], preferred_element_type=F32))

    @pl.when(first_of_run)
    def _():
        for cp in weight_copies(e, par):
            cp.wait()
        ffn(par)
        nxt2 = nxt2_ref[ic]

        @pl.when(nxt2 >= 0)
        def _():
            for cp in weight_copies(nxt2, par):
                cp.start()

    @pl.when(jnp.logical_and(active, jnp.logical_not(first_of_run)))
    def _():
        ffn(None)

    @pl.when(jnp.logical_not(active))
    def _():
        o_ref[...] = jnp.zeros(o_ref.shape, F32)


def _moe_ffn(xg, plan, w_gate, w_up, w_down, layer):
    n_slots, d = xg.shape
    hid = w_gate.shape[-1]
    assert w_gate.shape[-2] == 2 * d
    n_blocks = n_slots // MOE_BLOCK

    def rows(i, be, nxt1, nxt2, par, tot):
        return (jnp.maximum(jnp.minimum(i, tot[0] - 1), 0), 0)

    hbm = pl.BlockSpec(memory_space=pl.ANY)
    grid_spec = pltpu.PrefetchScalarGridSpec(
        num_scalar_prefetch=5,
        grid=(n_blocks,),
        in_specs=[pl.BlockSpec((MOE_BLOCK, d), rows), hbm, hbm, hbm],
        out_specs=pl.BlockSpec((MOE_BLOCK, d), lambda i, *_: (i, 0)),
        scratch_shapes=[pltpu.VMEM((2, 2 * d, hid), F32), pltpu.VMEM((2, 2 * d, hid), F32),
                        pltpu.VMEM((2, hid, 2 * d), F32),
                        pltpu.VMEM((2 * d, hid), BF16), pltpu.VMEM((2 * d, hid), BF16),
                        pltpu.VMEM((hid, 2 * d), BF16), pltpu.SemaphoreType.DMA((2, 3))],
    )
    return pl.pallas_call(
        functools.partial(_moe_kernel, layer=layer),
        grid_spec=grid_spec,
        out_shape=jax.ShapeDtypeStruct((n_slots, d), F32),
        compiler_params=_cparams(("arbitrary",), 58),
        name="moe_ffn",
    )(*plan, xg, w_gate, w_up, w_down)


def _dispatch(ew_t, cnt, n_groups, n_experts):
    t = ew_t.shape[1]
    eid = ew_t[2:4].astype(jnp.int32).reshape(-1)
    rank = ew_t[4:6].astype(jnp.int32).reshape(-1)
    counts = cnt[0, n_groups:n_groups + n_experts].astype(jnp.int32)
    n_assign = eid.shape[0]
    tok = jnp.tile(jnp.arange(t, dtype=jnp.int32), 2)
    padded = (counts + MOE_BLOCK - 1) // MOE_BLOCK * MOE_BLOCK
    pend = jnp.cumsum(padded)
    pstart = pend - padded
    n_blocks = n_assign // MOE_BLOCK + n_experts
    n_slots = n_blocks * MOE_BLOCK
    valid = eid < n_experts
    slot = jnp.where(valid, pstart[jnp.minimum(eid, n_experts - 1)] + rank, n_slots)
    slot_tok = (jnp.arange(n_slots, dtype=jnp.int32) % t).at[slot].set(tok, mode="drop")
    block_expert = jnp.minimum(
        jnp.searchsorted(pend, jnp.arange(n_blocks, dtype=jnp.int32) * MOE_BLOCK, side="right"),
        n_experts - 1).astype(jnp.int32)
    total_blocks = (pend[-1] // MOE_BLOCK).astype(jnp.int32).reshape(1)
    ids = jnp.arange(n_experts, dtype=jnp.int32)
    has = counts > 0
    later = jnp.where(has, ids, n_experts)
    nxt1 = lax.cummin(jnp.concatenate([later[1:], jnp.full((1,), n_experts, jnp.int32)]), reverse=True)
    nxt2 = jnp.concatenate([nxt1, jnp.full((1,), n_experts, jnp.int32)])[nxt1]
    nxt1 = jnp.where(nxt1 >= n_experts, -1, nxt1)
    nxt2 = jnp.where(nxt2 >= n_experts, -1, nxt2)
    parity = (jnp.cumsum(has.astype(jnp.int32)) - 1) % 2
    plan = (block_expert, nxt1[block_expert], nxt2[block_expert], parity[block_expert], total_blocks)
    slot_c = jnp.where(valid, slot, 0).reshape(2, t)
    return slot_tok, plan, slot_c


def kernel(x, meta_tokens, ln_emb_g, ln_emb_b, w_in, ssd_conv_w, ssd_conv_b, ssd_dt_bias, ssd_a_log, ssd_d,
           ssd_norm_g, ssd_proj, pool_w, pool_scale, pool_proj, conf_dw_w, conf_dw_b, conf_ln_g, conf_ln_b,
           conf_proj, gate_w, gate_b, w_out, ln1_g, ln1_b, router_group_w, router_group_b, router_expert_w,
           router_expert_b, exp_w_gate, exp_w_up, exp_w_down, ln2_g, ln2_b):
    nb, seq, d = x.shape
    n_meta = meta_tokens.shape[0]
    depth = w_in.shape[0]
    lp = PAD_ROWS + n_meta + seq
    t = nb * lp
    alpha = (2.0 * depth) ** 0.25
    d_inner = ssd_norm_g.shape[-1]
    xbc_w = ssd_conv_w.shape[-1]
    n_heads = ssd_a_log.shape[-1]
    pool_width = pool_scale.shape[-1]
    conf_width = conf_dw_b.shape[-1]
    n_groups = router_group_w.shape[-1]
    n_experts = router_expert_w.shape[-1]
    per_group = n_experts // n_groups
    assert (PAD_ROWS + n_meta) % SSD_CHUNK == 0 and lp % SSD_CHUNK == 0
    assert n_groups + n_experts <= LANES and n_heads <= LANES

    meta = jnp.broadcast_to(meta_tokens[None].astype(x.dtype), (nb, n_meta, d))
    xcat = jnp.concatenate([jnp.zeros((nb, PAD_ROWS, d), x.dtype), meta, x], axis=1).reshape(t, d)
    h, hb = _embed_ln(xcat, ln_emb_g, ln_emb_b, lp)

    c_dt = d_inner + xbc_w
    c_pc = c_dt + n_heads
    assert c_dt % LANES == 0 and c_dt + LANES <= w_in.shape[-1]
    w_in_t = jnp.swapaxes(w_in, 1, 2)
    rw = jnp.concatenate([router_group_w, router_expert_w], axis=-1)
    rw = jnp.pad(rw, ((0, 0), (0, 0), (0, LANES - rw.shape[-1])))
    rb = jnp.concatenate([router_group_b, router_expert_b], axis=-1)
    rb = jnp.pad(rb, ((0, 0), (0, LANES - rb.shape[-1]))).reshape(depth, 1, LANES)

    for i in range(depth):
        z = _mm(hb, w_in_t, i, 0, d_inner)
        xbc = _mm(hb, w_in_t, i, d_inner, xbc_w)
        dtr = _mm(hb, w_in_t, i, c_dt, LANES)
        pc = _mm(hb, w_in_t, i, c_pc, pool_width + 2 * conf_width)
        f_ssd = _ssd(z, xbc, dtr, ssd_conv_w[i], ssd_conv_b[i], ssd_dt_bias[i], ssd_a_log[i], ssd_d[i],
                     ssd_norm_g[i], nb, lp)
        f_pool = _pool(pc, pool_w[i], pool_scale[i], nb, lp)
        assert pool_width % conf_width == 0
        f_conf = _conf(pc, pool_width // conf_width, conf_dw_w[i], conf_dw_b[i], conf_ln_g[i], conf_ln_b[i], nb, lp)
        merged = _merge(hb, f_ssd, f_pool, f_conf, gate_w, gate_b, ssd_proj, pool_proj, conf_proj, i)
        h, h_packed, ew, ew_t, cnt = _mm_ln(merged, w_out, i, h, ln1_g[i], ln1_b[i], rw[i], rb[i], lp, alpha,
                                            n_groups, per_group)
        slot_tok, plan, slot_c = _dispatch(ew_t, cnt, n_groups, n_experts)
        xg = h_packed.at[slot_tok].get(mode="promise_in_bounds")
        out = _moe_ffn(xg, plan, exp_w_gate, exp_w_up, exp_w_down, i)
        g0 = out.at[slot_c[0]].get(mode="promise_in_bounds")
        g1 = out.at[slot_c[1]].get(mode="promise_in_bounds")
        if i + 1 < depth:
            h, hb = _combine_ln(h, g0, g1, ew, ln2_g[i], ln2_b[i], lp, alpha)
        else:
            h = _final_ln(h, g0, g1, ew, ln2_g[i], ln2_b[i], nb, lp, seq, alpha)

    return h.reshape(nb, seq, d)
```

```python
import functools

import jax
import jax.numpy as jnp
from jax import lax
from jax.experimental import pallas as pl
from jax.experimental.pallas import tpu as pltpu

F32 = jnp.float32
BF16 = jnp.bfloat16

PAD_ROWS = 112
SSD_CHUNK = 128
SSD_GROUPS = 4
SSD_HEAD_DIM = 64
POOL_WINDOWS = (2, 4, 8, 16)
MOE_BLOCK = 256
LN_EPS = 1e-5
LANES = 128
NEG_BIG = -1e30
LOG2E = 1.4426950408889634


def _cparams(sem, vmem_mb):
    return pltpu.CompilerParams(dimension_semantics=sem, vmem_limit_bytes=vmem_mb << 20)


def _sigmoid(x):
    return 1.0 / (1.0 + jnp.exp(-x))


def _softplus(x):
    return jnp.maximum(x, 0.0) + jnp.log1p(jnp.exp(-jnp.abs(x)))


def _ln(x, g, b):
    mu = jnp.mean(x, -1, keepdims=True)
    xc = x - mu
    var = jnp.mean(xc * xc, -1, keepdims=True)
    return xc * lax.rsqrt(var + LN_EPS) * g + b


def _pack_halves(y):
    n = y.shape[1] // 2
    lo = lax.bitcast_convert_type(y[:, :n].astype(BF16).astype(F32), jnp.uint32)
    hi = lax.bitcast_convert_type(y[:, n:].astype(BF16).astype(F32), jnp.uint32)
    return lax.bitcast_convert_type(hi | (lo >> 16), F32)


def _unpack_halves(p):
    u = lax.bitcast_convert_type(p, jnp.uint32)
    lo = lax.bitcast_convert_type(u << 16, F32)
    hi = lax.bitcast_convert_type(u & jnp.uint32(0xFFFF0000), F32)
    return jnp.concatenate([lo, hi], axis=1)


def _rows_in_batch(blk, bm, lp):
    base = lax.rem(blk, lp // bm) * bm
    return base + lax.broadcasted_iota(jnp.int32, (bm, 1), 0)


def _pick(n, cands):
    for c in cands:
        if n % c == 0:
            return c
    raise ValueError(f"no block size for {n} among {cands}")


def _embed_kernel(x_ref, meta_ref, g_ref, b_ref, h_ref, hb_ref):
    j = pl.program_id(1)
    head, d = h_ref.shape
    n_meta = meta_ref.shape[0]

    @pl.when(j == 0)
    def _():
        y = _ln(meta_ref[...], g_ref[...], b_ref[...])
        h_ref[0:head - n_meta, :] = jnp.zeros((head - n_meta, d), F32)
        hb_ref[0:head - n_meta, :] = jnp.zeros((head - n_meta, d), BF16)
        h_ref[head - n_meta:head, :] = y
        hb_ref[head - n_meta:head, :] = y.astype(BF16)

    @pl.when(j > 0)
    def _():
        y = _ln(x_ref[...], g_ref[...], b_ref[...])
        h_ref[...] = y
        hb_ref[...] = y.astype(BF16)


def _embed_ln(x, meta, g, b, lp):
    nb, seq, d = x.shape
    n_meta = meta.shape[0]
    head = lp - seq
    assert head == PAD_ROWS + n_meta and seq % head == 0 and n_meta % 16 == 0
    per = lp // head
    vec = pl.BlockSpec((1, d), lambda bi, j: (0, 0))
    row = pl.BlockSpec((head, d), lambda bi, j: (bi * per + j, 0))
    return pl.pallas_call(
        _embed_kernel,
        grid=(nb, per),
        in_specs=[pl.BlockSpec((None, head, d), lambda bi, j: (bi, jnp.maximum(j - 1, 0), 0)),
                  pl.BlockSpec((n_meta, d), lambda bi, j: (0, 0)), vec, vec],
        out_specs=[row, row],
        out_shape=[jax.ShapeDtypeStruct((nb * lp, d), F32), jax.ShapeDtypeStruct((nb * lp, d), BF16)],
        compiler_params=_cparams(("parallel", "arbitrary"), 32),
        name="embed_ln",
    )(x, meta, g.reshape(1, d), b.reshape(1, d))


def _combine_ln_kernel(h_ref, g0_ref, g1_ref, ew_ref, g_ref, b_ref, o_ref, ob_ref, *, bm, lp, alpha):
    y = ew_ref[:, 0:1] * _unpack_halves(g0_ref[...]) + ew_ref[:, 1:2] * _unpack_halves(g1_ref[...])
    y = _ln(alpha * h_ref[...] + y, g_ref[...], b_ref[...])
    y = jnp.where(_rows_in_batch(pl.program_id(0), bm, lp) >= PAD_ROWS, y, 0.0)
    o_ref[...] = y
    ob_ref[...] = y.astype(BF16)


def _combine_ln(h, g0, g1, ew, g, b, lp, alpha):
    t, d = h.shape
    bm = _pick(lp, (528, 384, 192, 128, 64, 8))
    row = pl.BlockSpec((bm, d), lambda i: (i, 0))
    half = pl.BlockSpec((bm, d // 2), lambda i: (i, 0))
    vec = pl.BlockSpec((1, d), lambda i: (0, 0))
    return pl.pallas_call(
        functools.partial(_combine_ln_kernel, bm=bm, lp=lp, alpha=alpha),
        grid=(t // bm,),
        in_specs=[row, half, half, pl.BlockSpec((bm, LANES), lambda i: (i, 0)), vec, vec],
        out_specs=[row, row],
        out_shape=[jax.ShapeDtypeStruct((t, d), F32), jax.ShapeDtypeStruct((t, d), BF16)],
        compiler_params=_cparams(("parallel",), 56),
        name="combine_ln",
    )(h, g0, g1, ew, g.reshape(1, d), b.reshape(1, d))


def _final_ln_kernel(h_ref, g0_ref, g1_ref, ew_ref, g_ref, b_ref, o_ref, *, alpha):
    y = ew_ref[:, 0:1] * _unpack_halves(g0_ref[...]) + ew_ref[:, 1:2] * _unpack_halves(g1_ref[...])
    o_ref[...] = _ln(alpha * h_ref[...] + y, g_ref[...], b_ref[...])


def _final_ln(h, g0, g1, ew, g, b, nb, lp, seq, alpha):
    d = h.shape[1]
    head = lp - seq
    bm = _pick(head, (128, 64, 8))
    assert seq % bm == 0
    per_seq, per_lp, skip = seq // bm, lp // bm, head // bm
    src = lambda i: ((i // per_seq) * per_lp + skip + i % per_seq, 0)
    row = pl.BlockSpec((bm, d), src)
    half = pl.BlockSpec((bm, d // 2), src)
    vec = pl.BlockSpec((1, d), lambda i: (0, 0))
    return pl.pallas_call(
        functools.partial(_final_ln_kernel, alpha=alpha),
        grid=(nb * per_seq,),
        in_specs=[row, half, half, pl.BlockSpec((bm, LANES), src), vec, vec],
        out_specs=pl.BlockSpec((bm, d), lambda i: (i, 0)),
        out_shape=jax.ShapeDtypeStruct((nb * seq, d), F32),
        compiler_params=_cparams(("parallel",), 32),
        name="final_ln",
    )(h, g0, g1, ew, g.reshape(1, d), b.reshape(1, d))


def _mm_kernel(x_ref, w_ref, *rest, shift):
    if shift:
        wnext_ref, o_ref, wbf_ref = rest
    else:
        o_ref, wbf_ref = rest
    bn = w_ref.shape[0]
    step = min(bn, 256)

    @pl.when(pl.program_id(1) == 0)
    def _():
        for c in range(bn // step):
            lo, hi = shift + c * step, shift + (c + 1) * step
            if hi <= bn:
                rows = w_ref[lo:hi, :]
            else:
                rows = jnp.concatenate([w_ref[lo:bn, :], wnext_ref[0:hi - bn, :]], axis=0)
            wbf_ref[:, c * step:(c + 1) * step] = rows.T.astype(BF16)

    o_ref[...] = jnp.dot(x_ref[...], wbf_ref[...], preferred_element_type=F32).astype(o_ref.dtype)


def _mm(x, w_t, layer, col0, n_cols, out_dtype=F32):
    t, k = x.shape
    bn = _pick(n_cols, (1024, 512, 256, 128))
    bm = _pick(t, (1056, 528, 384, 192, 128, 64, 8))
    shift = col0 % LANES
    base = col0 - shift
    assert base % bn == 0 and shift % 8 == 0
    cb0 = base // bn
    in_specs = [pl.BlockSpec((bm, k), lambda n, m: (m, 0)),
                pl.BlockSpec((None, bn, k), lambda n, m: (layer, cb0 + n, 0))]
    args = [x, w_t]
    if shift:
        tiles = bn // LANES
        in_specs.append(pl.BlockSpec((None, LANES, k), lambda n, m: (layer, (cb0 + n + 1) * tiles, 0)))
        args.append(w_t)
    return pl.pallas_call(
        functools.partial(_mm_kernel, shift=shift),
        grid=(n_cols // bn, t // bm),
        in_specs=in_specs,
        out_specs=pl.BlockSpec((bm, bn), lambda n, m: (m, n)),
        out_shape=jax.ShapeDtypeStruct((t, n_cols), out_dtype),
        scratch_shapes=[pltpu.VMEM((k, bn), BF16)],
        compiler_params=_cparams(("arbitrary", "arbitrary"), 48),
        name="mm",
    )(*args)


def _mm_ln_kernel(x_ref, w_ref, h_ref, g_ref, b_ref, rw_ref, rb_ref, o_ref, op_ref, ew_ref, ewt_ref, cnt_ref,
                  wbf_ref, run, *, bm, lp, alpha, n_groups, per_group):
    @pl.when(pl.program_id(0) == 0)
    def _():
        wbf_ref[...] = w_ref[...].astype(BF16)
        run[...] = jnp.zeros(run.shape, F32)

    y = jnp.dot(x_ref[...], wbf_ref[...], preferred_element_type=F32)
    y = _ln(alpha * h_ref[...] + y, g_ref[...], b_ref[...])
    real = _rows_in_batch(pl.program_id(0), bm, lp) >= PAD_ROWS
    y = jnp.where(real, y, 0.0)
    o_ref[...] = y
    op_ref[...] = _pack_halves(y)
    routed = _route_rows(y, rw_ref[...], rb_ref[...], run, real, n_groups, per_group)
    ew_ref[...] = routed
    ewt_ref[...] = routed.T[0:8, :]
    cnt_ref[...] = run[...]


def _mm_ln(x, w, layer, h, g, b, rw, rb, lp, alpha, n_groups, per_group):
    t, k = x.shape
    d = w.shape[-1]
    bm = _pick(lp, (384, 128))
    row = lambda width: pl.BlockSpec((bm, width), lambda i: (i, 0))
    vec = pl.BlockSpec((1, d), lambda i: (0, 0))
    lanes = pl.BlockSpec((1, LANES), lambda i: (0, 0))
    return pl.pallas_call(
        functools.partial(_mm_ln_kernel, bm=bm, lp=lp, alpha=alpha, n_groups=n_groups, per_group=per_group),
        grid=(t // bm,),
        in_specs=[row(k), pl.BlockSpec((None, k, d), lambda i: (layer, 0, 0), pipeline_mode=pl.Buffered(1)),
                  row(d), vec, vec, pl.BlockSpec((d, LANES), lambda i: (0, 0)), lanes],
        out_specs=[row(d), row(d // 2), row(LANES), pl.BlockSpec((8, bm), lambda i: (0, i)), lanes],
        out_shape=[jax.ShapeDtypeStruct((t, d), F32), jax.ShapeDtypeStruct((t, d // 2), F32),
                   jax.ShapeDtypeStruct((t, LANES), F32), jax.ShapeDtypeStruct((8, t), F32),
                   jax.ShapeDtypeStruct((1, LANES), F32)],
        scratch_shapes=[pltpu.VMEM((k, d), BF16), pltpu.VMEM((1, LANES), F32)],
        compiler_params=_cparams(("arbitrary",), 56),
        name="mm_ln",
    )(x, w, h, g.reshape(1, d), b.reshape(1, d), rw, rb)


def _ssd_kernel(z_ref, xbc_ref, dtr_ref, cw_ref, cb_ref, dtb_ref, alog_ref, dexp_ref, ng_ref, o_ref,
                cbuf, xc, ybuf, st, *, q, d_inner, n_state):
    c = pl.program_id(1)
    xbc_w = cbuf.shape[1]
    kc = cw_ref.shape[0]
    hist = 8

    @pl.when(c == 0)
    def _():
        cbuf[0:hist, :] = jnp.zeros((hist, xbc_w), F32)
        st[...] = jnp.zeros(st.shape, F32)

    cbuf[hist:hist + q, :] = xbc_ref[...]
    cw = 128
    for j in range(xbc_w // cw):
        sl = slice(j * cw, (j + 1) * cw)
        conv = cb_ref[:, sl] + cw_ref[kc - 1:kc, sl] * cbuf[hist:hist + q, sl]
        for k in range(kc - 1):
            off = hist - (kc - 1) + k
            conv = conv + cw_ref[k:k + 1, sl] * cbuf[off:off + q, sl]
        xc[:, sl] = conv * _sigmoid(conv)
    cbuf[0:hist, :] = cbuf[q:q + hist, :]

    row = c * q + lax.broadcasted_iota(jnp.int32, (q, 1), 0)
    dt = _softplus(dtr_ref[...] + dtb_ref[...])
    dt = jnp.where(row >= PAD_ROWS, dt, 0.0)
    adt = dt * (-jnp.exp(alog_ref[...]))
    li = lax.broadcasted_iota(jnp.int32, (q, q), 0)
    si = lax.broadcasted_iota(jnp.int32, (q, q), 1)
    tri = li >= si
    acs = jnp.dot(tri.astype(F32), adt, preferred_element_type=F32,
                  precision=lax.Precision.HIGHEST) * LOG2E
    acs_t = acs.T
    dt_t = dt.T
    e_acs = jnp.exp2(acs)
    last = acs_t[:, q - 1:q]
    w_t = jnp.exp2(last - acs_t) * dt_t
    src_t = acs_t - jnp.log2(dt_t)
    dlast = jnp.exp2(last)
    lane = lax.broadcasted_iota(jnp.int32, (1, LANES), 1)
    lo_half = lane < SSD_HEAD_DIM

    heads_per_group = d_inner // SSD_HEAD_DIM // SSD_GROUPS
    pairs_per_group = heads_per_group // 2
    for g in range(SSD_GROUPS):
        b0 = d_inner + g * n_state
        c0 = d_inner + SSD_GROUPS * n_state + g * n_state
        bg = xc[:, b0:b0 + n_state]
        cg = xc[:, c0:c0 + n_state]
        cbm = lax.dot_general(cg.astype(BF16), bg.astype(BF16), (((1,), (1,)), ((), ())),
                              preferred_element_type=F32)
        bg_t = bg.T
        for j in range(pairs_per_group):
            pair = g * pairs_per_group + j
            h0 = 2 * pair
            xs_pair = xc[:, h0 * SSD_HEAD_DIM:h0 * SSD_HEAD_DIM + LANES]
            st_pair = st[pair]
            lhs_y, lhs_s, rhs_x, rhs_st = [], [], [], []
            for u in range(2):
                h = h0 + u
                col = acs[:, h:h + 1]
                rw = src_t[h:h + 1, :]
                decay = jnp.where(tri, jnp.exp2(col - rw), 0.0)
                lhs_y.append((cbm * decay).astype(BF16))
                lhs_s.append((bg_t * w_t[h:h + 1, :]).astype(BF16))
                keep = lo_half if u == 0 else jnp.logical_not(lo_half)
                rhs_x.append(jnp.where(keep, xs_pair, 0.0).astype(BF16))
                rhs_st.append(jnp.where(keep, st_pair, 0.0).astype(BF16))
            for u in range(2):
                lhs_y.append((e_acs[:, h0 + u:h0 + u + 1] * cg).astype(BF16))
            y_pair = jnp.dot(jnp.concatenate(lhs_y, axis=1), jnp.concatenate(rhs_x + rhs_st, axis=0),
                             preferred_element_type=F32)
            s_new = jnp.dot(jnp.concatenate(lhs_s, axis=1), jnp.concatenate(rhs_x, axis=0),
                            preferred_element_type=F32)
            dpair = jnp.where(lo_half, dlast[h0:h0 + 1, :], dlast[h0 + 1:h0 + 2, :])
            st[pair] = st_pair * dpair + s_new
            ybuf[:, h0 * SSD_HEAD_DIM:h0 * SSD_HEAD_DIM + LANES] = y_pair

    gw = d_inner // SSD_GROUPS
    for g in range(SSD_GROUPS):
        sl = slice(g * gw, (g + 1) * gw)
        y = ybuf[:, sl] + xc[:, sl] * dexp_ref[:, sl]
        zz = z_ref[:, sl]
        y = y * (zz * _sigmoid(zz))
        y = y * lax.rsqrt(jnp.mean(y * y, -1, keepdims=True) + LN_EPS)
        o_ref[:, sl] = (y * ng_ref[:, sl]).astype(BF16)


def _ssd(z, xbc, dtr, conv_w, conv_b, dt_bias, a_log, d_skip, norm_g, nb, lp):
    t, d_inner = z.shape
    xbc_w = xbc.shape[1]
    q = SSD_CHUNK
    n_heads = d_inner // SSD_HEAD_DIM
    n_state = (xbc_w - d_inner) // (2 * SSD_GROUPS)
    assert n_state == LANES and n_heads % (2 * SSD_GROUPS) == 0 and lp % q == 0
    assert xbc_w % 512 == 0 and PAD_ROWS % 8 == 0
    nc = lp // q
    padh = LANES - n_heads
    row = lambda w: pl.BlockSpec((q, w), lambda b, c: (b * nc + c, 0))
    full = lambda a: pl.BlockSpec(a.shape, lambda b, c: (0,) * a.ndim)
    dtb = jnp.pad(dt_bias, (0, padh)).reshape(1, LANES)
    alog = jnp.pad(a_log, (0, padh)).reshape(1, LANES)
    dexp = jnp.repeat(d_skip, SSD_HEAD_DIM).reshape(1, d_inner)
    cb = conv_b.reshape(1, xbc_w)
    ng = norm_g.reshape(1, d_inner)
    return pl.pallas_call(
        functools.partial(_ssd_kernel, q=q, d_inner=d_inner, n_state=n_state),
        grid=(nb, nc),
        in_specs=[row(d_inner), row(xbc_w), row(LANES), full(conv_w), full(cb), full(dtb), full(alog),
                  full(dexp), full(ng)],
        out_specs=row(d_inner),
        out_shape=jax.ShapeDtypeStruct((t, d_inner), BF16),
        scratch_shapes=[pltpu.VMEM((8 + q, xbc_w), F32), pltpu.VMEM((q, xbc_w), F32),
                        pltpu.VMEM((q, d_inner), F32), pltpu.VMEM((n_heads // 2, n_state, LANES), F32)],
        compiler_params=_cparams(("arbitrary", "arbitrary"), 32),
        name="ssd",
    )(z, xbc, dtr, conv_w, cb, dtb, alog, dexp, ng)


def _pool_kernel(u_ref, w_ref, sc_ref, o_ref, buf, *, r, gd):
    c = pl.program_id(1)
    hist = 16
    width = buf.shape[1]

    @pl.when(c == 0)
    def _():
        buf[0:hist, :] = jnp.zeros((hist, width), F32)

    buf[hist:hist + r, :] = u_ref[...]
    pos = c * r + lax.broadcasted_iota(jnp.int32, (r, 1), 0) - PAD_ROWS
    for gi, win in enumerate(POOL_WINDOWS):
        sl = slice(gi * gd, (gi + 1) * gd)
        u = buf[hist:hist + r, sl]
        s = u
        for j in range(1, win):
            s = s + buf[hist - j:hist - j + r, sl]
        cnt = jnp.clip(pos + 1, 1, win).astype(F32)
        mixed = s / cnt - u
        y = jnp.dot(mixed.astype(BF16), w_ref[gi].astype(BF16), preferred_element_type=F32)
        o_ref[:, sl] = (y * sc_ref[:, sl]).astype(BF16)
    buf[0:hist, :] = buf[r:r + hist, :]


def _pool(pc, pool_w, scale, nb, lp):
    t = pc.shape[0]
    ng, gd, _ = pool_w.shape
    width = ng * gd
    assert ng == len(POOL_WINDOWS) and gd % LANES == 0
    r = _pick(lp, (384, 192, 128, 64))
    nc = lp // r
    return pl.pallas_call(
        functools.partial(_pool_kernel, r=r, gd=gd),
        grid=(nb, nc),
        in_specs=[pl.BlockSpec((r, width), lambda b, c: (b * nc + c, 0)),
                  pl.BlockSpec(pool_w.shape, lambda b, c: (0, 0, 0)),
                  pl.BlockSpec((1, width), lambda b, c: (0, 0))],
        out_specs=pl.BlockSpec((r, width), lambda b, c: (b * nc + c, 0)),
        out_shape=jax.ShapeDtypeStruct((t, width), BF16),
        scratch_shapes=[pltpu.VMEM((16 + r, width), F32)],
        compiler_params=_cparams(("arbitrary", "arbitrary"), 32),
        name="pool",
    )(pc, pool_w, scale.reshape(1, width))


def _conf_kernel(a_ref, g_ref, w_ref, b_ref, lg_ref, lb_ref, o_ref, vbuf, cv, ubuf, *, r):
    c = pl.program_id(1)
    hist = 32
    sub = 8
    width = vbuf.shape[1]
    kc = w_ref.shape[0]

    @pl.when(c == 0)
    def _():
        vbuf[0:hist, :] = jnp.zeros((hist, width), F32)

    vbuf[hist:hist + r, :] = a_ref[...] * _sigmoid(g_ref[...])
    ext = r + sub
    cw = ubuf.shape[2]
    for jc in range(width // cw):
        sl = slice(jc * cw, (jc + 1) * cw)
        acc = None
        for rr in range(sub):
            u = None
            for j in range((kc - 1 - rr) // sub + 1):
                k = kc - 1 - (sub * j + rr)
                lo = hist - sub - sub * j
                term = w_ref[k:k + 1, sl] * vbuf[lo:lo + ext, sl]
                u = term if u is None else u + term
            if rr == 0:
                acc = b_ref[:, sl] + u[sub:sub + r]
            else:
                ubuf[rr - 1] = u
                acc = acc + ubuf[rr - 1, sub - rr:sub - rr + r, :]
        cv[:, sl] = acc
    vbuf[0:hist, :] = vbuf[r:r + hist, :]
    v = _ln(cv[...], lg_ref[...], lb_ref[...])
    o_ref[...] = (v * _sigmoid(v)).astype(BF16)


def _conf(pc, col_blk, dw_w, dw_b, ln_g, ln_b, nb, lp):
    t = pc.shape[0]
    kc, width = dw_w.shape
    assert kc <= 33 and width % 256 == 0
    r = _pick(lp, (192, 128, 64))
    nc = lp // r
    vec = lambda: pl.BlockSpec((1, width), lambda b, c: (0, 0))
    return pl.pallas_call(
        functools.partial(_conf_kernel, r=r),
        grid=(nb, nc),
        in_specs=[pl.BlockSpec((r, width), lambda b, c: (b * nc + c, col_blk)),
                  pl.BlockSpec((r, width), lambda b, c: (b * nc + c, col_blk + 1)),
                  pl.BlockSpec((kc, width), lambda b, c: (0, 0)), vec(), vec(), vec()],
        out_specs=pl.BlockSpec((r, width), lambda b, c: (b * nc + c, 0)),
        out_shape=jax.ShapeDtypeStruct((t, width), BF16),
        scratch_shapes=[pltpu.VMEM((32 + r, width), F32), pltpu.VMEM((r, width), F32),
                        pltpu.VMEM((7, r + 8, LANES), F32)],
        compiler_params=_cparams(("arbitrary", "arbitrary"), 32),
        name="conf",
    )(pc, pc, dw_w, dw_b.reshape(1, width), ln_g.reshape(1, width), ln_b.reshape(1, width))


def _merge_kernel(hb_ref, f0_ref, f1_ref, f2_ref, g0_ref, g1_ref, g2_ref, gb_ref, p0_ref, p1_ref, p2_ref,
                  o_ref, gs, s0, s1, s2):
    @pl.when(pl.program_id(1) == 0)
    def _():
        gs[0] = g0_ref[...].astype(BF16)
        gs[1] = g1_ref[...].astype(BF16)
        gs[2] = g2_ref[...].astype(BF16)
        s0[...] = p0_ref[...].astype(BF16)
        s1[...] = p1_ref[...].astype(BF16)
        s2[...] = p2_ref[...].astype(BF16)

    hb = hb_ref[...]
    acc = None
    for j, (f_ref, s_ref) in enumerate(((f0_ref, s0), (f1_ref, s1), (f2_ref, s2))):
        gate = _sigmoid(jnp.dot(hb, gs[j], preferred_element_type=F32) + gb_ref[j:j + 1, :])
        term = gate * jnp.dot(f_ref[...], s_ref[...], preferred_element_type=F32)
        acc = term if acc is None else acc + term
    o_ref[...] = acc.astype(BF16)


def _merge(hb, f_ssd, f_pool, f_conf, gate_w, gate_b, ssd_proj, pool_proj, conf_proj, layer):
    t, d = hb.shape
    bn = _pick(d, (512, 256, 128))
    bm = _pick(t, (528, 384, 192, 128, 64, 8))
    nbn = d // bn
    once = pl.Buffered(1)
    rowspec = lambda a: pl.BlockSpec((bm, a.shape[1]), lambda n, m: (m, 0))
    gspec = lambda j: pl.BlockSpec((None, d, bn), lambda n, m: (layer, 0, j * nbn + n), pipeline_mode=once)
    pspec = lambda w: pl.BlockSpec((None, w.shape[1], bn), lambda n, m: (layer, 0, n), pipeline_mode=once)
    gb = gate_b[layer].reshape(3, d)
    return pl.pallas_call(
        _merge_kernel,
        grid=(nbn, t // bm),
        in_specs=[rowspec(hb), rowspec(f_ssd), rowspec(f_pool), rowspec(f_conf),
                  gspec(0), gspec(1), gspec(2), pl.BlockSpec((3, bn), lambda n, m: (0, n)),
                  pspec(ssd_proj), pspec(pool_proj), pspec(conf_proj)],
        out_specs=pl.BlockSpec((bm, bn), lambda n, m: (m, n)),
        out_shape=jax.ShapeDtypeStruct((t, d), BF16),
        scratch_shapes=[pltpu.VMEM((3, d, bn), BF16), pltpu.VMEM((ssd_proj.shape[1], bn), BF16),
                        pltpu.VMEM((pool_proj.shape[1], bn), BF16), pltpu.VMEM((conf_proj.shape[1], bn), BF16)],
        compiler_params=_cparams(("arbitrary", "arbitrary"), 52),
        name="merge",
    )(hb, f_ssd, f_pool, f_conf, gate_w, gate_w, gate_w, gb, ssd_proj, pool_proj, conf_proj)


def _route_rows(h, w, bias, run, real, n_groups, per_group):
    bm = h.shape[0]
    h_hi = h.astype(BF16)
    h_lo = (h - h_hi.astype(F32)).astype(BF16)
    w_hi = w.astype(BF16)
    w_lo = (w - w_hi.astype(F32)).astype(BF16)
    logits = (jnp.dot(h_hi, w_hi, preferred_element_type=F32)
              + (jnp.dot(h_hi, w_lo, preferred_element_type=F32)
                 + jnp.dot(h_lo, w_hi, preferred_element_type=F32))) + bias
    lane = lax.broadcasted_iota(jnp.int32, logits.shape, 1).astype(F32)

    def first_max(vals):
        m = jnp.max(vals, axis=1, keepdims=True)
        idx = jnp.min(jnp.where(vals == m, lane, float(LANES)), axis=1, keepdims=True)
        return m, idx

    gmask = lane < n_groups
    gmax, gsel = first_max(jnp.where(gmask, logits, NEG_BIG))
    gsum = jnp.sum(jnp.where(gmask, jnp.exp(logits - gmax), 0.0), axis=1, keepdims=True)
    p_group = 1.0 / gsum
    lo = n_groups + per_group * gsel
    el = jnp.where(jnp.logical_and(lane >= lo, lane < lo + per_group), logits, NEG_BIG)
    m1, i1 = first_max(el)
    m2, i2 = first_max(jnp.where(lane == i1, NEG_BIG, el))
    ratio = jnp.exp(m2 - m1)
    w1 = p_group / (1.0 + ratio)
    w2 = w1 * ratio
    sentinel = float(n_groups * per_group)
    e1 = jnp.where(real, i1 - n_groups, sentinel)
    e2 = jnp.where(real, i2 - n_groups, sentinel)
    w1 = jnp.where(real, w1, 0.0)
    w2 = jnp.where(real, w2, 0.0)
    hit1 = jnp.logical_and(lane == i1, real)
    hit2 = jnp.logical_and(lane == i2, real)
    onehot = jnp.where(jnp.logical_or(hit1, hit2), 1.0, 0.0)
    li = lax.broadcasted_iota(jnp.int32, (bm, bm), 0)
    si = lax.broadcasted_iota(jnp.int32, (bm, bm), 1)
    before = jnp.where(li > si, 1.0, 0.0).astype(BF16)
    seen = jnp.dot(before, onehot.astype(BF16), preferred_element_type=F32) + run[...]
    r1 = jnp.sum(jnp.where(hit1, seen, 0.0), axis=1, keepdims=True)
    r2 = jnp.sum(jnp.where(hit2, seen, 0.0), axis=1, keepdims=True)
    run[...] = run[...] + jnp.sum(onehot, axis=0, keepdims=True)
    out = jnp.zeros(logits.shape, F32)
    for k, v in enumerate((w1, w2, e1, e2, r1, r2)):
        out = jnp.where(lane == k, v, out)
    return out


def _moe_kernel(be_ref, nxt1_ref, nxt2_ref, par_ref, tot_ref, x_ref, wg_hbm, wu_hbm, wd_hbm, o_ref,
                wg_f, wu_f, wd_f, wg_s, wu_s, wd_s, sems, *, layer):
    i = pl.program_id(0)
    total = tot_ref[0]
    ic = jnp.maximum(jnp.minimum(i, total - 1), 0)
    e = be_ref[ic]
    e_prev = be_ref[jnp.maximum(ic - 1, 0)]
    active = i < total
    first_of_run = jnp.logical_and(active, jnp.logical_or(i == 0, e != e_prev))
    par = par_ref[ic]

    def weight_copies(expert, st):
        return (pltpu.make_async_copy(wg_hbm.at[layer, expert], wg_f.at[st], sems.at[st, 0]),
                pltpu.make_async_copy(wu_hbm.at[layer, expert], wu_f.at[st], sems.at[st, 1]),
                pltpu.make_async_copy(wd_hbm.at[layer, expert], wd_f.at[st], sems.at[st, 2]))

    @pl.when(jnp.logical_and(active, i == 0))
    def _():
        for cp in weight_copies(e, 0):
            cp.start()
        nxt1 = nxt1_ref[ic]

        @pl.when(nxt1 >= 0)
        def _():
            for cp in weight_copies(nxt1, 1):
                cp.start()

    @pl.when(first_of_run)
    def _():
        for cp, src, dst in zip(weight_copies(e, par), (wg_f, wu_f, wd_f), (wg_s, wu_s, wd_s)):
            cp.wait()
            dst[...] = src[par].astype(BF16)
        nxt2 = nxt2_ref[ic]

        @pl.when(nxt2 >= 0)
        def _():
            for cp in weight_copies(nxt2, par):
                cp.start()

    @pl.when(active)
    def _():
        x = _unpack_halves(x_ref[...]).astype(BF16)
        gte = jnp.dot(x, wg_s[...], preferred_element_type=F32)
        up = jnp.dot(x, wu_s[...], preferred_element_type=F32)
        hid = (gte * _sigmoid(gte)) * up
        o_ref[...] = _pack_halves(jnp.dot(hid.astype(BF16), wd_s[...], preferred_element_type=F32))

    @pl.when(jnp.logical_not(active))
    def _():
        o_ref[...] = jnp.zeros(o_ref.shape, F32)


def _moe_ffn(xg, plan, w_gate, w_up, w_down, layer):
    n_slots, d = xg.shape
    hid = w_gate.shape[-1]
    assert w_gate.shape[-2] == 2 * d
    n_blocks = n_slots // MOE_BLOCK

    def rows(i, be, nxt1, nxt2, par, tot):
        return (jnp.maximum(jnp.minimum(i, tot[0] - 1), 0), 0)

    hbm = pl.BlockSpec(memory_space=pl.ANY)
    grid_spec = pltpu.PrefetchScalarGridSpec(
        num_scalar_prefetch=5,
        grid=(n_blocks,),
        in_specs=[pl.BlockSpec((MOE_BLOCK, d), rows), hbm, hbm, hbm],
        out_specs=pl.BlockSpec((MOE_BLOCK, d), lambda i, *_: (i, 0)),
        scratch_shapes=[pltpu.VMEM((2, 2 * d, hid), F32), pltpu.VMEM((2, 2 * d, hid), F32),
                        pltpu.VMEM((2, hid, 2 * d), F32),
                        pltpu.VMEM((2 * d, hid), BF16), pltpu.VMEM((2 * d, hid), BF16),
                        pltpu.VMEM((hid, 2 * d), BF16), pltpu.SemaphoreType.DMA((2, 3))],
    )
    return pl.pallas_call(
        functools.partial(_moe_kernel, layer=layer),
        grid_spec=grid_spec,
        out_shape=jax.ShapeDtypeStruct((n_slots, d), F32),
        compiler_params=_cparams(("arbitrary",), 58),
        name="moe_ffn",
    )(*plan, xg, w_gate, w_up, w_down)


def _dispatch(ew_t, cnt, n_groups, n_experts):
    t = ew_t.shape[1]
    eid = ew_t[2:4].astype(jnp.int32).reshape(-1)
    rank = ew_t[4:6].astype(jnp.int32).reshape(-1)
    counts = cnt[0, n_groups:n_groups + n_experts].astype(jnp.int32)
    n_assign = eid.shape[0]
    tok = jnp.tile(jnp.arange(t, dtype=jnp.int32), 2)
    padded = (counts + MOE_BLOCK - 1) // MOE_BLOCK * MOE_BLOCK
    pend = jnp.cumsum(padded)
    pstart = pend - padded
    n_blocks = n_assign // MOE_BLOCK + n_experts
    n_slots = n_blocks * MOE_BLOCK
    valid = eid < n_experts
    slot = jnp.where(valid, pstart[jnp.minimum(eid, n_experts - 1)] + rank,
                     n_slots + jnp.arange(n_assign, dtype=jnp.int32))
    slot_tok = (jnp.arange(n_slots, dtype=jnp.int32) % t).at[slot].set(tok, mode="drop", unique_indices=True)
    first_row = jnp.arange(n_blocks, dtype=jnp.int32) * MOE_BLOCK
    block_expert = jnp.minimum(jnp.sum((pend[None, :] <= first_row[:, None]).astype(jnp.int32), axis=1),
                               n_experts - 1)
    total_blocks = (pend[-1] // MOE_BLOCK).astype(jnp.int32).reshape(1)
    ids = jnp.arange(n_experts, dtype=jnp.int32)
    has = counts > 0
    later = jnp.where(has, ids, n_experts)
    nxt1 = lax.cummin(jnp.concatenate([later[1:], jnp.full((1,), n_experts, jnp.int32)]), reverse=True)
    nxt2 = jnp.concatenate([nxt1, jnp.full((1,), n_experts, jnp.int32)])[nxt1]
    nxt1 = jnp.where(nxt1 >= n_experts, -1, nxt1)
    nxt2 = jnp.where(nxt2 >= n_experts, -1, nxt2)
    parity = (jnp.cumsum(has.astype(jnp.int32)) - 1) % 2
    plan = (block_expert, nxt1[block_expert], nxt2[block_expert], parity[block_expert], total_blocks)
    slot_c = jnp.where(valid, slot, 0).reshape(2, t)
    return slot_tok, plan, slot_c


def kernel(x, meta_tokens, ln_emb_g, ln_emb_b, w_in, ssd_conv_w, ssd_conv_b, ssd_dt_bias, ssd_a_log, ssd_d,
           ssd_norm_g, ssd_proj, pool_w, pool_scale, pool_proj, conf_dw_w, conf_dw_b, conf_ln_g, conf_ln_b,
           conf_proj, gate_w, gate_b, w_out, ln1_g, ln1_b, router_group_w, router_group_b, router_expert_w,
           router_expert_b, exp_w_gate, exp_w_up, exp_w_down, ln2_g, ln2_b):
    nb, seq, d = x.shape
    n_meta = meta_tokens.shape[0]
    depth = w_in.shape[0]
    lp = PAD_ROWS + n_meta + seq
    t = nb * lp
    alpha = (2.0 * depth) ** 0.25
    d_inner = ssd_norm_g.shape[-1]
    xbc_w = ssd_conv_w.shape[-1]
    n_heads = ssd_a_log.shape[-1]
    pool_width = pool_scale.shape[-1]
    conf_width = conf_dw_b.shape[-1]
    n_groups = router_group_w.shape[-1]
    n_experts = router_expert_w.shape[-1]
    per_group = n_experts // n_groups
    assert (PAD_ROWS + n_meta) % SSD_CHUNK == 0 and lp % SSD_CHUNK == 0
    assert n_groups + n_experts <= LANES and n_heads <= LANES

    h, hb = _embed_ln(x, meta_tokens.astype(x.dtype), ln_emb_g, ln_emb_b, lp)

    c_dt = d_inner + xbc_w
    c_pc = c_dt + n_heads
    assert c_dt % LANES == 0 and c_dt + LANES <= w_in.shape[-1]
    w_in_t = jnp.swapaxes(w_in, 1, 2)
    rw = jnp.concatenate([router_group_w, router_expert_w], axis=-1)
    rw = jnp.pad(rw, ((0, 0), (0, 0), (0, LANES - rw.shape[-1])))
    rb = jnp.concatenate([router_group_b, router_expert_b], axis=-1)
    rb = jnp.pad(rb, ((0, 0), (0, LANES - rb.shape[-1]))).reshape(depth, 1, LANES)

    for i in range(depth):
        z = _mm(hb, w_in_t, i, 0, d_inner)
        xbc = _mm(hb, w_in_t, i, d_inner, xbc_w)
        dtr = _mm(hb, w_in_t, i, c_dt, LANES)
        pc = _mm(hb, w_in_t, i, c_pc, pool_width + 2 * conf_width)
        f_ssd = _ssd(z, xbc, dtr, ssd_conv_w[i], ssd_conv_b[i], ssd_dt_bias[i], ssd_a_log[i], ssd_d[i],
                     ssd_norm_g[i], nb, lp)
        f_pool = _pool(pc, pool_w[i], pool_scale[i], nb, lp)
        assert pool_width % conf_width == 0
        f_conf = _conf(pc, pool_width // conf_width, conf_dw_w[i], conf_dw_b[i], conf_ln_g[i], conf_ln_b[i], nb, lp)
        merged = _merge(hb, f_ssd, f_pool, f_conf, gate_w, gate_b, ssd_proj, pool_proj, conf_proj, i)
        h, h_packed, ew, ew_t, cnt = _mm_ln(merged, w_out, i, h, ln1_g[i], ln1_b[i], rw[i], rb[i], lp, alpha,
                                            n_groups, per_group)
        slot_tok, plan, slot_c = _dispatch(ew_t, cnt, n_groups, n_experts)
        xg = h_packed.at[slot_tok].get(mode="promise_in_bounds")
        out = _moe_ffn(xg, plan, exp_w_gate, exp_w_up, exp_w_down, i)
        g0 = out.at[slot_c[0]].get(mode="promise_in_bounds")
        g1 = out.at[slot_c[1]].get(mode="promise_in_bounds")
        if i + 1 < depth:
            h, hb = _combine_ln(h, g0, g1, ew, ln2_g[i], ln2_b[i], lp, alpha)
        else:
            h = _final_ln(h, g0, g1, ew, ln2_g[i], ln2_b[i], nb, lp, seq, alpha)

    return h.reshape(nb, seq, d)
```

```python
import functools

import jax
import jax.numpy as jnp
from jax import lax
from jax.experimental import pallas as pl
from jax.experimental.pallas import tpu as pltpu

F32 = jnp.float32
BF16 = jnp.bfloat16

PAD_ROWS = 112
SSD_CHUNK = 128
SSD_GROUPS = 4
SSD_HEAD_DIM = 64
POOL_WINDOWS = (2, 4, 8, 16)
MOE_BLOCK = 256
LN_EPS = 1e-5
LANES = 128
NEG_BIG = -1e30
LOG2E = 1.4426950408889634


def _cparams(sem, vmem_mb):
    return pltpu.CompilerParams(dimension_semantics=sem, vmem_limit_bytes=vmem_mb << 20)


def _sigmoid(x):
    return 1.0 / (1.0 + jnp.exp(-x))


def _softplus(x):
    return jnp.maximum(x, 0.0) + jnp.log1p(jnp.exp(-jnp.abs(x)))


def _ln(x, g, b):
    mu = jnp.mean(x, -1, keepdims=True)
    xc = x - mu
    var = jnp.mean(xc * xc, -1, keepdims=True)
    return xc * lax.rsqrt(var + LN_EPS) * g + b


def _pack_halves(y):
    n = y.shape[1] // 2
    lo = lax.bitcast_convert_type(y[:, :n].astype(BF16).astype(F32), jnp.uint32)
    hi = lax.bitcast_convert_type(y[:, n:].astype(BF16).astype(F32), jnp.uint32)
    return lax.bitcast_convert_type(hi | (lo >> 16), F32)


def _unpack_halves(p):
    u = lax.bitcast_convert_type(p, jnp.uint32)
    lo = lax.bitcast_convert_type(u << 16, F32)
    hi = lax.bitcast_convert_type(u & jnp.uint32(0xFFFF0000), F32)
    return jnp.concatenate([lo, hi], axis=1)


def _rows_in_batch(blk, bm, lp):
    base = lax.rem(blk, lp // bm) * bm
    return base + lax.broadcasted_iota(jnp.int32, (bm, 1), 0)


def _pick(n, cands):
    for c in cands:
        if n % c == 0:
            return c
    raise ValueError(f"no block size for {n} among {cands}")


def _embed_kernel(x_ref, meta_ref, g_ref, b_ref, h_ref, hb_ref):
    j = pl.program_id(1)
    head, d = h_ref.shape
    n_meta = meta_ref.shape[0]

    @pl.when(j == 0)
    def _():
        y = _ln(meta_ref[...], g_ref[...], b_ref[...])
        h_ref[0:head - n_meta, :] = jnp.zeros((head - n_meta, d), F32)
        hb_ref[0:head - n_meta, :] = jnp.zeros((head - n_meta, d), BF16)
        h_ref[head - n_meta:head, :] = y
        hb_ref[head - n_meta:head, :] = y.astype(BF16)

    @pl.when(j > 0)
    def _():
        y = _ln(x_ref[...], g_ref[...], b_ref[...])
        h_ref[...] = y
        hb_ref[...] = y.astype(BF16)


def _embed_ln(x, meta, g, b, lp):
    nb, seq, d = x.shape
    n_meta = meta.shape[0]
    head = lp - seq
    assert head == PAD_ROWS + n_meta and seq % head == 0 and n_meta % 16 == 0
    per = lp // head
    vec = pl.BlockSpec((1, d), lambda bi, j: (0, 0))
    row = pl.BlockSpec((head, d), lambda bi, j: (bi * per + j, 0))
    return pl.pallas_call(
        _embed_kernel,
        grid=(nb, per),
        in_specs=[pl.BlockSpec((None, head, d), lambda bi, j: (bi, jnp.maximum(j - 1, 0), 0)),
                  pl.BlockSpec((n_meta, d), lambda bi, j: (0, 0)), vec, vec],
        out_specs=[row, row],
        out_shape=[jax.ShapeDtypeStruct((nb * lp, d), F32), jax.ShapeDtypeStruct((nb * lp, d), BF16)],
        compiler_params=_cparams(("parallel", "arbitrary"), 32),
        name="embed_ln",
    )(x, meta, g.reshape(1, d), b.reshape(1, d))


def _combine_ln_kernel(h_ref, g0_ref, g1_ref, ew_ref, g_ref, b_ref, o_ref, ob_ref, *, bm, lp, alpha):
    y = ew_ref[:, 0:1] * _unpack_halves(g0_ref[...]) + ew_ref[:, 1:2] * _unpack_halves(g1_ref[...])
    y = _ln(alpha * h_ref[...] + y, g_ref[...], b_ref[...])
    y = jnp.where(_rows_in_batch(pl.program_id(0), bm, lp) >= PAD_ROWS, y, 0.0)
    o_ref[...] = y
    ob_ref[...] = y.astype(BF16)


def _combine_ln(h, g0, g1, ew, g, b, lp, alpha):
    t, d = h.shape
    bm = _pick(lp, (528, 384, 192, 128, 64, 8))
    row = pl.BlockSpec((bm, d), lambda i: (i, 0))
    half = pl.BlockSpec((bm, d // 2), lambda i: (i, 0))
    vec = pl.BlockSpec((1, d), lambda i: (0, 0))
    return pl.pallas_call(
        functools.partial(_combine_ln_kernel, bm=bm, lp=lp, alpha=alpha),
        grid=(t // bm,),
        in_specs=[row, half, half, pl.BlockSpec((bm, LANES), lambda i: (i, 0)), vec, vec],
        out_specs=[row, row],
        out_shape=[jax.ShapeDtypeStruct((t, d), F32), jax.ShapeDtypeStruct((t, d), BF16)],
        compiler_params=_cparams(("parallel",), 56),
        name="combine_ln",
    )(h, g0, g1, ew, g.reshape(1, d), b.reshape(1, d))


def _final_ln_kernel(h_ref, g0_ref, g1_ref, ew_ref, g_ref, b_ref, o_ref, *, alpha):
    y = ew_ref[:, 0:1] * _unpack_halves(g0_ref[...]) + ew_ref[:, 1:2] * _unpack_halves(g1_ref[...])
    o_ref[...] = _ln(alpha * h_ref[...] + y, g_ref[...], b_ref[...])


def _final_ln(h, g0, g1, ew, g, b, nb, lp, seq, alpha):
    d = h.shape[1]
    head = lp - seq
    bm = _pick(head, (128, 64, 8))
    assert seq % bm == 0
    per_seq, per_lp, skip = seq // bm, lp // bm, head // bm
    src = lambda i: ((i // per_seq) * per_lp + skip + i % per_seq, 0)
    row = pl.BlockSpec((bm, d), src)
    half = pl.BlockSpec((bm, d // 2), src)
    vec = pl.BlockSpec((1, d), lambda i: (0, 0))
    return pl.pallas_call(
        functools.partial(_final_ln_kernel, alpha=alpha),
        grid=(nb * per_seq,),
        in_specs=[row, half, half, pl.BlockSpec((bm, LANES), src), vec, vec],
        out_specs=pl.BlockSpec((bm, d), lambda i: (i, 0)),
        out_shape=jax.ShapeDtypeStruct((nb * seq, d), F32),
        compiler_params=_cparams(("parallel",), 32),
        name="final_ln",
    )(h, g0, g1, ew, g.reshape(1, d), b.reshape(1, d))


def _mm_kernel(x_ref, w_ref, *rest, shift):
    if shift:
        wnext_ref, o_ref, wbf_ref = rest
    else:
        o_ref, wbf_ref = rest
    bn = w_ref.shape[0]
    step = min(bn, 256)

    @pl.when(pl.program_id(1) == 0)
    def _():
        for c in range(bn // step):
            lo, hi = shift + c * step, shift + (c + 1) * step
            if hi <= bn:
                rows = w_ref[lo:hi, :]
            else:
                rows = jnp.concatenate([w_ref[lo:bn, :], wnext_ref[0:hi - bn, :]], axis=0)
            wbf_ref[:, c * step:(c + 1) * step] = rows.T.astype(BF16)

    o_ref[...] = jnp.dot(x_ref[...], wbf_ref[...], preferred_element_type=F32).astype(o_ref.dtype)


def _mm(x, w_t, layer, col0, n_cols, out_dtype=F32):
    t, k = x.shape
    bn = _pick(n_cols, (1024, 512, 256, 128))
    bm = _pick(t, (1056, 528, 384, 192, 128, 64, 8))
    shift = col0 % LANES
    base = col0 - shift
    assert base % bn == 0 and shift % 8 == 0
    cb0 = base // bn
    in_specs = [pl.BlockSpec((bm, k), lambda n, m: (m, 0)),
                pl.BlockSpec((None, bn, k), lambda n, m: (layer, cb0 + n, 0))]
    args = [x, w_t]
    if shift:
        tiles = bn // LANES
        in_specs.append(pl.BlockSpec((None, LANES, k), lambda n, m: (layer, (cb0 + n + 1) * tiles, 0)))
        args.append(w_t)
    return pl.pallas_call(
        functools.partial(_mm_kernel, shift=shift),
        grid=(n_cols // bn, t // bm),
        in_specs=in_specs,
        out_specs=pl.BlockSpec((bm, bn), lambda n, m: (m, n)),
        out_shape=jax.ShapeDtypeStruct((t, n_cols), out_dtype),
        scratch_shapes=[pltpu.VMEM((k, bn), BF16)],
        compiler_params=_cparams(("arbitrary", "arbitrary"), 48),
        name="mm",
    )(*args)


def _mm_ln_kernel(x_ref, w_ref, h_ref, g_ref, b_ref, rw_ref, rb_ref, o_ref, op_ref, ew_ref, ewt_ref, cnt_ref,
                  wbf_ref, run, *, bm, lp, alpha, n_groups, per_group):
    @pl.when(pl.program_id(0) == 0)
    def _():
        wbf_ref[...] = w_ref[...].astype(BF16)
        run[...] = jnp.zeros(run.shape, F32)

    y = jnp.dot(x_ref[...], wbf_ref[...], preferred_element_type=F32)
    y = _ln(alpha * h_ref[...] + y, g_ref[...], b_ref[...])
    real = _rows_in_batch(pl.program_id(0), bm, lp) >= PAD_ROWS
    y = jnp.where(real, y, 0.0)
    o_ref[...] = y
    op_ref[...] = _pack_halves(y)
    routed = _route_rows(y, rw_ref[...], rb_ref[...], run, real, n_groups, per_group)
    ew_ref[...] = routed
    ewt_ref[...] = routed.T[0:8, :]
    cnt_ref[...] = run[...]


def _mm_ln(x, w, layer, h, g, b, rw, rb, lp, alpha, n_groups, per_group):
    t, k = x.shape
    d = w.shape[-1]
    bm = _pick(lp, (384, 128))
    row = lambda width: pl.BlockSpec((bm, width), lambda i: (i, 0))
    vec = pl.BlockSpec((1, d), lambda i: (0, 0))
    lanes = pl.BlockSpec((1, LANES), lambda i: (0, 0))
    return pl.pallas_call(
        functools.partial(_mm_ln_kernel, bm=bm, lp=lp, alpha=alpha, n_groups=n_groups, per_group=per_group),
        grid=(t // bm,),
        in_specs=[row(k), pl.BlockSpec((None, k, d), lambda i: (layer, 0, 0), pipeline_mode=pl.Buffered(1)),
                  row(d), vec, vec, pl.BlockSpec((d, LANES), lambda i: (0, 0)), lanes],
        out_specs=[row(d), row(d // 2), row(LANES), pl.BlockSpec((8, bm), lambda i: (0, i)), lanes],
        out_shape=[jax.ShapeDtypeStruct((t, d), F32), jax.ShapeDtypeStruct((t, d // 2), F32),
                   jax.ShapeDtypeStruct((t, LANES), F32), jax.ShapeDtypeStruct((8, t), F32),
                   jax.ShapeDtypeStruct((1, LANES), F32)],
        scratch_shapes=[pltpu.VMEM((k, d), BF16), pltpu.VMEM((1, LANES), F32)],
        compiler_params=_cparams(("arbitrary",), 56),
        name="mm_ln",
    )(x, w, h, g.reshape(1, d), b.reshape(1, d), rw, rb)


def _ssd_kernel(z_ref, xbc_ref, dtr_ref, cw_ref, cb_ref, dtb_ref, alog_ref, dexp_ref, ng_ref, o_ref,
                cbuf, xc, ybuf, st, *, q, d_inner, n_state):
    c = pl.program_id(1)
    xbc_w = cbuf.shape[1]
    kc = cw_ref.shape[0]
    hist = 8

    @pl.when(c == 0)
    def _():
        cbuf[0:hist, :] = jnp.zeros((hist, xbc_w), F32)
        st[...] = jnp.zeros(st.shape, F32)

    cbuf[hist:hist + q, :] = xbc_ref[...]
    cw = 128
    for j in range(xbc_w // cw):
        sl = slice(j * cw, (j + 1) * cw)
        conv = cb_ref[:, sl] + cw_ref[kc - 1:kc, sl] * cbuf[hist:hist + q, sl]
        for k in range(kc - 1):
            off = hist - (kc - 1) + k
            conv = conv + cw_ref[k:k + 1, sl] * cbuf[off:off + q, sl]
        xc[:, sl] = conv * _sigmoid(conv)
    cbuf[0:hist, :] = cbuf[q:q + hist, :]

    row = c * q + lax.broadcasted_iota(jnp.int32, (q, 1), 0)
    dt = _softplus(dtr_ref[...] + dtb_ref[...])
    dt = jnp.where(row >= PAD_ROWS, dt, 0.0)
    adt = dt * (-jnp.exp(alog_ref[...]))
    li = lax.broadcasted_iota(jnp.int32, (q, q), 0)
    si = lax.broadcasted_iota(jnp.int32, (q, q), 1)
    tri = li >= si
    acs = jnp.dot(tri.astype(F32), adt, preferred_element_type=F32,
                  precision=lax.Precision.HIGHEST) * LOG2E
    acs_t = acs.T
    dt_t = dt.T
    e_acs = jnp.exp2(acs)
    last = acs_t[:, q - 1:q]
    w_t = jnp.exp2(last - acs_t) * dt_t
    src_t = acs_t - jnp.log2(dt_t)
    dlast = jnp.exp2(last)
    lane = lax.broadcasted_iota(jnp.int32, (1, LANES), 1)
    lo_half = lane < SSD_HEAD_DIM

    heads_per_group = d_inner // SSD_HEAD_DIM // SSD_GROUPS
    pairs_per_group = heads_per_group // 2
    for g in range(SSD_GROUPS):
        b0 = d_inner + g * n_state
        c0 = d_inner + SSD_GROUPS * n_state + g * n_state
        bg = xc[:, b0:b0 + n_state]
        cg = xc[:, c0:c0 + n_state]
        cbm = lax.dot_general(cg.astype(BF16), bg.astype(BF16), (((1,), (1,)), ((), ())),
                              preferred_element_type=F32)
        bg_t = bg.T.astype(BF16)
        cbm = cbm.astype(BF16)
        cg_b = cg.astype(BF16)
        for j in range(pairs_per_group):
            pair = g * pairs_per_group + j
            h0 = 2 * pair
            xs_pair = xc[:, h0 * SSD_HEAD_DIM:h0 * SSD_HEAD_DIM + LANES]
            st_pair = st[pair]
            lhs_y, lhs_s, rhs_x, rhs_st = [], [], [], []
            for u in range(2):
                h = h0 + u
                col = acs[:, h:h + 1]
                rw = src_t[h:h + 1, :]
                decay = jnp.where(tri, jnp.exp2(col - rw), 0.0)
                lhs_y.append(cbm * decay.astype(BF16))
                lhs_s.append(bg_t * w_t[h:h + 1, :].astype(BF16))
                keep = lo_half if u == 0 else jnp.logical_not(lo_half)
                rhs_x.append(jnp.where(keep, xs_pair, 0.0).astype(BF16))
                rhs_st.append(jnp.where(keep, st_pair, 0.0).astype(BF16))
            for u in range(2):
                lhs_y.append(e_acs[:, h0 + u:h0 + u + 1].astype(BF16) * cg_b)
            y_pair = jnp.dot(jnp.concatenate(lhs_y, axis=1), jnp.concatenate(rhs_x + rhs_st, axis=0),
                             preferred_element_type=F32)
            s_new = jnp.dot(jnp.concatenate(lhs_s, axis=1), jnp.concatenate(rhs_x, axis=0),
                            preferred_element_type=F32)
            dpair = jnp.where(lo_half, dlast[h0:h0 + 1, :], dlast[h0 + 1:h0 + 2, :])
            st[pair] = st_pair * dpair + s_new
            ybuf[:, h0 * SSD_HEAD_DIM:h0 * SSD_HEAD_DIM + LANES] = y_pair

    gw = d_inner // SSD_GROUPS
    for g in range(SSD_GROUPS):
        sl = slice(g * gw, (g + 1) * gw)
        y = ybuf[:, sl] + xc[:, sl] * dexp_ref[:, sl]
        zz = z_ref[:, sl]
        y = y * (zz * _sigmoid(zz))
        y = y * lax.rsqrt(jnp.mean(y * y, -1, keepdims=True) + LN_EPS)
        o_ref[:, sl] = (y * ng_ref[:, sl]).astype(BF16)


def _ssd(z, xbc, dtr, conv_w, conv_b, dt_bias, a_log, d_skip, norm_g, nb, lp):
    t, d_inner = z.shape
    xbc_w = xbc.shape[1]
    q = SSD_CHUNK
    n_heads = d_inner // SSD_HEAD_DIM
    n_state = (xbc_w - d_inner) // (2 * SSD_GROUPS)
    assert n_state == LANES and n_heads % (2 * SSD_GROUPS) == 0 and lp % q == 0
    assert xbc_w % 512 == 0 and PAD_ROWS % 8 == 0
    nc = lp // q
    padh = LANES - n_heads
    row = lambda w: pl.BlockSpec((q, w), lambda b, c: (b * nc + c, 0))
    full = lambda a: pl.BlockSpec(a.shape, lambda b, c: (0,) * a.ndim)
    dtb = jnp.pad(dt_bias, (0, padh)).reshape(1, LANES)
    alog = jnp.pad(a_log, (0, padh)).reshape(1, LANES)
    dexp = jnp.repeat(d_skip, SSD_HEAD_DIM).reshape(1, d_inner)
    cb = conv_b.reshape(1, xbc_w)
    ng = norm_g.reshape(1, d_inner)
    return pl.pallas_call(
        functools.partial(_ssd_kernel, q=q, d_inner=d_inner, n_state=n_state),
        grid=(nb, nc),
        in_specs=[row(d_inner), row(xbc_w), row(LANES), full(conv_w), full(cb), full(dtb), full(alog),
                  full(dexp), full(ng)],
        out_specs=row(d_inner),
        out_shape=jax.ShapeDtypeStruct((t, d_inner), BF16),
        scratch_shapes=[pltpu.VMEM((8 + q, xbc_w), F32), pltpu.VMEM((q, xbc_w), F32),
                        pltpu.VMEM((q, d_inner), F32), pltpu.VMEM((n_heads // 2, n_state, LANES), F32)],
        compiler_params=_cparams(("arbitrary", "arbitrary"), 32),
        name="ssd",
    )(z, xbc, dtr, conv_w, cb, dtb, alog, dexp, ng)


def _pool_kernel(u_ref, w_ref, sc_ref, o_ref, buf, *, r, gd):
    c = pl.program_id(1)
    hist = 16
    width = buf.shape[1]

    @pl.when(c == 0)
    def _():
        buf[0:hist, :] = jnp.zeros((hist, width), F32)

    buf[hist:hist + r, :] = u_ref[...]
    pos = c * r + lax.broadcasted_iota(jnp.int32, (r, 1), 0) - PAD_ROWS
    for gi, win in enumerate(POOL_WINDOWS):
        sl = slice(gi * gd, (gi + 1) * gd)
        u = buf[hist:hist + r, sl]
        s = u
        for j in range(1, win):
            s = s + buf[hist - j:hist - j + r, sl]
        cnt = jnp.clip(pos + 1, 1, win).astype(F32)
        mixed = s / cnt - u
        y = jnp.dot(mixed.astype(BF16), w_ref[gi].astype(BF16), preferred_element_type=F32)
        o_ref[:, sl] = (y * sc_ref[:, sl]).astype(BF16)
    buf[0:hist, :] = buf[r:r + hist, :]


def _pool(pc, pool_w, scale, nb, lp):
    t = pc.shape[0]
    ng, gd, _ = pool_w.shape
    width = ng * gd
    assert ng == len(POOL_WINDOWS) and gd % LANES == 0
    r = _pick(lp, (384, 192, 128, 64))
    nc = lp // r
    return pl.pallas_call(
        functools.partial(_pool_kernel, r=r, gd=gd),
        grid=(nb, nc),
        in_specs=[pl.BlockSpec((r, width), lambda b, c: (b * nc + c, 0)),
                  pl.BlockSpec(pool_w.shape, lambda b, c: (0, 0, 0)),
                  pl.BlockSpec((1, width), lambda b, c: (0, 0))],
        out_specs=pl.BlockSpec((r, width), lambda b, c: (b * nc + c, 0)),
        out_shape=jax.ShapeDtypeStruct((t, width), BF16),
        scratch_shapes=[pltpu.VMEM((16 + r, width), F32)],
        compiler_params=_cparams(("arbitrary", "arbitrary"), 32),
        name="pool",
    )(pc, pool_w, scale.reshape(1, width))


def _conf_kernel(a_ref, g_ref, w_ref, b_ref, lg_ref, lb_ref, o_ref, vbuf, cv, ubuf, *, r):
    c = pl.program_id(1)
    hist = 32
    sub = 8
    width = vbuf.shape[1]
    kc = w_ref.shape[0]

    @pl.when(c == 0)
    def _():
        vbuf[0:hist, :] = jnp.zeros((hist, width), F32)

    vbuf[hist:hist + r, :] = a_ref[...] * _sigmoid(g_ref[...])
    ext = r + sub
    cw = ubuf.shape[2]
    for jc in range(width // cw):
        sl = slice(jc * cw, (jc + 1) * cw)
        acc = None
        for rr in range(sub):
            u = None
            for j in range((kc - 1 - rr) // sub + 1):
                k = kc - 1 - (sub * j + rr)
                lo = hist - sub - sub * j
                term = w_ref[k:k + 1, sl] * vbuf[lo:lo + ext, sl]
                u = term if u is None else u + term
            if rr == 0:
                acc = b_ref[:, sl] + u[sub:sub + r]
            else:
                ubuf[rr - 1] = u
                acc = acc + ubuf[rr - 1, sub - rr:sub - rr + r, :]
        cv[:, sl] = acc
    vbuf[0:hist, :] = vbuf[r:r + hist, :]
    v = _ln(cv[...], lg_ref[...], lb_ref[...])
    o_ref[...] = (v * _sigmoid(v)).astype(BF16)


def _conf(pc, col_blk, dw_w, dw_b, ln_g, ln_b, nb, lp):
    t = pc.shape[0]
    kc, width = dw_w.shape
    assert kc <= 33 and width % 256 == 0
    r = _pick(lp, (192, 128, 64))
    nc = lp // r
    vec = lambda: pl.BlockSpec((1, width), lambda b, c: (0, 0))
    return pl.pallas_call(
        functools.partial(_conf_kernel, r=r),
        grid=(nb, nc),
        in_specs=[pl.BlockSpec((r, width), lambda b, c: (b * nc + c, col_blk)),
                  pl.BlockSpec((r, width), lambda b, c: (b * nc + c, col_blk + 1)),
                  pl.BlockSpec((kc, width), lambda b, c: (0, 0)), vec(), vec(), vec()],
        out_specs=pl.BlockSpec((r, width), lambda b, c: (b * nc + c, 0)),
        out_shape=jax.ShapeDtypeStruct((t, width), BF16),
        scratch_shapes=[pltpu.VMEM((32 + r, width), F32), pltpu.VMEM((r, width), F32),
                        pltpu.VMEM((7, r + 8, LANES), F32)],
        compiler_params=_cparams(("arbitrary", "arbitrary"), 32),
        name="conf",
    )(pc, pc, dw_w, dw_b.reshape(1, width), ln_g.reshape(1, width), ln_b.reshape(1, width))


def _merge_kernel(hb_ref, f0_ref, f1_ref, f2_ref, g0_ref, g1_ref, g2_ref, gb_ref, p0_ref, p1_ref, p2_ref,
                  o_ref, gs, s0, s1, s2):
    @pl.when(pl.program_id(1) == 0)
    def _():
        gs[0] = g0_ref[...].astype(BF16)
        gs[1] = g1_ref[...].astype(BF16)
        gs[2] = g2_ref[...].astype(BF16)
        s0[...] = p0_ref[...].astype(BF16)
        s1[...] = p1_ref[...].astype(BF16)
        s2[...] = p2_ref[...].astype(BF16)

    hb = hb_ref[...]
    acc = None
    for j, (f_ref, s_ref) in enumerate(((f0_ref, s0), (f1_ref, s1), (f2_ref, s2))):
        gate = _sigmoid(jnp.dot(hb, gs[j], preferred_element_type=F32) + gb_ref[j:j + 1, :])
        term = gate * jnp.dot(f_ref[...], s_ref[...], preferred_element_type=F32)
        acc = term if acc is None else acc + term
    o_ref[...] = acc.astype(BF16)


def _merge(hb, f_ssd, f_pool, f_conf, gate_w, gate_b, ssd_proj, pool_proj, conf_proj, layer):
    t, d = hb.shape
    bn = _pick(d, (512, 256, 128))
    bm = _pick(t, (528, 384, 192, 128, 64, 8))
    nbn = d // bn
    once = pl.Buffered(1)
    rowspec = lambda a: pl.BlockSpec((bm, a.shape[1]), lambda n, m: (m, 0))
    gspec = lambda j: pl.BlockSpec((None, d, bn), lambda n, m: (layer, 0, j * nbn + n), pipeline_mode=once)
    pspec = lambda w: pl.BlockSpec((None, w.shape[1], bn), lambda n, m: (layer, 0, n))
    gb = gate_b[layer].reshape(3, d)
    return pl.pallas_call(
        _merge_kernel,
        grid=(nbn, t // bm),
        in_specs=[rowspec(hb), rowspec(f_ssd), rowspec(f_pool), rowspec(f_conf),
                  gspec(0), gspec(1), gspec(2), pl.BlockSpec((3, bn), lambda n, m: (0, n)),
                  pspec(ssd_proj), pspec(pool_proj), pspec(conf_proj)],
        out_specs=pl.BlockSpec((bm, bn), lambda n, m: (m, n)),
        out_shape=jax.ShapeDtypeStruct((t, d), BF16),
        scratch_shapes=[pltpu.VMEM((3, d, bn), BF16), pltpu.VMEM((ssd_proj.shape[1], bn), BF16),
                        pltpu.VMEM((pool_proj.shape[1], bn), BF16), pltpu.VMEM((conf_proj.shape[1], bn), BF16)],
        compiler_params=_cparams(("arbitrary", "arbitrary"), 58),
        name="merge",
    )(hb, f_ssd, f_pool, f_conf, gate_w, gate_w, gate_w, gb, ssd_proj, pool_proj, conf_proj)


def _route_rows(h, w, bias, run, real, n_groups, per_group):
    bm = h.shape[0]
    h_hi = h.astype(BF16)
    h_lo = (h - h_hi.astype(F32)).astype(BF16)
    w_hi = w.astype(BF16)
    w_lo = (w - w_hi.astype(F32)).astype(BF16)
    logits = (jnp.dot(h_hi, w_hi, preferred_element_type=F32)
              + (jnp.dot(h_hi, w_lo, preferred_element_type=F32)
                 + jnp.dot(h_lo, w_hi, preferred_element_type=F32))) + bias
    lane = lax.broadcasted_iota(jnp.int32, logits.shape, 1).astype(F32)

    def first_max(vals):
        m = jnp.max(vals, axis=1, keepdims=True)
        idx = jnp.min(jnp.where(vals == m, lane, float(LANES)), axis=1, keepdims=True)
        return m, idx

    gmask = lane < n_groups
    gmax, gsel = first_max(jnp.where(gmask, logits, NEG_BIG))
    gsum = jnp.sum(jnp.where(gmask, jnp.exp(logits - gmax), 0.0), axis=1, keepdims=True)
    p_group = 1.0 / gsum
    lo = n_groups + per_group * gsel
    el = jnp.where(jnp.logical_and(lane >= lo, lane < lo + per_group), logits, NEG_BIG)
    m1, i1 = first_max(el)
    m2, i2 = first_max(jnp.where(lane == i1, NEG_BIG, el))
    ratio = jnp.exp(m2 - m1)
    w1 = p_group / (1.0 + ratio)
    w2 = w1 * ratio
    sentinel = float(n_groups * per_group)
    e1 = jnp.where(real, i1 - n_groups, sentinel)
    e2 = jnp.where(real, i2 - n_groups, sentinel)
    w1 = jnp.where(real, w1, 0.0)
    w2 = jnp.where(real, w2, 0.0)
    hit1 = jnp.logical_and(lane == i1, real)
    hit2 = jnp.logical_and(lane == i2, real)
    onehot = jnp.where(jnp.logical_or(hit1, hit2), 1.0, 0.0)
    li = lax.broadcasted_iota(jnp.int32, (bm, bm), 0)
    si = lax.broadcasted_iota(jnp.int32, (bm, bm), 1)
    before = jnp.where(li > si, 1.0, 0.0).astype(BF16)
    seen = jnp.dot(before, onehot.astype(BF16), preferred_element_type=F32) + run[...]
    r1 = jnp.sum(jnp.where(hit1, seen, 0.0), axis=1, keepdims=True)
    r2 = jnp.sum(jnp.where(hit2, seen, 0.0), axis=1, keepdims=True)
    run[...] = run[...] + jnp.sum(onehot, axis=0, keepdims=True)
    out = jnp.zeros(logits.shape, F32)
    for k, v in enumerate((w1, w2, e1, e2, r1, r2)):
        out = jnp.where(lane == k, v, out)
    return out


def _moe_kernel(be_ref, nxt1_ref, nxt2_ref, par_ref, tot_ref, x_ref, wg_hbm, wu_hbm, wd_hbm, o_ref,
                wg_f, wu_f, wd_f, wg_s, wu_s, wd_s, sems, *, layer):
    i = pl.program_id(0)
    total = tot_ref[0]
    ic = jnp.maximum(jnp.minimum(i, total - 1), 0)
    e = be_ref[ic]
    e_prev = be_ref[jnp.maximum(ic - 1, 0)]
    active = i < total
    first_of_run = jnp.logical_and(active, jnp.logical_or(i == 0, e != e_prev))
    par = par_ref[ic]

    def weight_copies(expert, st):
        return (pltpu.make_async_copy(wg_hbm.at[layer, expert], wg_f.at[st], sems.at[st, 0]),
                pltpu.make_async_copy(wu_hbm.at[layer, expert], wu_f.at[st], sems.at[st, 1]),
                pltpu.make_async_copy(wd_hbm.at[layer, expert], wd_f.at[st], sems.at[st, 2]))

    @pl.when(jnp.logical_and(active, i == 0))
    def _():
        for cp in weight_copies(e, 0):
            cp.start()
        nxt1 = nxt1_ref[ic]

        @pl.when(nxt1 >= 0)
        def _():
            for cp in weight_copies(nxt1, 1):
                cp.start()

    @pl.when(first_of_run)
    def _():
        for cp, src, dst in zip(weight_copies(e, par), (wg_f, wu_f, wd_f), (wg_s, wu_s, wd_s)):
            cp.wait()
            dst[...] = src[par].astype(BF16)
        nxt2 = nxt2_ref[ic]

        @pl.when(nxt2 >= 0)
        def _():
            for cp in weight_copies(nxt2, par):
                cp.start()

    @pl.when(active)
    def _():
        x = _unpack_halves(x_ref[...]).astype(BF16)
        gte = jnp.dot(x, wg_s[...], preferred_element_type=F32)
        up = jnp.dot(x, wu_s[...], preferred_element_type=F32)
        hid = (gte * _sigmoid(gte)) * up
        o_ref[...] = _pack_halves(jnp.dot(hid.astype(BF16), wd_s[...], preferred_element_type=F32))

    @pl.when(jnp.logical_not(active))
    def _():
        o_ref[...] = jnp.zeros(o_ref.shape, F32)


def _moe_ffn(xg, plan, w_gate, w_up, w_down, layer):
    n_slots, d = xg.shape
    hid = w_gate.shape[-1]
    assert w_gate.shape[-2] == 2 * d
    n_blocks = n_slots // MOE_BLOCK

    def rows(i, be, nxt1, nxt2, par, tot):
        return (jnp.maximum(jnp.minimum(i, tot[0] - 1), 0), 0)

    hbm = pl.BlockSpec(memory_space=pl.ANY)
    grid_spec = pltpu.PrefetchScalarGridSpec(
        num_scalar_prefetch=5,
        grid=(n_blocks,),
        in_specs=[pl.BlockSpec((MOE_BLOCK, d), rows), hbm, hbm, hbm],
        out_specs=pl.BlockSpec((MOE_BLOCK, d), lambda i, *_: (i, 0)),
        scratch_shapes=[pltpu.VMEM((2, 2 * d, hid), F32), pltpu.VMEM((2, 2 * d, hid), F32),
                        pltpu.VMEM((2, hid, 2 * d), F32),
                        pltpu.VMEM((2 * d, hid), BF16), pltpu.VMEM((2 * d, hid), BF16),
                        pltpu.VMEM((hid, 2 * d), BF16), pltpu.SemaphoreType.DMA((2, 3))],
    )
    return pl.pallas_call(
        functools.partial(_moe_kernel, layer=layer),
        grid_spec=grid_spec,
        out_shape=jax.ShapeDtypeStruct((n_slots, d), F32),
        compiler_params=_cparams(("arbitrary",), 58),
        name="moe_ffn",
    )(*plan, xg, w_gate, w_up, w_down)


def _dispatch(ew_t, cnt, n_groups, n_experts):
    t = ew_t.shape[1]
    eid = ew_t[2:4].astype(jnp.int32).reshape(-1)
    rank = ew_t[4:6].astype(jnp.int32).reshape(-1)
    counts = cnt[0, n_groups:n_groups + n_experts].astype(jnp.int32)
    n_assign = eid.shape[0]
    tok = jnp.tile(jnp.arange(t, dtype=jnp.int32), 2)
    padded = (counts + MOE_BLOCK - 1) // MOE_BLOCK * MOE_BLOCK
    pend = jnp.cumsum(padded)
    pstart = pend - padded
    n_blocks = n_assign // MOE_BLOCK + n_experts
    n_slots = n_blocks * MOE_BLOCK
    valid = eid < n_experts
    slot = jnp.where(valid, pstart[jnp.minimum(eid, n_experts - 1)] + rank,
                     n_slots + jnp.arange(n_assign, dtype=jnp.int32))
    slot_tok = (jnp.arange(n_slots, dtype=jnp.int32) % t).at[slot].set(tok, mode="drop", unique_indices=True)
    first_row = jnp.arange(n_blocks, dtype=jnp.int32) * MOE_BLOCK
    block_expert = jnp.minimum(jnp.sum((pend[None, :] <= first_row[:, None]).astype(jnp.int32), axis=1),
                               n_experts - 1)
    total_blocks = (pend[-1] // MOE_BLOCK).astype(jnp.int32).reshape(1)
    ids = jnp.arange(n_experts, dtype=jnp.int32)
    has = counts > 0
    later = jnp.where(has, ids, n_experts)
    nxt1 = lax.cummin(jnp.concatenate([later[1:], jnp.full((1,), n_experts, jnp.int32)]), reverse=True)
    nxt2 = jnp.concatenate([nxt1, jnp.full((1,), n_experts, jnp.int32)])[nxt1]
    nxt1 = jnp.where(nxt1 >= n_experts, -1, nxt1)
    nxt2 = jnp.where(nxt2 >= n_experts, -1, nxt2)
    parity = (jnp.cumsum(has.astype(jnp.int32)) - 1) % 2
    plan = (block_expert, nxt1[block_expert], nxt2[block_expert], parity[block_expert], total_blocks)
    slot_c = jnp.where(valid, slot, 0).reshape(2, t)
    return slot_tok, plan, slot_c


def kernel(x, meta_tokens, ln_emb_g, ln_emb_b, w_in, ssd_conv_w, ssd_conv_b, ssd_dt_bias, ssd_a_log, ssd_d,
           ssd_norm_g, ssd_proj, pool_w, pool_scale, pool_proj, conf_dw_w, conf_dw_b, conf_ln_g, conf_ln_b,
           conf_proj, gate_w, gate_b, w_out, ln1_g, ln1_b, router_group_w, router_group_b, router_expert_w,
           router_expert_b, exp_w_gate, exp_w_up, exp_w_down, ln2_g, ln2_b):
    nb, seq, d = x.shape
    n_meta = meta_tokens.shape[0]
    depth = w_in.shape[0]
    lp = PAD_ROWS + n_meta + seq
    t = nb * lp
    alpha = (2.0 * depth) ** 0.25
    d_inner = ssd_norm_g.shape[-1]
    xbc_w = ssd_conv_w.shape[-1]
    n_heads = ssd_a_log.shape[-1]
    pool_width = pool_scale.shape[-1]
    conf_width = conf_dw_b.shape[-1]
    n_groups = router_group_w.shape[-1]
    n_experts = router_expert_w.shape[-1]
    per_group = n_experts // n_groups
    assert (PAD_ROWS + n_meta) % SSD_CHUNK == 0 and lp % SSD_CHUNK == 0
    assert n_groups + n_experts <= LANES and n_heads <= LANES

    h, hb = _embed_ln(x, meta_tokens.astype(x.dtype), ln_emb_g, ln_emb_b, lp)

    c_dt = d_inner + xbc_w
    c_pc = c_dt + n_heads
    assert c_dt % LANES == 0 and c_dt + LANES <= w_in.shape[-1]
    w_in_t = jnp.swapaxes(w_in, 1, 2)
    rw = jnp.concatenate([router_group_w, router_expert_w], axis=-1)
    rw = jnp.pad(rw, ((0, 0), (0, 0), (0, LANES - rw.shape[-1])))
    rb = jnp.concatenate([router_group_b, router_expert_b], axis=-1)
    rb = jnp.pad(rb, ((0, 0), (0, LANES - rb.shape[-1]))).reshape(depth, 1, LANES)

    for i in range(depth):
        z = _mm(hb, w_in_t, i, 0, d_inner)
        xbc = _mm(hb, w_in_t, i, d_inner, xbc_w)
        dtr = _mm(hb, w_in_t, i, c_dt, LANES)
        pc = _mm(hb, w_in_t, i, c_pc, pool_width + 2 * conf_width)
        f_ssd = _ssd(z, xbc, dtr, ssd_conv_w[i], ssd_conv_b[i], ssd_dt_bias[i], ssd_a_log[i], ssd_d[i],
                     ssd_norm_g[i], nb, lp)
        f_pool = _pool(pc, pool_w[i], pool_scale[i], nb, lp)
        assert pool_width % conf_width == 0
        f_conf = _conf(pc, pool_width // conf_width, conf_dw_w[i], conf_dw_b[i], conf_ln_g[i], conf_ln_b[i], nb, lp)
        merged = _merge(hb, f_ssd, f_pool, f_conf, gate_w, gate_b, ssd_proj, pool_proj, conf_proj, i)
        h, h_packed, ew, ew_t, cnt = _mm_ln(merged, w_out, i, h, ln1_g[i], ln1_b[i], rw[i], rb[i], lp, alpha,
                                            n_groups, per_group)
        slot_tok, plan, slot_c = _dispatch(ew_t, cnt, n_groups, n_experts)
        xg = h_packed.at[slot_tok].get(mode="promise_in_bounds")
        out = _moe_ffn(xg, plan, exp_w_gate, exp_w_up, exp_w_down, i)
        g0 = out.at[slot_c[0]].get(mode="promise_in_bounds")
        g1 = out.at[slot_c[1]].get(mode="promise_in_bounds")
        if i + 1 < depth:
            h, hb = _combine_ln(h, g0, g1, ew, ln2_g[i], ln2_b[i], lp, alpha)
        else:
            h = _final_ln(h, g0, g1, ew, ln2_g[i], ln2_b[i], nb, lp, seq, alpha)

    return h.reshape(nb, seq, d)
```

```python
import functools

import jax
import jax.numpy as jnp
from jax import lax
from jax.experimental import pallas as pl
from jax.experimental.pallas import tpu as pltpu

F32 = jnp.float32
BF16 = jnp.bfloat16

PAD_ROWS = 112
SSD_CHUNK = 128
SSD_GROUPS = 4
SSD_HEAD_DIM = 64
POOL_WINDOWS = (2, 4, 8, 16)
MOE_BLOCK = 256
LN_EPS = 1e-5
LANES = 128
NEG_BIG = -1e30
LOG2E = 1.4426950408889634


def _cparams(sem, vmem_mb):
    return pltpu.CompilerParams(dimension_semantics=sem, vmem_limit_bytes=vmem_mb << 20)


def _sigmoid(x):
    return 1.0 / (1.0 + jnp.exp(-x))


def _softplus(x):
    return jnp.maximum(x, 0.0) + jnp.log1p(jnp.exp(-jnp.abs(x)))


def _ln(x, g, b):
    mu = jnp.mean(x, -1, keepdims=True)
    xc = x - mu
    var = jnp.mean(xc * xc, -1, keepdims=True)
    return xc * lax.rsqrt(var + LN_EPS) * g + b


def _pack_halves(y):
    n = y.shape[1] // 2
    lo = lax.bitcast_convert_type(y[:, :n].astype(BF16).astype(F32), jnp.uint32)
    hi = lax.bitcast_convert_type(y[:, n:].astype(BF16).astype(F32), jnp.uint32)
    return lax.bitcast_convert_type(hi | (lo >> 16), F32)


def _unpack_halves(p):
    u = lax.bitcast_convert_type(p, jnp.uint32)
    lo = lax.bitcast_convert_type(u << 16, F32)
    hi = lax.bitcast_convert_type(u & jnp.uint32(0xFFFF0000), F32)
    return jnp.concatenate([lo, hi], axis=1)


def _rows_in_batch(blk, bm, lp):
    base = lax.rem(blk, lp // bm) * bm
    return base + lax.broadcasted_iota(jnp.int32, (bm, 1), 0)


def _pick(n, cands):
    for c in cands:
        if n % c == 0:
            return c
    raise ValueError(f"no block size for {n} among {cands}")


def _embed_kernel(x_ref, meta_ref, g_ref, b_ref, h_ref, hb_ref):
    j = pl.program_id(1)
    head, d = h_ref.shape
    n_meta = meta_ref.shape[0]

    @pl.when(j == 0)
    def _():
        y = _ln(meta_ref[...], g_ref[...], b_ref[...])
        h_ref[0:head - n_meta, :] = jnp.zeros((head - n_meta, d), F32)
        hb_ref[0:head - n_meta, :] = jnp.zeros((head - n_meta, d), BF16)
        h_ref[head - n_meta:head, :] = y
        hb_ref[head - n_meta:head, :] = y.astype(BF16)

    @pl.when(j > 0)
    def _():
        y = _ln(x_ref[...], g_ref[...], b_ref[...])
        h_ref[...] = y
        hb_ref[...] = y.astype(BF16)


def _embed_ln(x, meta, g, b, lp):
    nb, seq, d = x.shape
    n_meta = meta.shape[0]
    head = lp - seq
    assert head == PAD_ROWS + n_meta and seq % head == 0 and n_meta % 16 == 0
    per = lp // head
    vec = pl.BlockSpec((1, d), lambda bi, j: (0, 0))
    row = pl.BlockSpec((head, d), lambda bi, j: (bi * per + j, 0))
    return pl.pallas_call(
        _embed_kernel,
        grid=(nb, per),
        in_specs=[pl.BlockSpec((None, head, d), lambda bi, j: (bi, jnp.maximum(j - 1, 0), 0)),
                  pl.BlockSpec((n_meta, d), lambda bi, j: (0, 0)), vec, vec],
        out_specs=[row, row],
        out_shape=[jax.ShapeDtypeStruct((nb * lp, d), F32), jax.ShapeDtypeStruct((nb * lp, d), BF16)],
        compiler_params=_cparams(("parallel", "arbitrary"), 32),
        name="embed_ln",
    )(x, meta, g.reshape(1, d), b.reshape(1, d))


def _combine_ln_kernel(h_ref, g0_ref, g1_ref, ew_ref, g_ref, b_ref, o_ref, ob_ref, *, bm, lp, alpha):
    y = ew_ref[:, 0:1] * _unpack_halves(g0_ref[...]) + ew_ref[:, 1:2] * _unpack_halves(g1_ref[...])
    y = _ln(alpha * h_ref[...] + y, g_ref[...], b_ref[...])
    y = jnp.where(_rows_in_batch(pl.program_id(0), bm, lp) >= PAD_ROWS, y, 0.0)
    o_ref[...] = y
    ob_ref[...] = y.astype(BF16)


def _combine_ln(h, gathered, ew, g, b, lp, alpha):
    t, d = h.shape
    bm = _pick(lp, (528, 384, 192, 128, 64, 8))
    nblk = t // bm
    row = pl.BlockSpec((bm, d), lambda i: (i, 0))
    first = pl.BlockSpec((bm, d // 2), lambda i: (i, 0))
    second = pl.BlockSpec((bm, d // 2), lambda i: (i + nblk, 0))
    vec = pl.BlockSpec((1, d), lambda i: (0, 0))
    return pl.pallas_call(
        functools.partial(_combine_ln_kernel, bm=bm, lp=lp, alpha=alpha),
        grid=(nblk,),
        in_specs=[row, first, second, pl.BlockSpec((bm, LANES), lambda i: (i, 0)), vec, vec],
        out_specs=[row, row],
        out_shape=[jax.ShapeDtypeStruct((t, d), F32), jax.ShapeDtypeStruct((t, d), BF16)],
        compiler_params=_cparams(("parallel",), 56),
        name="combine_ln",
    )(h, gathered, gathered, ew, g.reshape(1, d), b.reshape(1, d))


def _final_ln_kernel(h_ref, g0_ref, g1_ref, ew_ref, g_ref, b_ref, o_ref, *, alpha):
    y = ew_ref[:, 0:1] * _unpack_halves(g0_ref[...]) + ew_ref[:, 1:2] * _unpack_halves(g1_ref[...])
    o_ref[...] = _ln(alpha * h_ref[...] + y, g_ref[...], b_ref[...])


def _final_ln(h, gathered, ew, g, b, nb, lp, seq, alpha):
    t, d = h.shape
    head = lp - seq
    bm = _pick(head, (128, 64, 8))
    assert seq % bm == 0
    per_seq, per_lp, skip = seq // bm, lp // bm, head // bm
    src = lambda i: ((i // per_seq) * per_lp + skip + i % per_seq, 0)
    src2 = lambda i: ((i // per_seq) * per_lp + skip + i % per_seq + t // bm, 0)
    row = pl.BlockSpec((bm, d), src)
    vec = pl.BlockSpec((1, d), lambda i: (0, 0))
    return pl.pallas_call(
        functools.partial(_final_ln_kernel, alpha=alpha),
        grid=(nb * per_seq,),
        in_specs=[row, pl.BlockSpec((bm, d // 2), src), pl.BlockSpec((bm, d // 2), src2),
                  pl.BlockSpec((bm, LANES), src), vec, vec],
        out_specs=pl.BlockSpec((bm, d), lambda i: (i, 0)),
        out_shape=jax.ShapeDtypeStruct((nb * seq, d), F32),
        compiler_params=_cparams(("parallel",), 32),
        name="final_ln",
    )(h, gathered, gathered, ew, g.reshape(1, d), b.reshape(1, d))


def _mm_kernel(x_ref, w_ref, *rest, shift):
    if shift:
        wnext_ref, o_ref, wbf_ref = rest
    else:
        o_ref, wbf_ref = rest
    bn = w_ref.shape[0]
    step = min(bn, 256)

    @pl.when(pl.program_id(1) == 0)
    def _():
        for c in range(bn // step):
            lo, hi = shift + c * step, shift + (c + 1) * step
            if hi <= bn:
                rows = w_ref[lo:hi, :]
            else:
                rows = jnp.concatenate([w_ref[lo:bn, :], wnext_ref[0:hi - bn, :]], axis=0)
            wbf_ref[:, c * step:(c + 1) * step] = rows.T.astype(BF16)

    o_ref[...] = jnp.dot(x_ref[...], wbf_ref[...], preferred_element_type=F32).astype(o_ref.dtype)


def _mm(x, w_t, layer, col0, n_cols, out_dtype=F32):
    t, k = x.shape
    bn = _pick(n_cols, (1024, 512, 256, 128))
    bm = _pick(t, (1056, 528, 384, 192, 128, 64, 8))
    shift = col0 % LANES
    base = col0 - shift
    assert base % bn == 0 and shift % 8 == 0
    cb0 = base // bn
    in_specs = [pl.BlockSpec((bm, k), lambda n, m: (m, 0)),
                pl.BlockSpec((None, bn, k), lambda n, m: (layer, cb0 + n, 0))]
    args = [x, w_t]
    if shift:
        tiles = bn // LANES
        in_specs.append(pl.BlockSpec((None, LANES, k), lambda n, m: (layer, (cb0 + n + 1) * tiles, 0)))
        args.append(w_t)
    return pl.pallas_call(
        functools.partial(_mm_kernel, shift=shift),
        grid=(n_cols // bn, t // bm),
        in_specs=in_specs,
        out_specs=pl.BlockSpec((bm, bn), lambda n, m: (m, n)),
        out_shape=jax.ShapeDtypeStruct((t, n_cols), out_dtype),
        scratch_shapes=[pltpu.VMEM((k, bn), BF16)],
        compiler_params=_cparams(("arbitrary", "arbitrary"), 48),
        name="mm",
    )(*args)


def _mm_ln_kernel(x_ref, w_ref, h_ref, g_ref, b_ref, rw_ref, rb_ref, o_ref, op_ref, ew_ref, ewt_ref, cnt_ref,
                  wbf_ref, run, *, bm, lp, alpha, n_groups, per_group):
    @pl.when(pl.program_id(0) == 0)
    def _():
        wbf_ref[...] = w_ref[...].astype(BF16)
        run[...] = jnp.zeros(run.shape, F32)

    y = jnp.dot(x_ref[...], wbf_ref[...], preferred_element_type=F32)
    y = _ln(alpha * h_ref[...] + y, g_ref[...], b_ref[...])
    real = _rows_in_batch(pl.program_id(0), bm, lp) >= PAD_ROWS
    y = jnp.where(real, y, 0.0)
    o_ref[...] = y
    op_ref[...] = _pack_halves(y)
    routed = _route_rows(y, rw_ref[...], rb_ref[...], run, real, n_groups, per_group)
    ew_ref[...] = routed
    ewt_ref[...] = routed.T[0:8, :]
    cnt_ref[...] = run[...]


def _mm_ln(x, w, layer, h, g, b, rw, rb, lp, alpha, n_groups, per_group):
    t, k = x.shape
    d = w.shape[-1]
    bm = _pick(lp, (384, 128))
    row = lambda width: pl.BlockSpec((bm, width), lambda i: (i, 0))
    vec = pl.BlockSpec((1, d), lambda i: (0, 0))
    lanes = pl.BlockSpec((1, LANES), lambda i: (0, 0))
    return pl.pallas_call(
        functools.partial(_mm_ln_kernel, bm=bm, lp=lp, alpha=alpha, n_groups=n_groups, per_group=per_group),
        grid=(t // bm,),
        in_specs=[row(k), pl.BlockSpec((None, k, d), lambda i: (layer, 0, 0), pipeline_mode=pl.Buffered(1)),
                  row(d), vec, vec, pl.BlockSpec((d, LANES), lambda i: (0, 0)), lanes],
        out_specs=[row(d), row(d // 2), row(LANES), pl.BlockSpec((8, bm), lambda i: (0, i)), lanes],
        out_shape=[jax.ShapeDtypeStruct((t, d), F32), jax.ShapeDtypeStruct((t, d // 2), F32),
                   jax.ShapeDtypeStruct((t, LANES), F32), jax.ShapeDtypeStruct((8, t), F32),
                   jax.ShapeDtypeStruct((1, LANES), F32)],
        scratch_shapes=[pltpu.VMEM((k, d), BF16), pltpu.VMEM((1, LANES), F32)],
        compiler_params=_cparams(("arbitrary",), 56),
        name="mm_ln",
    )(x, w, h, g.reshape(1, d), b.reshape(1, d), rw, rb)


def _ssd_kernel(z_ref, xbc_ref, dtr_ref, cw_ref, cb_ref, dtb_ref, alog_ref, dexp_ref, ng_ref, o_ref,
                cbuf, xc, ybuf, st, *, q, d_inner, n_state):
    c = pl.program_id(1)
    xbc_w = cbuf.shape[1]
    kc = cw_ref.shape[0]
    hist = 8

    @pl.when(c == 0)
    def _():
        cbuf[0:hist, :] = jnp.zeros((hist, xbc_w), F32)
        st[...] = jnp.zeros(st.shape, F32)

    cbuf[hist:hist + q, :] = xbc_ref[...]
    cw = 128
    for j in range(xbc_w // cw):
        sl = slice(j * cw, (j + 1) * cw)
        conv = cb_ref[:, sl] + cw_ref[kc - 1:kc, sl] * cbuf[hist:hist + q, sl]
        for k in range(kc - 1):
            off = hist - (kc - 1) + k
            conv = conv + cw_ref[k:k + 1, sl] * cbuf[off:off + q, sl]
        xc[:, sl] = conv * _sigmoid(conv)
    cbuf[0:hist, :] = cbuf[q:q + hist, :]

    row = c * q + lax.broadcasted_iota(jnp.int32, (q, 1), 0)
    dt = _softplus(dtr_ref[...] + dtb_ref[...])
    dt = jnp.where(row >= PAD_ROWS, dt, 0.0)
    adt = dt * (-jnp.exp(alog_ref[...]))
    li = lax.broadcasted_iota(jnp.int32, (q, q), 0)
    si = lax.broadcasted_iota(jnp.int32, (q, q), 1)
    tri = li >= si
    acs = jnp.dot(tri.astype(F32), adt, preferred_element_type=F32,
                  precision=lax.Precision.HIGHEST) * LOG2E
    acs_t = acs.T
    dt_t = dt.T
    e_acs = jnp.exp2(acs)
    last = acs_t[:, q - 1:q]
    w_t = jnp.exp2(last - acs_t) * dt_t
    src_t = acs_t - jnp.log2(dt_t)
    dlast = jnp.exp2(last)
    lane = lax.broadcasted_iota(jnp.int32, (1, LANES), 1)
    lo_half = lane < SSD_HEAD_DIM

    heads_per_group = d_inner // SSD_HEAD_DIM // SSD_GROUPS
    pairs_per_group = heads_per_group // 2
    for g in range(SSD_GROUPS):
        b0 = d_inner + g * n_state
        c0 = d_inner + SSD_GROUPS * n_state + g * n_state
        bg = xc[:, b0:b0 + n_state]
        cg = xc[:, c0:c0 + n_state]
        cbm = lax.dot_general(cg.astype(BF16), bg.astype(BF16), (((1,), (1,)), ((), ())),
                              preferred_element_type=F32)
        bg_t = bg.T.astype(BF16)
        cbm = cbm.astype(BF16)
        cg_b = cg.astype(BF16)
        for j in range(pairs_per_group):
            pair = g * pairs_per_group + j
            h0 = 2 * pair
            xs_pair = xc[:, h0 * SSD_HEAD_DIM:h0 * SSD_HEAD_DIM + LANES]
            st_pair = st[pair]
            lhs_y, lhs_s, rhs_x, rhs_st = [], [], [], []
            for u in range(2):
                h = h0 + u
                col = acs[:, h:h + 1]
                rw = src_t[h:h + 1, :]
                decay = jnp.where(tri, jnp.exp2(col - rw), 0.0)
                lhs_y.append(cbm * decay.astype(BF16))
                lhs_s.append(bg_t * w_t[h:h + 1, :].astype(BF16))
                keep = lo_half if u == 0 else jnp.logical_not(lo_half)
                rhs_x.append(jnp.where(keep, xs_pair, 0.0).astype(BF16))
                rhs_st.append(jnp.where(keep, st_pair, 0.0).astype(BF16))
            for u in range(2):
                lhs_y.append(e_acs[:, h0 + u:h0 + u + 1].astype(BF16) * cg_b)
            y_pair = jnp.dot(jnp.concatenate(lhs_y, axis=1), jnp.concatenate(rhs_x + rhs_st, axis=0),
                             preferred_element_type=F32)
            s_new = jnp.dot(jnp.concatenate(lhs_s, axis=1), jnp.concatenate(rhs_x, axis=0),
                            preferred_element_type=F32)
            dpair = jnp.where(lo_half, dlast[h0:h0 + 1, :], dlast[h0 + 1:h0 + 2, :])
            st[pair] = st_pair * dpair + s_new
            ybuf[:, h0 * SSD_HEAD_DIM:h0 * SSD_HEAD_DIM + LANES] = y_pair

    gw = d_inner // SSD_GROUPS
    for g in range(SSD_GROUPS):
        sl = slice(g * gw, (g + 1) * gw)
        y = ybuf[:, sl] + xc[:, sl] * dexp_ref[:, sl]
        zz = z_ref[:, sl]
        y = y * (zz * _sigmoid(zz))
        y = y * lax.rsqrt(jnp.mean(y * y, -1, keepdims=True) + LN_EPS)
        o_ref[:, sl] = (y * ng_ref[:, sl]).astype(BF16)


def _ssd(z, xbc, dtr, conv_w, conv_b, dt_bias, a_log, d_skip, norm_g, nb, lp):
    t, d_inner = z.shape
    xbc_w = xbc.shape[1]
    q = SSD_CHUNK
    n_heads = d_inner // SSD_HEAD_DIM
    n_state = (xbc_w - d_inner) // (2 * SSD_GROUPS)
    assert n_state == LANES and n_heads % (2 * SSD_GROUPS) == 0 and lp % q == 0
    assert xbc_w % 512 == 0 and PAD_ROWS % 8 == 0
    nc = lp // q
    padh = LANES - n_heads
    row = lambda w: pl.BlockSpec((q, w), lambda b, c: (b * nc + c, 0))
    full = lambda a: pl.BlockSpec(a.shape, lambda b, c: (0,) * a.ndim)
    dtb = jnp.pad(dt_bias, (0, padh)).reshape(1, LANES)
    alog = jnp.pad(a_log, (0, padh)).reshape(1, LANES)
    dexp = jnp.repeat(d_skip, SSD_HEAD_DIM).reshape(1, d_inner)
    cb = conv_b.reshape(1, xbc_w)
    ng = norm_g.reshape(1, d_inner)
    return pl.pallas_call(
        functools.partial(_ssd_kernel, q=q, d_inner=d_inner, n_state=n_state),
        grid=(nb, nc),
        in_specs=[row(d_inner), row(xbc_w), row(LANES), full(conv_w), full(cb), full(dtb), full(alog),
                  full(dexp), full(ng)],
        out_specs=row(d_inner),
        out_shape=jax.ShapeDtypeStruct((t, d_inner), BF16),
        scratch_shapes=[pltpu.VMEM((8 + q, xbc_w), F32), pltpu.VMEM((q, xbc_w), F32),
                        pltpu.VMEM((q, d_inner), F32), pltpu.VMEM((n_heads // 2, n_state, LANES), F32)],
        compiler_params=_cparams(("arbitrary", "arbitrary"), 32),
        name="ssd",
    )(z, xbc, dtr, conv_w, cb, dtb, alog, dexp, ng)


def _pool_kernel(u_ref, w_ref, sc_ref, o_ref, buf, *, r, gd):
    c = pl.program_id(1)
    hist = 16
    width = buf.shape[1]

    @pl.when(c == 0)
    def _():
        buf[0:hist, :] = jnp.zeros((hist, width), F32)

    buf[hist:hist + r, :] = u_ref[...]
    pos = c * r + lax.broadcasted_iota(jnp.int32, (r, 1), 0) - PAD_ROWS
    for gi, win in enumerate(POOL_WINDOWS):
        sl = slice(gi * gd, (gi + 1) * gd)
        u = buf[hist:hist + r, sl]
        s = u
        for j in range(1, win):
            s = s + buf[hist - j:hist - j + r, sl]
        cnt = jnp.clip(pos + 1, 1, win).astype(F32)
        mixed = s / cnt - u
        y = jnp.dot(mixed.astype(BF16), w_ref[gi].astype(BF16), preferred_element_type=F32)
        o_ref[:, sl] = (y * sc_ref[:, sl]).astype(BF16)
    buf[0:hist, :] = buf[r:r + hist, :]


def _pool(pc, pool_w, scale, nb, lp):
    t = pc.shape[0]
    ng, gd, _ = pool_w.shape
    width = ng * gd
    assert ng == len(POOL_WINDOWS) and gd % LANES == 0
    r = _pick(lp, (384, 192, 128, 64))
    nc = lp // r
    return pl.pallas_call(
        functools.partial(_pool_kernel, r=r, gd=gd),
        grid=(nb, nc),
        in_specs=[pl.BlockSpec((r, width), lambda b, c: (b * nc + c, 0)),
                  pl.BlockSpec(pool_w.shape, lambda b, c: (0, 0, 0)),
                  pl.BlockSpec((1, width), lambda b, c: (0, 0))],
        out_specs=pl.BlockSpec((r, width), lambda b, c: (b * nc + c, 0)),
        out_shape=jax.ShapeDtypeStruct((t, width), BF16),
        scratch_shapes=[pltpu.VMEM((16 + r, width), F32)],
        compiler_params=_cparams(("arbitrary", "arbitrary"), 32),
        name="pool",
    )(pc, pool_w, scale.reshape(1, width))


def _conf_kernel(a_ref, g_ref, w_ref, b_ref, lg_ref, lb_ref, o_ref, vbuf, cv, ubuf, *, r):
    c = pl.program_id(1)
    hist = 32
    sub = 8
    width = vbuf.shape[1]
    kc = w_ref.shape[0]

    @pl.when(c == 0)
    def _():
        vbuf[0:hist, :] = jnp.zeros((hist, width), F32)

    vbuf[hist:hist + r, :] = a_ref[...] * _sigmoid(g_ref[...])
    ext = r + sub
    cw = ubuf.shape[2]
    for jc in range(width // cw):
        sl = slice(jc * cw, (jc + 1) * cw)
        acc = None
        for rr in range(sub):
            u = None
            for j in range((kc - 1 - rr) // sub + 1):
                k = kc - 1 - (sub * j + rr)
                lo = hist - sub - sub * j
                term = w_ref[k:k + 1, sl] * vbuf[lo:lo + ext, sl]
                u = term if u is None else u + term
            if rr == 0:
                acc = b_ref[:, sl] + u[sub:sub + r]
            else:
                ubuf[rr - 1] = u
                acc = acc + ubuf[rr - 1, sub - rr:sub - rr + r, :]
        cv[:, sl] = acc
    vbuf[0:hist, :] = vbuf[r:r + hist, :]
    v = _ln(cv[...], lg_ref[...], lb_ref[...])
    o_ref[...] = (v * _sigmoid(v)).astype(BF16)


def _conf(pc, col_blk, dw_w, dw_b, ln_g, ln_b, nb, lp):
    t = pc.shape[0]
    kc, width = dw_w.shape
    assert kc <= 33 and width % 256 == 0
    r = _pick(lp, (192, 128, 64))
    nc = lp // r
    vec = lambda: pl.BlockSpec((1, width), lambda b, c: (0, 0))
    return pl.pallas_call(
        functools.partial(_conf_kernel, r=r),
        grid=(nb, nc),
        in_specs=[pl.BlockSpec((r, width), lambda b, c: (b * nc + c, col_blk)),
                  pl.BlockSpec((r, width), lambda b, c: (b * nc + c, col_blk + 1)),
                  pl.BlockSpec((kc, width), lambda b, c: (0, 0)), vec(), vec(), vec()],
        out_specs=pl.BlockSpec((r, width), lambda b, c: (b * nc + c, 0)),
        out_shape=jax.ShapeDtypeStruct((t, width), BF16),
        scratch_shapes=[pltpu.VMEM((32 + r, width), F32), pltpu.VMEM((r, width), F32),
                        pltpu.VMEM((7, r + 8, LANES), F32)],
        compiler_params=_cparams(("arbitrary", "arbitrary"), 32),
        name="conf",
    )(pc, pc, dw_w, dw_b.reshape(1, width), ln_g.reshape(1, width), ln_b.reshape(1, width))


def _merge_kernel(hb_ref, f0_ref, f1_ref, f2_ref, g0_ref, g1_ref, g2_ref, gb_ref, p0_ref, p1_ref, p2_ref,
                  o_ref, gs, s0, s1, s2):
    @pl.when(pl.program_id(1) == 0)
    def _():
        gs[0] = g0_ref[...].astype(BF16)
        gs[1] = g1_ref[...].astype(BF16)
        gs[2] = g2_ref[...].astype(BF16)
        s0[...] = p0_ref[...].astype(BF16)
        s1[...] = p1_ref[...].astype(BF16)
        s2[...] = p2_ref[...].astype(BF16)

    hb = hb_ref[...]
    acc = None
    for j, (f_ref, s_ref) in enumerate(((f0_ref, s0), (f1_ref, s1), (f2_ref, s2))):
        gate = _sigmoid(jnp.dot(hb, gs[j], preferred_element_type=F32) + gb_ref[j:j + 1, :])
        term = gate * jnp.dot(f_ref[...], s_ref[...], preferred_element_type=F32)
        acc = term if acc is None else acc + term
    o_ref[...] = acc.astype(BF16)


def _merge(hb, f_ssd, f_pool, f_conf, gate_w, gate_b, ssd_proj, pool_proj, conf_proj, layer):
    t, d = hb.shape
    bn = _pick(d, (512, 256, 128))
    bm = _pick(t, (528, 384, 192, 128, 64, 8))
    nbn = d // bn
    once = pl.Buffered(1)
    rowspec = lambda a: pl.BlockSpec((bm, a.shape[1]), lambda n, m: (m, 0))
    gspec = lambda j: pl.BlockSpec((None, d, bn), lambda n, m: (layer, 0, j * nbn + n), pipeline_mode=once)
    pspec = lambda w: pl.BlockSpec((None, w.shape[1], bn), lambda n, m: (layer, 0, n))
    gb = gate_b[layer].reshape(3, d)
    return pl.pallas_call(
        _merge_kernel,
        grid=(nbn, t // bm),
        in_specs=[rowspec(hb), rowspec(f_ssd), rowspec(f_pool), rowspec(f_conf),
                  gspec(0), gspec(1), gspec(2), pl.BlockSpec((3, bn), lambda n, m: (0, n)),
                  pspec(ssd_proj), pspec(pool_proj), pspec(conf_proj)],
        out_specs=pl.BlockSpec((bm, bn), lambda n, m: (m, n)),
        out_shape=jax.ShapeDtypeStruct((t, d), BF16),
        scratch_shapes=[pltpu.VMEM((3, d, bn), BF16), pltpu.VMEM((ssd_proj.shape[1], bn), BF16),
                        pltpu.VMEM((pool_proj.shape[1], bn), BF16), pltpu.VMEM((conf_proj.shape[1], bn), BF16)],
        compiler_params=_cparams(("arbitrary", "arbitrary"), 58),
        name="merge",
    )(hb, f_ssd, f_pool, f_conf, gate_w, gate_w, gate_w, gb, ssd_proj, pool_proj, conf_proj)


def _route_rows(h, w, bias, run, real, n_groups, per_group):
    bm = h.shape[0]
    h_hi = h.astype(BF16)
    h_lo = (h - h_hi.astype(F32)).astype(BF16)
    w_hi = w.astype(BF16)
    w_lo = (w - w_hi.astype(F32)).astype(BF16)
    logits = (jnp.dot(h_hi, w_hi, preferred_element_type=F32)
              + (jnp.dot(h_hi, w_lo, preferred_element_type=F32)
                 + jnp.dot(h_lo, w_hi, preferred_element_type=F32))) + bias
    lane = lax.broadcasted_iota(jnp.int32, logits.shape, 1).astype(F32)

    def first_max(vals):
        m = jnp.max(vals, axis=1, keepdims=True)
        idx = jnp.min(jnp.where(vals == m, lane, float(LANES)), axis=1, keepdims=True)
        return m, idx

    gmask = lane < n_groups
    gmax, gsel = first_max(jnp.where(gmask, logits, NEG_BIG))
    gsum = jnp.sum(jnp.where(gmask, jnp.exp(logits - gmax), 0.0), axis=1, keepdims=True)
    p_group = 1.0 / gsum
    lo = n_groups + per_group * gsel
    el = jnp.where(jnp.logical_and(lane >= lo, lane < lo + per_group), logits, NEG_BIG)
    m1, i1 = first_max(el)
    m2, i2 = first_max(jnp.where(lane == i1, NEG_BIG, el))
    ratio = jnp.exp(m2 - m1)
    w1 = p_group / (1.0 + ratio)
    w2 = w1 * ratio
    sentinel = float(n_groups * per_group)
    e1 = jnp.where(real, i1 - n_groups, sentinel)
    e2 = jnp.where(real, i2 - n_groups, sentinel)
    w1 = jnp.where(real, w1, 0.0)
    w2 = jnp.where(real, w2, 0.0)
    hit1 = jnp.logical_and(lane == i1, real)
    hit2 = jnp.logical_and(lane == i2, real)
    onehot = jnp.where(jnp.logical_or(hit1, hit2), 1.0, 0.0)
    li = lax.broadcasted_iota(jnp.int32, (bm, bm), 0)
    si = lax.broadcasted_iota(jnp.int32, (bm, bm), 1)
    before = jnp.where(li > si, 1.0, 0.0).astype(BF16)
    seen = jnp.dot(before, onehot.astype(BF16), preferred_element_type=F32) + run[...]
    r1 = jnp.sum(jnp.where(hit1, seen, 0.0), axis=1, keepdims=True)
    r2 = jnp.sum(jnp.where(hit2, seen, 0.0), axis=1, keepdims=True)
    run[...] = run[...] + jnp.sum(onehot, axis=0, keepdims=True)
    out = jnp.zeros(logits.shape, F32)
    for k, v in enumerate((w1, w2, e1, e2, r1, r2)):
        out = jnp.where(lane == k, v, out)
    return out


def _moe_kernel(be_ref, nxt1_ref, nxt2_ref, par_ref, tot_ref, x_ref, wg_hbm, wu_hbm, wd_hbm, o_ref,
                wg_f, wu_f, wd_f, wg_s, wu_s, wd_s, sems, *, layer):
    i = pl.program_id(0)
    total = tot_ref[0]
    ic = jnp.maximum(jnp.minimum(i, total - 1), 0)
    e = be_ref[ic]
    e_prev = be_ref[jnp.maximum(ic - 1, 0)]
    active = i < total
    first_of_run = jnp.logical_and(active, jnp.logical_or(i == 0, e != e_prev))
    par = par_ref[ic]

    def weight_copies(expert, st):
        return (pltpu.make_async_copy(wg_hbm.at[layer, expert], wg_f.at[st], sems.at[st, 0]),
                pltpu.make_async_copy(wu_hbm.at[layer, expert], wu_f.at[st], sems.at[st, 1]),
                pltpu.make_async_copy(wd_hbm.at[layer, expert], wd_f.at[st], sems.at[st, 2]))

    @pl.when(jnp.logical_and(active, i == 0))
    def _():
        for cp in weight_copies(e, 0):
            cp.start()
        nxt1 = nxt1_ref[ic]

        @pl.when(nxt1 >= 0)
        def _():
            for cp in weight_copies(nxt1, 1):
                cp.start()

    @pl.when(first_of_run)
    def _():
        for cp, src, dst in zip(weight_copies(e, par), (wg_f, wu_f, wd_f), (wg_s, wu_s, wd_s)):
            cp.wait()
            dst[...] = src[par].astype(BF16)
        nxt2 = nxt2_ref[ic]

        @pl.when(nxt2 >= 0)
        def _():
            for cp in weight_copies(nxt2, par):
                cp.start()

    @pl.when(active)
    def _():
        x = _unpack_halves(x_ref[...]).astype(BF16)
        gte = jnp.dot(x, wg_s[...], preferred_element_type=F32)
        up = jnp.dot(x, wu_s[...], preferred_element_type=F32)
        hid = (gte * _sigmoid(gte)) * up
        o_ref[...] = _pack_halves(jnp.dot(hid.astype(BF16), wd_s[...], preferred_element_type=F32))

    @pl.when(jnp.logical_not(active))
    def _():
        o_ref[...] = jnp.zeros(o_ref.shape, F32)


def _moe_ffn(xg, plan, w_gate, w_up, w_down, layer):
    n_slots, d = xg.shape
    hid = w_gate.shape[-1]
    assert w_gate.shape[-2] == 2 * d
    n_blocks = n_slots // MOE_BLOCK

    def rows(i, be, nxt1, nxt2, par, tot):
        return (jnp.maximum(jnp.minimum(i, tot[0] - 1), 0), 0)

    hbm = pl.BlockSpec(memory_space=pl.ANY)
    grid_spec = pltpu.PrefetchScalarGridSpec(
        num_scalar_prefetch=5,
        grid=(n_blocks,),
        in_specs=[pl.BlockSpec((MOE_BLOCK, d), rows), hbm, hbm, hbm],
        out_specs=pl.BlockSpec((MOE_BLOCK, d), lambda i, *_: (i, 0)),
        scratch_shapes=[pltpu.VMEM((2, 2 * d, hid), F32), pltpu.VMEM((2, 2 * d, hid), F32),
                        pltpu.VMEM((2, hid, 2 * d), F32),
                        pltpu.VMEM((2 * d, hid), BF16), pltpu.VMEM((2 * d, hid), BF16),
                        pltpu.VMEM((hid, 2 * d), BF16), pltpu.SemaphoreType.DMA((2, 3))],
    )
    return pl.pallas_call(
        functools.partial(_moe_kernel, layer=layer),
        grid_spec=grid_spec,
        out_shape=jax.ShapeDtypeStruct((n_slots, d), F32),
        compiler_params=_cparams(("arbitrary",), 58),
        name="moe_ffn",
    )(*plan, xg, w_gate, w_up, w_down)


def _dispatch(ew_t, cnt, n_groups, n_experts):
    t = ew_t.shape[1]
    eid = ew_t[2:4].astype(jnp.int32).reshape(-1)
    rank = ew_t[4:6].astype(jnp.int32).reshape(-1)
    counts = cnt[0, n_groups:n_groups + n_experts].astype(jnp.int32)
    n_assign = eid.shape[0]
    tok = jnp.tile(jnp.arange(t, dtype=jnp.int32), 2)
    padded = (counts + MOE_BLOCK - 1) // MOE_BLOCK * MOE_BLOCK
    pend = jnp.cumsum(padded)
    pstart = pend - padded
    n_blocks = n_assign // MOE_BLOCK + n_experts
    n_slots = n_blocks * MOE_BLOCK
    valid = eid < n_experts
    slot = jnp.where(valid, pstart[jnp.minimum(eid, n_experts - 1)] + rank,
                     n_slots + jnp.arange(n_assign, dtype=jnp.int32))
    slot_tok = (jnp.arange(n_slots, dtype=jnp.int32) % t).at[slot].set(tok, mode="drop", unique_indices=True)
    first_row = jnp.arange(n_blocks, dtype=jnp.int32) * MOE_BLOCK
    block_expert = jnp.minimum(jnp.sum((pend[None, :] <= first_row[:, None]).astype(jnp.int32), axis=1),
                               n_experts - 1)
    total_blocks = (pend[-1] // MOE_BLOCK).astype(jnp.int32).reshape(1)
    ids = jnp.arange(n_experts, dtype=jnp.int32)
    has = counts > 0
    later = jnp.where(has, ids, n_experts)
    nxt1 = lax.cummin(jnp.concatenate([later[1:], jnp.full((1,), n_experts, jnp.int32)]), reverse=True)
    nxt2 = jnp.concatenate([nxt1, jnp.full((1,), n_experts, jnp.int32)])[nxt1]
    nxt1 = jnp.where(nxt1 >= n_experts, -1, nxt1)
    nxt2 = jnp.where(nxt2 >= n_experts, -1, nxt2)
    parity = (jnp.cumsum(has.astype(jnp.int32)) - 1) % 2
    plan = (block_expert, nxt1[block_expert], nxt2[block_expert], parity[block_expert], total_blocks)
    slot_c = jnp.where(valid, slot, 0).reshape(2, t)
    return slot_tok, plan, slot_c


def kernel(x, meta_tokens, ln_emb_g, ln_emb_b, w_in, ssd_conv_w, ssd_conv_b, ssd_dt_bias, ssd_a_log, ssd_d,
           ssd_norm_g, ssd_proj, pool_w, pool_scale, pool_proj, conf_dw_w, conf_dw_b, conf_ln_g, conf_ln_b,
           conf_proj, gate_w, gate_b, w_out, ln1_g, ln1_b, router_group_w, router_group_b, router_expert_w,
           router_expert_b, exp_w_gate, exp_w_up, exp_w_down, ln2_g, ln2_b):
    nb, seq, d = x.shape
    n_meta = meta_tokens.shape[0]
    depth = w_in.shape[0]
    lp = PAD_ROWS + n_meta + seq
    t = nb * lp
    alpha = (2.0 * depth) ** 0.25
    d_inner = ssd_norm_g.shape[-1]
    xbc_w = ssd_conv_w.shape[-1]
    n_heads = ssd_a_log.shape[-1]
    pool_width = pool_scale.shape[-1]
    conf_width = conf_dw_b.shape[-1]
    n_groups = router_group_w.shape[-1]
    n_experts = router_expert_w.shape[-1]
    per_group = n_experts // n_groups
    assert (PAD_ROWS + n_meta) % SSD_CHUNK == 0 and lp % SSD_CHUNK == 0
    assert n_groups + n_experts <= LANES and n_heads <= LANES

    h, hb = _embed_ln(x, meta_tokens.astype(x.dtype), ln_emb_g, ln_emb_b, lp)

    c_dt = d_inner + xbc_w
    c_pc = c_dt + n_heads
    assert c_dt % LANES == 0 and c_dt + LANES <= w_in.shape[-1]
    w_in_t = jnp.swapaxes(w_in, 1, 2)
    rw = jnp.concatenate([router_group_w, router_expert_w], axis=-1)
    rw = jnp.pad(rw, ((0, 0), (0, 0), (0, LANES - rw.shape[-1])))
    rb = jnp.concatenate([router_group_b, router_expert_b], axis=-1)
    rb = jnp.pad(rb, ((0, 0), (0, LANES - rb.shape[-1]))).reshape(depth, 1, LANES)

    for i in range(depth):
        z = _mm(hb, w_in_t, i, 0, d_inner)
        xbc = _mm(hb, w_in_t, i, d_inner, xbc_w)
        dtr = _mm(hb, w_in_t, i, c_dt, LANES)
        pc = _mm(hb, w_in_t, i, c_pc, pool_width + 2 * conf_width)
        f_ssd = _ssd(z, xbc, dtr, ssd_conv_w[i], ssd_conv_b[i], ssd_dt_bias[i], ssd_a_log[i], ssd_d[i],
                     ssd_norm_g[i], nb, lp)
        f_pool = _pool(pc, pool_w[i], pool_scale[i], nb, lp)
        assert pool_width % conf_width == 0
        f_conf = _conf(pc, pool_width // conf_width, conf_dw_w[i], conf_dw_b[i], conf_ln_g[i], conf_ln_b[i], nb, lp)
        merged = _merge(hb, f_ssd, f_pool, f_conf, gate_w, gate_b, ssd_proj, pool_proj, conf_proj, i)
        h, h_packed, ew, ew_t, cnt = _mm_ln(merged, w_out, i, h, ln1_g[i], ln1_b[i], rw[i], rb[i], lp, alpha,
                                            n_groups, per_group)
        slot_tok, plan, slot_c = _dispatch(ew_t, cnt, n_groups, n_experts)
        xg = h_packed.at[slot_tok].get(mode="promise_in_bounds")
        out = _moe_ffn(xg, plan, exp_w_gate, exp_w_up, exp_w_down, i)
        gathered = out.at[slot_c.reshape(-1)].get(mode="promise_in_bounds")
        if i + 1 < depth:
            h, hb = _combine_ln(h, gathered, ew, ln2_g[i], ln2_b[i], lp, alpha)
        else:
            h = _final_ln(h, gathered, ew, ln2_g[i], ln2_b[i], nb, lp, seq, alpha)

    return h.reshape(nb, seq, d)
```

```python
import functools

import jax
import jax.numpy as jnp
from jax import lax
from jax.experimental import pallas as pl
from jax.experimental.pallas import tpu as pltpu

F32 = jnp.float32
BF16 = jnp.bfloat16

PAD_ROWS = 112
SSD_CHUNK = 128
SSD_GROUPS = 4
SSD_HEAD_DIM = 64
POOL_WINDOWS = (2, 4, 8, 16)
MOE_BLOCK = 256
LN_EPS = 1e-5
LANES = 128
NEG_BIG = -1e30
LOG2E = 1.4426950408889634


def _cparams(sem, vmem_mb):
    return pltpu.CompilerParams(dimension_semantics=sem, vmem_limit_bytes=vmem_mb << 20)


def _sigmoid(x):
    return 1.0 / (1.0 + jnp.exp(-x))


def _softplus(x):
    return jnp.maximum(x, 0.0) + jnp.log1p(jnp.exp(-jnp.abs(x)))


def _ln(x, g, b):
    mu = jnp.mean(x, -1, keepdims=True)
    xc = x - mu
    var = jnp.mean(xc * xc, -1, keepdims=True)
    return xc * lax.rsqrt(var + LN_EPS) * g + b


def _pack_halves(y):
    n = y.shape[1] // 2
    lo = lax.bitcast_convert_type(y[:, :n].astype(BF16).astype(F32), jnp.uint32)
    hi = lax.bitcast_convert_type(y[:, n:].astype(BF16).astype(F32), jnp.uint32)
    return lax.bitcast_convert_type(hi | (lo >> 16), F32)


def _unpack_halves(p):
    u = lax.bitcast_convert_type(p, jnp.uint32)
    lo = lax.bitcast_convert_type(u << 16, F32)
    hi = lax.bitcast_convert_type(u & jnp.uint32(0xFFFF0000), F32)
    return jnp.concatenate([lo, hi], axis=1)


def _rows_in_batch(blk, bm, lp):
    base = lax.rem(blk, lp // bm) * bm
    return base + lax.broadcasted_iota(jnp.int32, (bm, 1), 0)


def _pick(n, cands):
    for c in cands:
        if n % c == 0:
            return c
    raise ValueError(f"no block size for {n} among {cands}")


def _embed_kernel(x_ref, meta_ref, g_ref, b_ref, h_ref, hb_ref):
    j = pl.program_id(1)
    head, d = h_ref.shape
    n_meta = meta_ref.shape[0]

    @pl.when(j == 0)
    def _():
        y = _ln(meta_ref[...], g_ref[...], b_ref[...])
        h_ref[0:head - n_meta, :] = jnp.zeros((head - n_meta, d), F32)
        hb_ref[0:head - n_meta, :] = jnp.zeros((head - n_meta, d), BF16)
        h_ref[head - n_meta:head, :] = y
        hb_ref[head - n_meta:head, :] = y.astype(BF16)

    @pl.when(j > 0)
    def _():
        y = _ln(x_ref[...], g_ref[...], b_ref[...])
        h_ref[...] = y
        hb_ref[...] = y.astype(BF16)


def _embed_ln(x, meta, g, b, lp):
    nb, seq, d = x.shape
    n_meta = meta.shape[0]
    head = lp - seq
    assert head == PAD_ROWS + n_meta and seq % head == 0 and n_meta % 16 == 0
    per = lp // head
    vec = pl.BlockSpec((1, d), lambda bi, j: (0, 0))
    row = pl.BlockSpec((head, d), lambda bi, j: (bi * per + j, 0))
    return pl.pallas_call(
        _embed_kernel,
        grid=(nb, per),
        in_specs=[pl.BlockSpec((None, head, d), lambda bi, j: (bi, jnp.maximum(j - 1, 0), 0)),
                  pl.BlockSpec((n_meta, d), lambda bi, j: (0, 0)), vec, vec],
        out_specs=[row, row],
        out_shape=[jax.ShapeDtypeStruct((nb * lp, d), F32), jax.ShapeDtypeStruct((nb * lp, d), BF16)],
        compiler_params=_cparams(("parallel", "arbitrary"), 32),
        name="embed_ln",
    )(x, meta, g.reshape(1, d), b.reshape(1, d))


def _combine_ln_kernel(h_ref, g0_ref, g1_ref, ew_ref, g_ref, b_ref, o_ref, ob_ref, *, bm, lp, alpha):
    y = ew_ref[:, 0:1] * _unpack_halves(g0_ref[...]) + ew_ref[:, 1:2] * _unpack_halves(g1_ref[...])
    y = _ln(alpha * h_ref[...] + y, g_ref[...], b_ref[...])
    y = jnp.where(_rows_in_batch(pl.program_id(0), bm, lp) >= PAD_ROWS, y, 0.0)
    o_ref[...] = y
    ob_ref[...] = y.astype(BF16)


def _combine_ln(h, g0, g1, ew, g, b, lp, alpha):
    t, d = h.shape
    bm = _pick(lp, (528, 384, 192, 128, 64, 8))
    row = pl.BlockSpec((bm, d), lambda i: (i, 0))
    half = pl.BlockSpec((bm, d // 2), lambda i: (i, 0))
    vec = pl.BlockSpec((1, d), lambda i: (0, 0))
    return pl.pallas_call(
        functools.partial(_combine_ln_kernel, bm=bm, lp=lp, alpha=alpha),
        grid=(t // bm,),
        in_specs=[row, half, half, pl.BlockSpec((bm, LANES), lambda i: (i, 0)), vec, vec],
        out_specs=[row, row],
        out_shape=[jax.ShapeDtypeStruct((t, d), F32), jax.ShapeDtypeStruct((t, d), BF16)],
        compiler_params=_cparams(("parallel",), 56),
        name="combine_ln",
    )(h, g0, g1, ew, g.reshape(1, d), b.reshape(1, d))


def _final_ln_kernel(h_ref, g0_ref, g1_ref, ew_ref, g_ref, b_ref, o_ref, *, alpha):
    y = ew_ref[:, 0:1] * _unpack_halves(g0_ref[...]) + ew_ref[:, 1:2] * _unpack_halves(g1_ref[...])
    o_ref[...] = _ln(alpha * h_ref[...] + y, g_ref[...], b_ref[...])


def _final_ln(h, g0, g1, ew, g, b, nb, lp, seq, alpha):
    d = h.shape[1]
    head = lp - seq
    bm = _pick(head, (128, 64, 8))
    assert seq % bm == 0
    per_seq, per_lp, skip = seq // bm, lp // bm, head // bm
    src = lambda i: ((i // per_seq) * per_lp + skip + i % per_seq, 0)
    row = pl.BlockSpec((bm, d), src)
    half = pl.BlockSpec((bm, d // 2), src)
    vec = pl.BlockSpec((1, d), lambda i: (0, 0))
    return pl.pallas_call(
        functools.partial(_final_ln_kernel, alpha=alpha),
        grid=(nb * per_seq,),
        in_specs=[row, half, half, pl.BlockSpec((bm, LANES), src), vec, vec],
        out_specs=pl.BlockSpec((bm, d), lambda i: (i, 0)),
        out_shape=jax.ShapeDtypeStruct((nb * seq, d), F32),
        compiler_params=_cparams(("parallel",), 32),
        name="final_ln",
    )(h, g0, g1, ew, g.reshape(1, d), b.reshape(1, d))


def _mm_kernel(x_ref, w_ref, *rest, shift):
    if shift:
        wnext_ref, o_ref, wbf_ref = rest
    else:
        o_ref, wbf_ref = rest
    bn = w_ref.shape[0]
    step = min(bn, 256)

    @pl.when(pl.program_id(1) == 0)
    def _():
        for c in range(bn // step):
            lo, hi = shift + c * step, shift + (c + 1) * step
            if hi <= bn:
                rows = w_ref[lo:hi, :]
            else:
                rows = jnp.concatenate([w_ref[lo:bn, :], wnext_ref[0:hi - bn, :]], axis=0)
            wbf_ref[:, c * step:(c + 1) * step] = rows.T.astype(BF16)

    o_ref[...] = jnp.dot(x_ref[...], wbf_ref[...], preferred_element_type=F32).astype(o_ref.dtype)


def _mm(x, w_t, layer, col0, n_cols, out_dtype=F32):
    t, k = x.shape
    bn = _pick(n_cols, (1024, 512, 256, 128))
    bm = _pick(t, (1056, 528, 384, 192, 128, 64, 8))
    shift = col0 % LANES
    base = col0 - shift
    assert base % bn == 0 and shift % 8 == 0
    cb0 = base // bn
    in_specs = [pl.BlockSpec((bm, k), lambda n, m: (m, 0)),
                pl.BlockSpec((None, bn, k), lambda n, m: (layer, cb0 + n, 0))]
    args = [x, w_t]
    if shift:
        tiles = bn // LANES
        in_specs.append(pl.BlockSpec((None, LANES, k), lambda n, m: (layer, (cb0 + n + 1) * tiles, 0)))
        args.append(w_t)
    return pl.pallas_call(
        functools.partial(_mm_kernel, shift=shift),
        grid=(n_cols // bn, t // bm),
        in_specs=in_specs,
        out_specs=pl.BlockSpec((bm, bn), lambda n, m: (m, n)),
        out_shape=jax.ShapeDtypeStruct((t, n_cols), out_dtype),
        scratch_shapes=[pltpu.VMEM((k, bn), BF16)],
        compiler_params=_cparams(("arbitrary", "arbitrary"), 48),
        name="mm",
    )(*args)


def _mm_ln_kernel(x_ref, w_ref, h_ref, g_ref, b_ref, rw_ref, rb_ref, o_ref, op_ref, ew_ref, ewt_ref, cnt_ref,
                  wbf_ref, run, *, bm, lp, alpha, n_groups, per_group):
    @pl.when(pl.program_id(0) == 0)
    def _():
        wbf_ref[...] = w_ref[...].astype(BF16)
        run[...] = jnp.zeros(run.shape, F32)

    y = jnp.dot(x_ref[...], wbf_ref[...], preferred_element_type=F32)
    y = _ln(alpha * h_ref[...] + y, g_ref[...], b_ref[...])
    real = _rows_in_batch(pl.program_id(0), bm, lp) >= PAD_ROWS
    y = jnp.where(real, y, 0.0)
    o_ref[...] = y
    op_ref[...] = _pack_halves(y)
    routed = _route_rows(y, rw_ref[...], rb_ref[...], run, real, n_groups, per_group)
    ew_ref[...] = routed
    ewt_ref[...] = routed.T[0:8, :]
    cnt_ref[...] = run[...]


def _mm_ln(x, w, layer, h, g, b, rw, rb, lp, alpha, n_groups, per_group):
    t, k = x.shape
    d = w.shape[-1]
    bm = _pick(lp, (384, 128))
    row = lambda width: pl.BlockSpec((bm, width), lambda i: (i, 0))
    vec = pl.BlockSpec((1, d), lambda i: (0, 0))
    lanes = pl.BlockSpec((1, LANES), lambda i: (0, 0))
    return pl.pallas_call(
        functools.partial(_mm_ln_kernel, bm=bm, lp=lp, alpha=alpha, n_groups=n_groups, per_group=per_group),
        grid=(t // bm,),
        in_specs=[row(k), pl.BlockSpec((None, k, d), lambda i: (layer, 0, 0), pipeline_mode=pl.Buffered(1)),
                  row(d), vec, vec, pl.BlockSpec((d, LANES), lambda i: (0, 0)), lanes],
        out_specs=[row(d), row(d // 2), row(LANES), pl.BlockSpec((8, bm), lambda i: (0, i)), lanes],
        out_shape=[jax.ShapeDtypeStruct((t, d), F32), jax.ShapeDtypeStruct((t, d // 2), F32),
                   jax.ShapeDtypeStruct((t, LANES), F32), jax.ShapeDtypeStruct((8, t), F32),
                   jax.ShapeDtypeStruct((1, LANES), F32)],
        scratch_shapes=[pltpu.VMEM((k, d), BF16), pltpu.VMEM((1, LANES), F32)],
        compiler_params=_cparams(("arbitrary",), 56),
        name="mm_ln",
    )(x, w, h, g.reshape(1, d), b.reshape(1, d), rw, rb)


def _ssd_kernel(z_ref, xbc_ref, dtr_ref, cw_ref, cb_ref, dtb_ref, alog_ref, dexp_ref, ng_ref, o_ref,
                cbuf, xc, ybuf, st, *, q, d_inner, n_state):
    c = pl.program_id(1)
    xbc_w = cbuf.shape[1]
    kc = cw_ref.shape[0]
    hist = 8

    @pl.when(c == 0)
    def _():
        cbuf[0:hist, :] = jnp.zeros((hist, xbc_w), F32)
        st[...] = jnp.zeros(st.shape, F32)

    cbuf[hist:hist + q, :] = xbc_ref[...]
    cw = 128
    for j in range(xbc_w // cw):
        sl = slice(j * cw, (j + 1) * cw)
        conv = cb_ref[:, sl] + cw_ref[kc - 1:kc, sl] * cbuf[hist:hist + q, sl]
        for k in range(kc - 1):
            off = hist - (kc - 1) + k
            conv = conv + cw_ref[k:k + 1, sl] * cbuf[off:off + q, sl]
        xc[:, sl] = conv * _sigmoid(conv)
    cbuf[0:hist, :] = cbuf[q:q + hist, :]

    row = c * q + lax.broadcasted_iota(jnp.int32, (q, 1), 0)
    dt = _softplus(dtr_ref[...] + dtb_ref[...])
    dt = jnp.where(row >= PAD_ROWS, dt, 0.0)
    adt = dt * (-jnp.exp(alog_ref[...]))
    li = lax.broadcasted_iota(jnp.int32, (q, q), 0)
    si = lax.broadcasted_iota(jnp.int32, (q, q), 1)
    tri = li >= si
    acs = jnp.dot(tri.astype(F32), adt, preferred_element_type=F32,
                  precision=lax.Precision.HIGHEST) * LOG2E
    acs_t = acs.T
    dt_t = dt.T
    e_acs = jnp.exp2(acs)
    last = acs_t[:, q - 1:q]
    w_t = jnp.exp2(last - acs_t) * dt_t
    src_t = acs_t - jnp.log2(dt_t)
    dlast = jnp.exp2(last)
    lane = lax.broadcasted_iota(jnp.int32, (1, LANES), 1)
    lo_half = lane < SSD_HEAD_DIM

    heads_per_group = d_inner // SSD_HEAD_DIM // SSD_GROUPS
    pairs_per_group = heads_per_group // 2
    for g in range(SSD_GROUPS):
        b0 = d_inner + g * n_state
        c0 = d_inner + SSD_GROUPS * n_state + g * n_state
        bg = xc[:, b0:b0 + n_state]
        cg = xc[:, c0:c0 + n_state]
        cbm = lax.dot_general(cg.astype(BF16), bg.astype(BF16), (((1,), (1,)), ((), ())),
                              preferred_element_type=F32)
        bg_t = bg.T.astype(BF16)
        cbm = cbm.astype(BF16)
        cg_b = cg.astype(BF16)
        for j in range(pairs_per_group):
            pair = g * pairs_per_group + j
            h0 = 2 * pair
            xs_pair = xc[:, h0 * SSD_HEAD_DIM:h0 * SSD_HEAD_DIM + LANES]
            st_pair = st[pair]
            lhs_y, lhs_s, rhs_x, rhs_st = [], [], [], []
            for u in range(2):
                h = h0 + u
                col = acs[:, h:h + 1]
                rw = src_t[h:h + 1, :]
                decay = jnp.where(tri, jnp.exp2(col - rw), 0.0)
                lhs_y.append(cbm * decay.astype(BF16))
                lhs_s.append(bg_t * w_t[h:h + 1, :].astype(BF16))
                keep = lo_half if u == 0 else jnp.logical_not(lo_half)
                rhs_x.append(jnp.where(keep, xs_pair, 0.0).astype(BF16))
                rhs_st.append(jnp.where(keep, st_pair, 0.0).astype(BF16))
            for u in range(2):
                lhs_y.append(e_acs[:, h0 + u:h0 + u + 1].astype(BF16) * cg_b)
            y_pair = jnp.dot(jnp.concatenate(lhs_y, axis=1), jnp.concatenate(rhs_x + rhs_st, axis=0),
                             preferred_element_type=F32)
            s_new = jnp.dot(jnp.concatenate(lhs_s, axis=1), jnp.concatenate(rhs_x, axis=0),
                            preferred_element_type=F32)
            dpair = jnp.where(lo_half, dlast[h0:h0 + 1, :], dlast[h0 + 1:h0 + 2, :])
            st[pair] = st_pair * dpair + s_new
            ybuf[:, h0 * SSD_HEAD_DIM:h0 * SSD_HEAD_DIM + LANES] = y_pair

    gw = d_inner // SSD_GROUPS
    for g in range(SSD_GROUPS):
        sl = slice(g * gw, (g + 1) * gw)
        y = ybuf[:, sl] + xc[:, sl] * dexp_ref[:, sl]
        zz = z_ref[:, sl]
        y = y * (zz * _sigmoid(zz))
        y = y * lax.rsqrt(jnp.mean(y * y, -1, keepdims=True) + LN_EPS)
        o_ref[:, sl] = (y * ng_ref[:, sl]).astype(BF16)


def _ssd(z, xbc, dtr, conv_w, conv_b, dt_bias, a_log, d_skip, norm_g, nb, lp):
    t, d_inner = z.shape
    xbc_w = xbc.shape[1]
    q = SSD_CHUNK
    n_heads = d_inner // SSD_HEAD_DIM
    n_state = (xbc_w - d_inner) // (2 * SSD_GROUPS)
    assert n_state == LANES and n_heads % (2 * SSD_GROUPS) == 0 and lp % q == 0
    assert xbc_w % 512 == 0 and PAD_ROWS % 8 == 0
    nc = lp // q
    padh = LANES - n_heads
    row = lambda w: pl.BlockSpec((q, w), lambda b, c: (b * nc + c, 0))
    full = lambda a: pl.BlockSpec(a.shape, lambda b, c: (0,) * a.ndim)
    dtb = jnp.pad(dt_bias, (0, padh)).reshape(1, LANES)
    alog = jnp.pad(a_log, (0, padh)).reshape(1, LANES)
    dexp = jnp.repeat(d_skip, SSD_HEAD_DIM).reshape(1, d_inner)
    cb = conv_b.reshape(1, xbc_w)
    ng = norm_g.reshape(1, d_inner)
    return pl.pallas_call(
        functools.partial(_ssd_kernel, q=q, d_inner=d_inner, n_state=n_state),
        grid=(nb, nc),
        in_specs=[row(d_inner), row(xbc_w), row(LANES), full(conv_w), full(cb), full(dtb), full(alog),
                  full(dexp), full(ng)],
        out_specs=row(d_inner),
        out_shape=jax.ShapeDtypeStruct((t, d_inner), BF16),
        scratch_shapes=[pltpu.VMEM((8 + q, xbc_w), F32), pltpu.VMEM((q, xbc_w), F32),
                        pltpu.VMEM((q, d_inner), F32), pltpu.VMEM((n_heads // 2, n_state, LANES), F32)],
        compiler_params=_cparams(("arbitrary", "arbitrary"), 32),
        name="ssd",
    )(z, xbc, dtr, conv_w, cb, dtb, alog, dexp, ng)


def _pool_kernel(u_ref, w_ref, sc_ref, o_ref, buf, *, r, gd):
    c = pl.program_id(1)
    hist = 16
    width = buf.shape[1]

    @pl.when(c == 0)
    def _():
        buf[0:hist, :] = jnp.zeros((hist, width), F32)

    buf[hist:hist + r, :] = u_ref[...]
    pos = c * r + lax.broadcasted_iota(jnp.int32, (r, 1), 0) - PAD_ROWS
    for gi, win in enumerate(POOL_WINDOWS):
        sl = slice(gi * gd, (gi + 1) * gd)
        u = buf[hist:hist + r, sl]
        s = u
        for j in range(1, win):
            s = s + buf[hist - j:hist - j + r, sl]
        cnt = jnp.clip(pos + 1, 1, win).astype(F32)
        mixed = s / cnt - u
        y = jnp.dot(mixed.astype(BF16), w_ref[gi].astype(BF16), preferred_element_type=F32)
        o_ref[:, sl] = (y * sc_ref[:, sl]).astype(BF16)
    buf[0:hist, :] = buf[r:r + hist, :]


def _pool(pc, pool_w, scale, nb, lp):
    t = pc.shape[0]
    ng, gd, _ = pool_w.shape
    width = ng * gd
    assert ng == len(POOL_WINDOWS) and gd % LANES == 0
    r = _pick(lp, (384, 192, 128, 64))
    nc = lp // r
    return pl.pallas_call(
        functools.partial(_pool_kernel, r=r, gd=gd),
        grid=(nb, nc),
        in_specs=[pl.BlockSpec((r, width), lambda b, c: (b * nc + c, 0)),
                  pl.BlockSpec(pool_w.shape, lambda b, c: (0, 0, 0)),
                  pl.BlockSpec((1, width), lambda b, c: (0, 0))],
        out_specs=pl.BlockSpec((r, width), lambda b, c: (b * nc + c, 0)),
        out_shape=jax.ShapeDtypeStruct((t, width), BF16),
        scratch_shapes=[pltpu.VMEM((16 + r, width), F32)],
        compiler_params=_cparams(("arbitrary", "arbitrary"), 32),
        name="pool",
    )(pc, pool_w, scale.reshape(1, width))


def _conf_kernel(a_ref, g_ref, w_ref, b_ref, lg_ref, lb_ref, o_ref, vbuf, cv, ubuf, *, r):
    c = pl.program_id(1)
    hist = 32
    sub = 8
    width = vbuf.shape[1]
    kc = w_ref.shape[0]

    @pl.when(c == 0)
    def _():
        vbuf[0:hist, :] = jnp.zeros((hist, width), F32)

    vbuf[hist:hist + r, :] = a_ref[...] * _sigmoid(g_ref[...])
    ext = r + sub
    cw = ubuf.shape[2]
    for jc in range(width // cw):
        sl = slice(jc * cw, (jc + 1) * cw)
        acc = None
        for rr in range(sub):
            u = None
            for j in range((kc - 1 - rr) // sub + 1):
                k = kc - 1 - (sub * j + rr)
                lo = hist - sub - sub * j
                term = w_ref[k:k + 1, sl] * vbuf[lo:lo + ext, sl]
                u = term if u is None else u + term
            if rr == 0:
                acc = b_ref[:, sl] + u[sub:sub + r]
            else:
                ubuf[rr - 1] = u
                acc = acc + ubuf[rr - 1, sub - rr:sub - rr + r, :]
        cv[:, sl] = acc
    vbuf[0:hist, :] = vbuf[r:r + hist, :]
    v = _ln(cv[...], lg_ref[...], lb_ref[...])
    o_ref[...] = (v * _sigmoid(v)).astype(BF16)


def _conf(pc, col_blk, dw_w, dw_b, ln_g, ln_b, nb, lp):
    t = pc.shape[0]
    kc, width = dw_w.shape
    assert kc <= 33 and width % 256 == 0
    r = _pick(lp, (192, 128, 64))
    nc = lp // r
    vec = lambda: pl.BlockSpec((1, width), lambda b, c: (0, 0))
    return pl.pallas_call(
        functools.partial(_conf_kernel, r=r),
        grid=(nb, nc),
        in_specs=[pl.BlockSpec((r, width), lambda b, c: (b * nc + c, col_blk)),
                  pl.BlockSpec((r, width), lambda b, c: (b * nc + c, col_blk + 1)),
                  pl.BlockSpec((kc, width), lambda b, c: (0, 0)), vec(), vec(), vec()],
        out_specs=pl.BlockSpec((r, width), lambda b, c: (b * nc + c, 0)),
        out_shape=jax.ShapeDtypeStruct((t, width), BF16),
        scratch_shapes=[pltpu.VMEM((32 + r, width), F32), pltpu.VMEM((r, width), F32),
                        pltpu.VMEM((7, r + 8, LANES), F32)],
        compiler_params=_cparams(("arbitrary", "arbitrary"), 32),
        name="conf",
    )(pc, pc, dw_w, dw_b.reshape(1, width), ln_g.reshape(1, width), ln_b.reshape(1, width))


def _merge_kernel(hb_ref, f0_ref, f1_ref, f2_ref, g0_ref, g1_ref, g2_ref, gb_ref, p0_ref, p1_ref, p2_ref,
                  o_ref, gs, s0, s1, s2):
    @pl.when(pl.program_id(1) == 0)
    def _():
        gs[0] = g0_ref[...].astype(BF16)
        gs[1] = g1_ref[...].astype(BF16)
        gs[2] = g2_ref[...].astype(BF16)
        s0[...] = p0_ref[...].astype(BF16)
        s1[...] = p1_ref[...].astype(BF16)
        s2[...] = p2_ref[...].astype(BF16)

    hb = hb_ref[...]
    acc = None
    for j, (f_ref, s_ref) in enumerate(((f0_ref, s0), (f1_ref, s1), (f2_ref, s2))):
        gate = _sigmoid(jnp.dot(hb, gs[j], preferred_element_type=F32) + gb_ref[j:j + 1, :])
        term = gate * jnp.dot(f_ref[...], s_ref[...], preferred_element_type=F32)
        acc = term if acc is None else acc + term
    o_ref[...] = acc.astype(BF16)


def _merge(hb, f_ssd, f_pool, f_conf, gate_w, gate_b, ssd_proj, pool_proj, conf_proj, layer):
    t, d = hb.shape
    bn = _pick(d, (512, 256, 128))
    bm = _pick(t, (528, 384, 192, 128, 64, 8))
    nbn = d // bn
    once = pl.Buffered(1)
    rowspec = lambda a: pl.BlockSpec((bm, a.shape[1]), lambda n, m: (m, 0))
    gspec = lambda j: pl.BlockSpec((None, d, bn), lambda n, m: (layer, 0, j * nbn + n), pipeline_mode=once)
    pspec = lambda w: pl.BlockSpec((None, w.shape[1], bn), lambda n, m: (layer, 0, n))
    gb = gate_b[layer].reshape(3, d)
    return pl.pallas_call(
        _merge_kernel,
        grid=(nbn, t // bm),
        in_specs=[rowspec(hb), rowspec(f_ssd), rowspec(f_pool), rowspec(f_conf),
                  gspec(0), gspec(1), gspec(2), pl.BlockSpec((3, bn), lambda n, m: (0, n)),
                  pspec(ssd_proj), pspec(pool_proj), pspec(conf_proj)],
        out_specs=pl.BlockSpec((bm, bn), lambda n, m: (m, n)),
        out_shape=jax.ShapeDtypeStruct((t, d), BF16),
        scratch_shapes=[pltpu.VMEM((3, d, bn), BF16), pltpu.VMEM((ssd_proj.shape[1], bn), BF16),
                        pltpu.VMEM((pool_proj.shape[1], bn), BF16), pltpu.VMEM((conf_proj.shape[1], bn), BF16)],
        compiler_params=_cparams(("arbitrary", "arbitrary"), 58),
        name="merge",
    )(hb, f_ssd, f_pool, f_conf, gate_w, gate_w, gate_w, gb, ssd_proj, pool_proj, conf_proj)


def _route_rows(h, w, bias, run, real, n_groups, per_group):
    bm = h.shape[0]
    h_hi = h.astype(BF16)
    h_lo = (h - h_hi.astype(F32)).astype(BF16)
    w_hi = w.astype(BF16)
    w_lo = (w - w_hi.astype(F32)).astype(BF16)
    logits = (jnp.dot(h_hi, w_hi, preferred_element_type=F32)
              + (jnp.dot(h_hi, w_lo, preferred_element_type=F32)
                 + jnp.dot(h_lo, w_hi, preferred_element_type=F32))) + bias
    lane = lax.broadcasted_iota(jnp.int32, logits.shape, 1).astype(F32)

    def first_max(vals):
        m = jnp.max(vals, axis=1, keepdims=True)
        idx = jnp.min(jnp.where(vals == m, lane, float(LANES)), axis=1, keepdims=True)
        return m, idx

    gmask = lane < n_groups
    gmax, gsel = first_max(jnp.where(gmask, logits, NEG_BIG))
    gsum = jnp.sum(jnp.where(gmask, jnp.exp(logits - gmax), 0.0), axis=1, keepdims=True)
    p_group = 1.0 / gsum
    lo = n_groups + per_group * gsel
    el = jnp.where(jnp.logical_and(lane >= lo, lane < lo + per_group), logits, NEG_BIG)
    m1, i1 = first_max(el)
    m2, i2 = first_max(jnp.where(lane == i1, NEG_BIG, el))
    ratio = jnp.exp(m2 - m1)
    w1 = p_group / (1.0 + ratio)
    w2 = w1 * ratio
    sentinel = float(n_groups * per_group)
    e1 = jnp.where(real, i1 - n_groups, sentinel)
    e2 = jnp.where(real, i2 - n_groups, sentinel)
    w1 = jnp.where(real, w1, 0.0)
    w2 = jnp.where(real, w2, 0.0)
    hit1 = jnp.logical_and(lane == i1, real)
    hit2 = jnp.logical_and(lane == i2, real)
    onehot = jnp.where(jnp.logical_or(hit1, hit2), 1.0, 0.0)
    li = lax.broadcasted_iota(jnp.int32, (bm, bm), 0)
    si = lax.broadcasted_iota(jnp.int32, (bm, bm), 1)
    before = jnp.where(li > si, 1.0, 0.0).astype(BF16)
    seen = jnp.dot(before, onehot.astype(BF16), preferred_element_type=F32) + run[...]
    r1 = jnp.sum(jnp.where(hit1, seen, 0.0), axis=1, keepdims=True)
    r2 = jnp.sum(jnp.where(hit2, seen, 0.0), axis=1, keepdims=True)
    run[...] = run[...] + jnp.sum(onehot, axis=0, keepdims=True)
    out = jnp.zeros(logits.shape, F32)
    for k, v in enumerate((w1, w2, e1, e2, r1, r2)):
        out = jnp.where(lane == k, v, out)
    return out


def _moe_kernel(be_ref, nxt1_ref, nxt2_ref, par_ref, tot_ref, tok_ref, x_hbm, wg_hbm, wu_hbm, wd_hbm, o_ref,
                wg_f, wu_f, wd_f, wg_s, wu_s, wd_s, sems, xbuf, xsems, *, layer):
    i = pl.program_id(0)
    total = tot_ref[0]
    blk = xbuf.shape[1]

    def row_copy(block, r, buf):
        tok = tok_ref[block * blk + r]
        return pltpu.make_async_copy(x_hbm.at[pl.ds(tok, 1)], xbuf.at[buf, pl.ds(r, 1)], xsems.at[buf])

    def start_rows(block, buf):
        def body(rr, carry):
            for u in range(8):
                row_copy(block, 8 * rr + u, buf).start(priority=u % 2)
            return carry
        lax.fori_loop(0, blk // 8, body, 0)

    def wait_rows(buf):
        pltpu.make_async_copy(xbuf.at[buf], xbuf.at[buf], xsems.at[buf]).wait()

    @pl.when(jnp.logical_and(i == 0, total > 0))
    def _():
        start_rows(0, 0)

    @pl.when(i + 1 < total)
    def _():
        start_rows(i + 1, (i + 1) % 2)
    ic = jnp.maximum(jnp.minimum(i, total - 1), 0)
    e = be_ref[ic]
    e_prev = be_ref[jnp.maximum(ic - 1, 0)]
    active = i < total
    first_of_run = jnp.logical_and(active, jnp.logical_or(i == 0, e != e_prev))
    par = par_ref[ic]

    def weight_copies(expert, st):
        return (pltpu.make_async_copy(wg_hbm.at[layer, expert], wg_f.at[st], sems.at[st, 0]),
                pltpu.make_async_copy(wu_hbm.at[layer, expert], wu_f.at[st], sems.at[st, 1]),
                pltpu.make_async_copy(wd_hbm.at[layer, expert], wd_f.at[st], sems.at[st, 2]))

    @pl.when(jnp.logical_and(active, i == 0))
    def _():
        for cp in weight_copies(e, 0):
            cp.start()
        nxt1 = nxt1_ref[ic]

        @pl.when(nxt1 >= 0)
        def _():
            for cp in weight_copies(nxt1, 1):
                cp.start()

    @pl.when(first_of_run)
    def _():
        for cp, src, dst in zip(weight_copies(e, par), (wg_f, wu_f, wd_f), (wg_s, wu_s, wd_s)):
            cp.wait()
            dst[...] = src[par].astype(BF16)
        nxt2 = nxt2_ref[ic]

        @pl.when(nxt2 >= 0)
        def _():
            for cp in weight_copies(nxt2, par):
                cp.start()

    @pl.when(active)
    def _():
        wait_rows(i % 2)
        x = _unpack_halves(xbuf[i % 2]).astype(BF16)
        gte = jnp.dot(x, wg_s[...], preferred_element_type=F32)
        up = jnp.dot(x, wu_s[...], preferred_element_type=F32)
        hid = (gte * _sigmoid(gte)) * up
        o_ref[...] = _pack_halves(jnp.dot(hid.astype(BF16), wd_s[...], preferred_element_type=F32))

    @pl.when(jnp.logical_not(active))
    def _():
        o_ref[...] = jnp.zeros(o_ref.shape, F32)


def _moe_ffn(x_packed, slot_tok, plan, w_gate, w_up, w_down, layer):
    d = x_packed.shape[1]
    n_slots = slot_tok.shape[0]
    hid = w_gate.shape[-1]
    assert w_gate.shape[-2] == 2 * d
    n_blocks = n_slots // MOE_BLOCK

    hbm = pl.BlockSpec(memory_space=pl.ANY)
    grid_spec = pltpu.PrefetchScalarGridSpec(
        num_scalar_prefetch=6,
        grid=(n_blocks,),
        in_specs=[hbm, hbm, hbm, hbm],
        out_specs=pl.BlockSpec((MOE_BLOCK, d), lambda i, *_: (i, 0)),
        scratch_shapes=[pltpu.VMEM((2, 2 * d, hid), F32), pltpu.VMEM((2, 2 * d, hid), F32),
                        pltpu.VMEM((2, hid, 2 * d), F32),
                        pltpu.VMEM((2 * d, hid), BF16), pltpu.VMEM((2 * d, hid), BF16),
                        pltpu.VMEM((hid, 2 * d), BF16), pltpu.SemaphoreType.DMA((2, 3)),
                        pltpu.VMEM((2, MOE_BLOCK, d), F32), pltpu.SemaphoreType.DMA((2,))],
    )
    return pl.pallas_call(
        functools.partial(_moe_kernel, layer=layer),
        grid_spec=grid_spec,
        out_shape=jax.ShapeDtypeStruct((n_slots, d), F32),
        compiler_params=_cparams(("arbitrary",), 58),
        name="moe_ffn",
    )(*plan, slot_tok, x_packed, w_gate, w_up, w_down)


def _dispatch(ew_t, cnt, n_groups, n_experts):
    t = ew_t.shape[1]
    eid = ew_t[2:4].astype(jnp.int32).reshape(-1)
    rank = ew_t[4:6].astype(jnp.int32).reshape(-1)
    counts = cnt[0, n_groups:n_groups + n_experts].astype(jnp.int32)
    n_assign = eid.shape[0]
    tok = jnp.tile(jnp.arange(t, dtype=jnp.int32), 2)
    padded = (counts + MOE_BLOCK - 1) // MOE_BLOCK * MOE_BLOCK
    pend = jnp.cumsum(padded)
    pstart = pend - padded
    n_blocks = n_assign // MOE_BLOCK + n_experts
    n_slots = n_blocks * MOE_BLOCK
    valid = eid < n_experts
    slot = jnp.where(valid, pstart[jnp.minimum(eid, n_experts - 1)] + rank,
                     n_slots + jnp.arange(n_assign, dtype=jnp.int32))
    slot_tok = (jnp.arange(n_slots, dtype=jnp.int32) % t).at[slot].set(tok, mode="drop", unique_indices=True)
    first_row = jnp.arange(n_blocks, dtype=jnp.int32) * MOE_BLOCK
    block_expert = jnp.minimum(jnp.sum((pend[None, :] <= first_row[:, None]).astype(jnp.int32), axis=1),
                               n_experts - 1)
    total_blocks = (pend[-1] // MOE_BLOCK).astype(jnp.int32).reshape(1)
    ids = jnp.arange(n_experts, dtype=jnp.int32)
    has = counts > 0
    later = jnp.where(has, ids, n_experts)
    nxt1 = lax.cummin(jnp.concatenate([later[1:], jnp.full((1,), n_experts, jnp.int32)]), reverse=True)
    nxt2 = jnp.concatenate([nxt1, jnp.full((1,), n_experts, jnp.int32)])[nxt1]
    nxt1 = jnp.where(nxt1 >= n_experts, -1, nxt1)
    nxt2 = jnp.where(nxt2 >= n_experts, -1, nxt2)
    parity = (jnp.cumsum(has.astype(jnp.int32)) - 1) % 2
    plan = (block_expert, nxt1[block_expert], nxt2[block_expert], parity[block_expert], total_blocks)
    slot_c = jnp.where(valid, slot, 0).reshape(2, t)
    return slot_tok, plan, slot_c


def kernel(x, meta_tokens, ln_emb_g, ln_emb_b, w_in, ssd_conv_w, ssd_conv_b, ssd_dt_bias, ssd_a_log, ssd_d,
           ssd_norm_g, ssd_proj, pool_w, pool_scale, pool_proj, conf_dw_w, conf_dw_b, conf_ln_g, conf_ln_b,
           conf_proj, gate_w, gate_b, w_out, ln1_g, ln1_b, router_group_w, router_group_b, router_expert_w,
           router_expert_b, exp_w_gate, exp_w_up, exp_w_down, ln2_g, ln2_b):
    nb, seq, d = x.shape
    n_meta = meta_tokens.shape[0]
    depth = w_in.shape[0]
    lp = PAD_ROWS + n_meta + seq
    t = nb * lp
    alpha = (2.0 * depth) ** 0.25
    d_inner = ssd_norm_g.shape[-1]
    xbc_w = ssd_conv_w.shape[-1]
    n_heads = ssd_a_log.shape[-1]
    pool_width = pool_scale.shape[-1]
    conf_width = conf_dw_b.shape[-1]
    n_groups = router_group_w.shape[-1]
    n_experts = router_expert_w.shape[-1]
    per_group = n_experts // n_groups
    assert (PAD_ROWS + n_meta) % SSD_CHUNK == 0 and lp % SSD_CHUNK == 0
    assert n_groups + n_experts <= LANES and n_heads <= LANES

    h, hb = _embed_ln(x, meta_tokens.astype(x.dtype), ln_emb_g, ln_emb_b, lp)

    c_dt = d_inner + xbc_w
    c_pc = c_dt + n_heads
    assert c_dt % LANES == 0 and c_dt + LANES <= w_in.shape[-1]
    w_in_t = jnp.swapaxes(w_in, 1, 2)
    rw = jnp.concatenate([router_group_w, router_expert_w], axis=-1)
    rw = jnp.pad(rw, ((0, 0), (0, 0), (0, LANES - rw.shape[-1])))
    rb = jnp.concatenate([router_group_b, router_expert_b], axis=-1)
    rb = jnp.pad(rb, ((0, 0), (0, LANES - rb.shape[-1]))).reshape(depth, 1, LANES)

    for i in range(depth):
        z = _mm(hb, w_in_t, i, 0, d_inner)
        xbc = _mm(hb, w_in_t, i, d_inner, xbc_w)
        dtr = _mm(hb, w_in_t, i, c_dt, LANES)
        pc = _mm(hb, w_in_t, i, c_pc, pool_width + 2 * conf_width)
        f_ssd = _ssd(z, xbc, dtr, ssd_conv_w[i], ssd_conv_b[i], ssd_dt_bias[i], ssd_a_log[i], ssd_d[i],
                     ssd_norm_g[i], nb, lp)
        f_pool = _pool(pc, pool_w[i], pool_scale[i], nb, lp)
        assert pool_width % conf_width == 0
        f_conf = _conf(pc, pool_width // conf_width, conf_dw_w[i], conf_dw_b[i], conf_ln_g[i], conf_ln_b[i], nb, lp)
        merged = _merge(hb, f_ssd, f_pool, f_conf, gate_w, gate_b, ssd_proj, pool_proj, conf_proj, i)
        h, h_packed, ew, ew_t, cnt = _mm_ln(merged, w_out, i, h, ln1_g[i], ln1_b[i], rw[i], rb[i], lp, alpha,
                                            n_groups, per_group)
        slot_tok, plan, slot_c = _dispatch(ew_t, cnt, n_groups, n_experts)
        out = _moe_ffn(h_packed, slot_tok, plan, exp_w_gate, exp_w_up, exp_w_down, i)
        g0 = out.at[slot_c[0]].get(mode="promise_in_bounds")
        g1 = out.at[slot_c[1]].get(mode="promise_in_bounds")
        if i + 1 < depth:
            h, hb = _combine_ln(h, g0, g1, ew, ln2_g[i], ln2_b[i], lp, alpha)
        else:
            h = _final_ln(h, g0, g1, ew, ln2_g[i], ln2_b[i], nb, lp, seq, alpha)

    return h.reshape(nb, seq, d)
```

```python
import functools

import jax
import jax.numpy as jnp
from jax import lax
from jax.experimental import pallas as pl
from jax.experimental.pallas import tpu as pltpu

F32 = jnp.float32
BF16 = jnp.bfloat16

PAD_ROWS = 112
SSD_CHUNK = 128
SSD_GROUPS = 4
SSD_HEAD_DIM = 64
POOL_WINDOWS = (2, 4, 8, 16)
MOE_BLOCK = 256
LN_EPS = 1e-5
LANES = 128
NEG_BIG = -1e30
LOG2E = 1.4426950408889634


def _cparams(sem, vmem_mb):
    return pltpu.CompilerParams(dimension_semantics=sem, vmem_limit_bytes=vmem_mb << 20)


def _sigmoid(x):
    return 1.0 / (1.0 + jnp.exp(-x))


def _softplus(x):
    return jnp.maximum(x, 0.0) + jnp.log1p(jnp.exp(-jnp.abs(x)))


def _ln(x, g, b):
    mu = jnp.mean(x, -1, keepdims=True)
    xc = x - mu
    var = jnp.mean(xc * xc, -1, keepdims=True)
    return xc * lax.rsqrt(var + LN_EPS) * g + b


def _pack_halves(y):
    n = y.shape[1] // 2
    lo = lax.bitcast_convert_type(y[:, :n].astype(BF16).astype(F32), jnp.uint32)
    hi = lax.bitcast_convert_type(y[:, n:].astype(BF16).astype(F32), jnp.uint32)
    return lax.bitcast_convert_type(hi | (lo >> 16), F32)


def _unpack_halves(p):
    u = lax.bitcast_convert_type(p, jnp.uint32)
    lo = lax.bitcast_convert_type(u << 16, F32)
    hi = lax.bitcast_convert_type(u & jnp.uint32(0xFFFF0000), F32)
    return jnp.concatenate([lo, hi], axis=1)


def _rows_in_batch(blk, bm, lp):
    base = lax.rem(blk, lp // bm) * bm
    return base + lax.broadcasted_iota(jnp.int32, (bm, 1), 0)


def _pick(n, cands):
    for c in cands:
        if n % c == 0:
            return c
    raise ValueError(f"no block size for {n} among {cands}")


def _embed_kernel(x_ref, meta_ref, g_ref, b_ref, h_ref, hb_ref):
    j = pl.program_id(1)
    head, d = h_ref.shape
    n_meta = meta_ref.shape[0]

    @pl.when(j == 0)
    def _():
        y = _ln(meta_ref[...], g_ref[...], b_ref[...])
        h_ref[0:head - n_meta, :] = jnp.zeros((head - n_meta, d), F32)
        hb_ref[0:head - n_meta, :] = jnp.zeros((head - n_meta, d), BF16)
        h_ref[head - n_meta:head, :] = y
        hb_ref[head - n_meta:head, :] = y.astype(BF16)

    @pl.when(j > 0)
    def _():
        y = _ln(x_ref[...], g_ref[...], b_ref[...])
        h_ref[...] = y
        hb_ref[...] = y.astype(BF16)


def _embed_ln(x, meta, g, b, lp):
    nb, seq, d = x.shape
    n_meta = meta.shape[0]
    head = lp - seq
    assert head == PAD_ROWS + n_meta and seq % head == 0 and n_meta % 16 == 0
    per = lp // head
    vec = pl.BlockSpec((1, d), lambda bi, j: (0, 0))
    row = pl.BlockSpec((head, d), lambda bi, j: (bi * per + j, 0))
    return pl.pallas_call(
        _embed_kernel,
        grid=(nb, per),
        in_specs=[pl.BlockSpec((None, head, d), lambda bi, j: (bi, jnp.maximum(j - 1, 0), 0)),
                  pl.BlockSpec((n_meta, d), lambda bi, j: (0, 0)), vec, vec],
        out_specs=[row, row],
        out_shape=[jax.ShapeDtypeStruct((nb * lp, d), F32), jax.ShapeDtypeStruct((nb * lp, d), BF16)],
        compiler_params=_cparams(("parallel", "arbitrary"), 32),
        name="embed_ln",
    )(x, meta, g.reshape(1, d), b.reshape(1, d))


def _combine_ln_kernel(h_ref, g0_ref, g1_ref, ew_ref, g_ref, b_ref, o_ref, ob_ref, *, bm, lp, alpha):
    y = ew_ref[:, 0:1] * _unpack_halves(g0_ref[...]) + ew_ref[:, 1:2] * _unpack_halves(g1_ref[...])
    y = _ln(alpha * h_ref[...] + y, g_ref[...], b_ref[...])
    y = jnp.where(_rows_in_batch(pl.program_id(0), bm, lp) >= PAD_ROWS, y, 0.0)
    o_ref[...] = y
    ob_ref[...] = y.astype(BF16)


def _combine_ln(h, g0, g1, ew, g, b, lp, alpha):
    t, d = h.shape
    bm = _pick(lp, (528, 384, 192, 128, 64, 8))
    row = pl.BlockSpec((bm, d), lambda i: (i, 0))
    half = pl.BlockSpec((bm, d // 2), lambda i: (i, 0))
    vec = pl.BlockSpec((1, d), lambda i: (0, 0))
    return pl.pallas_call(
        functools.partial(_combine_ln_kernel, bm=bm, lp=lp, alpha=alpha),
        grid=(t // bm,),
        in_specs=[row, half, half, pl.BlockSpec((bm, LANES), lambda i: (i, 0)), vec, vec],
        out_specs=[row, row],
        out_shape=[jax.ShapeDtypeStruct((t, d), F32), jax.ShapeDtypeStruct((t, d), BF16)],
        compiler_params=_cparams(("parallel",), 56),
        name="combine_ln",
    )(h, g0, g1, ew, g.reshape(1, d), b.reshape(1, d))


def _final_ln_kernel(h_ref, g0_ref, g1_ref, ew_ref, g_ref, b_ref, o_ref, *, alpha):
    y = ew_ref[:, 0:1] * _unpack_halves(g0_ref[...]) + ew_ref[:, 1:2] * _unpack_halves(g1_ref[...])
    o_ref[...] = _ln(alpha * h_ref[...] + y, g_ref[...], b_ref[...])


def _final_ln(h, g0, g1, ew, g, b, nb, lp, seq, alpha):
    d = h.shape[1]
    head = lp - seq
    bm = _pick(head, (128, 64, 8))
    assert seq % bm == 0
    per_seq, per_lp, skip = seq // bm, lp // bm, head // bm
    src = lambda i: ((i // per_seq) * per_lp + skip + i % per_seq, 0)
    row = pl.BlockSpec((bm, d), src)
    half = pl.BlockSpec((bm, d // 2), src)
    vec = pl.BlockSpec((1, d), lambda i: (0, 0))
    return pl.pallas_call(
        functools.partial(_final_ln_kernel, alpha=alpha),
        grid=(nb * per_seq,),
        in_specs=[row, half, half, pl.BlockSpec((bm, LANES), src), vec, vec],
        out_specs=pl.BlockSpec((bm, d), lambda i: (i, 0)),
        out_shape=jax.ShapeDtypeStruct((nb * seq, d), F32),
        compiler_params=_cparams(("parallel",), 32),
        name="final_ln",
    )(h, g0, g1, ew, g.reshape(1, d), b.reshape(1, d))


def _mm_kernel(x_ref, w_ref, *rest, shift):
    if shift:
        wnext_ref, o_ref, wbf_ref = rest
    else:
        o_ref, wbf_ref = rest
    bn = w_ref.shape[0]
    step = min(bn, 256)

    @pl.when(pl.program_id(1) == 0)
    def _():
        for c in range(bn // step):
            lo, hi = shift + c * step, shift + (c + 1) * step
            if hi <= bn:
                rows = w_ref[lo:hi, :]
            else:
                rows = jnp.concatenate([w_ref[lo:bn, :], wnext_ref[0:hi - bn, :]], axis=0)
            wbf_ref[:, c * step:(c + 1) * step] = rows.T.astype(BF16)

    o_ref[...] = jnp.dot(x_ref[...], wbf_ref[...], preferred_element_type=F32).astype(o_ref.dtype)


def _mm(x, w_t, layer, col0, n_cols, out_dtype=F32):
    t, k = x.shape
    bn = _pick(n_cols, (1024, 512, 256, 128))
    bm = _pick(t, (1056, 528, 384, 192, 128, 64, 8))
    shift = col0 % LANES
    base = col0 - shift
    assert base % bn == 0 and shift % 8 == 0
    cb0 = base // bn
    in_specs = [pl.BlockSpec((bm, k), lambda n, m: (m, 0)),
                pl.BlockSpec((None, bn, k), lambda n, m: (layer, cb0 + n, 0))]
    args = [x, w_t]
    if shift:
        tiles = bn // LANES
        in_specs.append(pl.BlockSpec((None, LANES, k), lambda n, m: (layer, (cb0 + n + 1) * tiles, 0)))
        args.append(w_t)
    return pl.pallas_call(
        functools.partial(_mm_kernel, shift=shift),
        grid=(n_cols // bn, t // bm),
        in_specs=in_specs,
        out_specs=pl.BlockSpec((bm, bn), lambda n, m: (m, n)),
        out_shape=jax.ShapeDtypeStruct((t, n_cols), out_dtype),
        scratch_shapes=[pltpu.VMEM((k, bn), BF16)],
        compiler_params=_cparams(("arbitrary", "arbitrary"), 48),
        name="mm",
    )(*args)


def _mm_ln_kernel(x_ref, w_ref, h_ref, g_ref, b_ref, rw_ref, rb_ref, o_ref, op_ref, ew_ref, ewt_ref, cnt_ref,
                  wbf_ref, run, *, bm, lp, alpha, n_groups, per_group):
    @pl.when(pl.program_id(0) == 0)
    def _():
        wbf_ref[...] = w_ref[...].astype(BF16)
        run[...] = jnp.zeros(run.shape, F32)

    y = jnp.dot(x_ref[...], wbf_ref[...], preferred_element_type=F32)
    y = _ln(alpha * h_ref[...] + y, g_ref[...], b_ref[...])
    real = _rows_in_batch(pl.program_id(0), bm, lp) >= PAD_ROWS
    y = jnp.where(real, y, 0.0)
    o_ref[...] = y
    op_ref[...] = _pack_halves(y)
    routed = _route_rows(y, rw_ref[...], rb_ref[...], run, real, n_groups, per_group)
    ew_ref[...] = routed
    ewt_ref[...] = routed.T[0:8, :]
    cnt_ref[...] = run[...]


def _mm_ln(x, w, layer, h, g, b, rw, rb, lp, alpha, n_groups, per_group):
    t, k = x.shape
    d = w.shape[-1]
    bm = _pick(lp, (384, 128))
    row = lambda width: pl.BlockSpec((bm, width), lambda i: (i, 0))
    vec = pl.BlockSpec((1, d), lambda i: (0, 0))
    lanes = pl.BlockSpec((1, LANES), lambda i: (0, 0))
    return pl.pallas_call(
        functools.partial(_mm_ln_kernel, bm=bm, lp=lp, alpha=alpha, n_groups=n_groups, per_group=per_group),
        grid=(t // bm,),
        in_specs=[row(k), pl.BlockSpec((None, k, d), lambda i: (layer, 0, 0), pipeline_mode=pl.Buffered(1)),
                  row(d), vec, vec, pl.BlockSpec((d, LANES), lambda i: (0, 0)), lanes],
        out_specs=[row(d), row(d // 2), row(LANES), pl.BlockSpec((8, bm), lambda i: (0, i)), lanes],
        out_shape=[jax.ShapeDtypeStruct((t, d), F32), jax.ShapeDtypeStruct((t, d // 2), F32),
                   jax.ShapeDtypeStruct((t, LANES), F32), jax.ShapeDtypeStruct((8, t), F32),
                   jax.ShapeDtypeStruct((1, LANES), F32)],
        scratch_shapes=[pltpu.VMEM((k, d), BF16), pltpu.VMEM((1, LANES), F32)],
        compiler_params=_cparams(("arbitrary",), 56),
        name="mm_ln",
    )(x, w, h, g.reshape(1, d), b.reshape(1, d), rw, rb)


def _ssd_kernel(z_ref, xbc_ref, dtr_ref, cw_ref, cb_ref, dtb_ref, alog_ref, dexp_ref, ng_ref, o_ref,
                cbuf, xc, ybuf, st, *, q, d_inner, n_state):
    c = pl.program_id(1)
    xbc_w = cbuf.shape[1]
    kc = cw_ref.shape[0]
    hist = 8

    @pl.when(c == 0)
    def _():
        cbuf[0:hist, :] = jnp.zeros((hist, xbc_w), F32)
        st[...] = jnp.zeros(st.shape, F32)

    cbuf[hist:hist + q, :] = xbc_ref[...]
    cw = 128
    for j in range(xbc_w // cw):
        sl = slice(j * cw, (j + 1) * cw)
        conv = cb_ref[:, sl] + cw_ref[kc - 1:kc, sl] * cbuf[hist:hist + q, sl]
        for k in range(kc - 1):
            off = hist - (kc - 1) + k
            conv = conv + cw_ref[k:k + 1, sl] * cbuf[off:off + q, sl]
        xc[:, sl] = conv * _sigmoid(conv)
    cbuf[0:hist, :] = cbuf[q:q + hist, :]

    row = c * q + lax.broadcasted_iota(jnp.int32, (q, 1), 0)
    dt = _softplus(dtr_ref[...] + dtb_ref[...])
    dt = jnp.where(row >= PAD_ROWS, dt, 0.0)
    adt = dt * (-jnp.exp(alog_ref[...]))
    li = lax.broadcasted_iota(jnp.int32, (q, q), 0)
    si = lax.broadcasted_iota(jnp.int32, (q, q), 1)
    tri = li >= si
    acs = jnp.dot(tri.astype(F32), adt, preferred_element_type=F32,
                  precision=lax.Precision.HIGHEST) * LOG2E
    acs_t = acs.T
    dt_t = dt.T
    e_acs = jnp.exp2(acs)
    last = acs_t[:, q - 1:q]
    w_t = jnp.exp2(last - acs_t) * dt_t
    src_t = acs_t - jnp.log2(dt_t)
    dlast = jnp.exp2(last)
    lane = lax.broadcasted_iota(jnp.int32, (1, LANES), 1)
    lo_half = lane < SSD_HEAD_DIM

    heads_per_group = d_inner // SSD_HEAD_DIM // SSD_GROUPS
    pairs_per_group = heads_per_group // 2
    for g in range(SSD_GROUPS):
        b0 = d_inner + g * n_state
        c0 = d_inner + SSD_GROUPS * n_state + g * n_state
        bg = xc[:, b0:b0 + n_state]
        cg = xc[:, c0:c0 + n_state]
        cbm = lax.dot_general(cg.astype(BF16), bg.astype(BF16), (((1,), (1,)), ((), ())),
                              preferred_element_type=F32)
        bg_t = bg.T.astype(BF16)
        cbm = cbm.astype(BF16)
        cg_b = cg.astype(BF16)
        for j in range(pairs_per_group):
            pair = g * pairs_per_group + j
            h0 = 2 * pair
            xs_pair = xc[:, h0 * SSD_HEAD_DIM:h0 * SSD_HEAD_DIM + LANES]
            st_pair = st[pair]
            lhs_y, lhs_s, rhs_x, rhs_st = [], [], [], []
            for u in range(2):
                h = h0 + u
                col = acs[:, h:h + 1]
                rw = src_t[h:h + 1, :]
                decay = jnp.where(tri, jnp.exp2(col - rw), 0.0)
                lhs_y.append(cbm * decay.astype(BF16))
                lhs_s.append(bg_t * w_t[h:h + 1, :].astype(BF16))
                keep = lo_half if u == 0 else jnp.logical_not(lo_half)
                rhs_x.append(jnp.where(keep, xs_pair, 0.0).astype(BF16))
                rhs_st.append(jnp.where(keep, st_pair, 0.0).astype(BF16))
            for u in range(2):
                lhs_y.append(e_acs[:, h0 + u:h0 + u + 1].astype(BF16) * cg_b)
            y_pair = jnp.dot(jnp.concatenate(lhs_y, axis=1), jnp.concatenate(rhs_x + rhs_st, axis=0),
                             preferred_element_type=F32)
            s_new = jnp.dot(jnp.concatenate(lhs_s, axis=1), jnp.concatenate(rhs_x, axis=0),
                            preferred_element_type=F32)
            dpair = jnp.where(lo_half, dlast[h0:h0 + 1, :], dlast[h0 + 1:h0 + 2, :])
            st[pair] = st_pair * dpair + s_new
            ybuf[:, h0 * SSD_HEAD_DIM:h0 * SSD_HEAD_DIM + LANES] = y_pair

    gw = d_inner // SSD_GROUPS
    for g in range(SSD_GROUPS):
        sl = slice(g * gw, (g + 1) * gw)
        y = ybuf[:, sl] + xc[:, sl] * dexp_ref[:, sl]
        zz = z_ref[:, sl]
        y = y * (zz * _sigmoid(zz))
        y = y * lax.rsqrt(jnp.mean(y * y, -1, keepdims=True) + LN_EPS)
        o_ref[:, sl] = (y * ng_ref[:, sl]).astype(BF16)


def _ssd(z, xbc, dtr, conv_w, conv_b, dt_bias, a_log, d_skip, norm_g, nb, lp):
    t, d_inner = z.shape
    xbc_w = xbc.shape[1]
    q = SSD_CHUNK
    n_heads = d_inner // SSD_HEAD_DIM
    n_state = (xbc_w - d_inner) // (2 * SSD_GROUPS)
    assert n_state == LANES and n_heads % (2 * SSD_GROUPS) == 0 and lp % q == 0
    assert xbc_w % 512 == 0 and PAD_ROWS % 8 == 0
    nc = lp // q
    padh = LANES - n_heads
    row = lambda w: pl.BlockSpec((q, w), lambda b, c: (b * nc + c, 0))
    full = lambda a: pl.BlockSpec(a.shape, lambda b, c: (0,) * a.ndim)
    dtb = jnp.pad(dt_bias, (0, padh)).reshape(1, LANES)
    alog = jnp.pad(a_log, (0, padh)).reshape(1, LANES)
    dexp = jnp.repeat(d_skip, SSD_HEAD_DIM).reshape(1, d_inner)
    cb = conv_b.reshape(1, xbc_w)
    ng = norm_g.reshape(1, d_inner)
    return pl.pallas_call(
        functools.partial(_ssd_kernel, q=q, d_inner=d_inner, n_state=n_state),
        grid=(nb, nc),
        in_specs=[row(d_inner), row(xbc_w), row(LANES), full(conv_w), full(cb), full(dtb), full(alog),
                  full(dexp), full(ng)],
        out_specs=row(d_inner),
        out_shape=jax.ShapeDtypeStruct((t, d_inner), BF16),
        scratch_shapes=[pltpu.VMEM((8 + q, xbc_w), F32), pltpu.VMEM((q, xbc_w), F32),
                        pltpu.VMEM((q, d_inner), F32), pltpu.VMEM((n_heads // 2, n_state, LANES), F32)],
        compiler_params=_cparams(("arbitrary", "arbitrary"), 32),
        name="ssd",
    )(z, xbc, dtr, conv_w, cb, dtb, alog, dexp, ng)


def _pool_kernel(u_ref, w_ref, sc_ref, o_ref, buf, *, r, gd):
    c = pl.program_id(1)
    hist = 16
    width = buf.shape[1]

    @pl.when(c == 0)
    def _():
        buf[0:hist, :] = jnp.zeros((hist, width), F32)

    buf[hist:hist + r, :] = u_ref[...]
    pos = c * r + lax.broadcasted_iota(jnp.int32, (r, 1), 0) - PAD_ROWS
    for gi, win in enumerate(POOL_WINDOWS):
        sl = slice(gi * gd, (gi + 1) * gd)
        u = buf[hist:hist + r, sl]
        s = u
        for j in range(1, win):
            s = s + buf[hist - j:hist - j + r, sl]
        cnt = jnp.clip(pos + 1, 1, win).astype(F32)
        mixed = s / cnt - u
        y = jnp.dot(mixed.astype(BF16), w_ref[gi].astype(BF16), preferred_element_type=F32)
        o_ref[:, sl] = (y * sc_ref[:, sl]).astype(BF16)
    buf[0:hist, :] = buf[r:r + hist, :]


def _pool(pc, pool_w, scale, nb, lp):
    t = pc.shape[0]
    ng, gd, _ = pool_w.shape
    width = ng * gd
    assert ng == len(POOL_WINDOWS) and gd % LANES == 0
    r = _pick(lp, (384, 192, 128, 64))
    nc = lp // r
    return pl.pallas_call(
        functools.partial(_pool_kernel, r=r, gd=gd),
        grid=(nb, nc),
        in_specs=[pl.BlockSpec((r, width), lambda b, c: (b * nc + c, 0)),
                  pl.BlockSpec(pool_w.shape, lambda b, c: (0, 0, 0)),
                  pl.BlockSpec((1, width), lambda b, c: (0, 0))],
        out_specs=pl.BlockSpec((r, width), lambda b, c: (b * nc + c, 0)),
        out_shape=jax.ShapeDtypeStruct((t, width), BF16),
        scratch_shapes=[pltpu.VMEM((16 + r, width), F32)],
        compiler_params=_cparams(("arbitrary", "arbitrary"), 32),
        name="pool",
    )(pc, pool_w, scale.reshape(1, width))


def _conf_kernel(a_ref, g_ref, w_ref, b_ref, lg_ref, lb_ref, o_ref, vbuf, cv, ubuf, *, r):
    c = pl.program_id(1)
    hist = 32
    sub = 8
    width = vbuf.shape[1]
    kc = w_ref.shape[0]

    @pl.when(c == 0)
    def _():
        vbuf[0:hist, :] = jnp.zeros((hist, width), F32)

    vbuf[hist:hist + r, :] = a_ref[...] * _sigmoid(g_ref[...])
    ext = r + sub
    cw = ubuf.shape[2]
    for jc in range(width // cw):
        sl = slice(jc * cw, (jc + 1) * cw)
        acc = None
        for rr in range(sub):
            u = None
            for j in range((kc - 1 - rr) // sub + 1):
                k = kc - 1 - (sub * j + rr)
                lo = hist - sub - sub * j
                term = w_ref[k:k + 1, sl] * vbuf[lo:lo + ext, sl]
                u = term if u is None else u + term
            if rr == 0:
                acc = b_ref[:, sl] + u[sub:sub + r]
            else:
                ubuf[rr - 1] = u
                acc = acc + ubuf[rr - 1, sub - rr:sub - rr + r, :]
        cv[:, sl] = acc
    vbuf[0:hist, :] = vbuf[r:r + hist, :]
    v = _ln(cv[...], lg_ref[...], lb_ref[...])
    o_ref[...] = (v * _sigmoid(v)).astype(BF16)


def _conf(pc, col_blk, dw_w, dw_b, ln_g, ln_b, nb, lp):
    t = pc.shape[0]
    kc, width = dw_w.shape
    assert kc <= 33 and width % 256 == 0
    r = _pick(lp, (192, 128, 64))
    nc = lp // r
    vec = lambda: pl.BlockSpec((1, width), lambda b, c: (0, 0))
    return pl.pallas_call(
        functools.partial(_conf_kernel, r=r),
        grid=(nb, nc),
        in_specs=[pl.BlockSpec((r, width), lambda b, c: (b * nc + c, col_blk)),
                  pl.BlockSpec((r, width), lambda b, c: (b * nc + c, col_blk + 1)),
                  pl.BlockSpec((kc, width), lambda b, c: (0, 0)), vec(), vec(), vec()],
        out_specs=pl.BlockSpec((r, width), lambda b, c: (b * nc + c, 0)),
        out_shape=jax.ShapeDtypeStruct((t, width), BF16),
        scratch_shapes=[pltpu.VMEM((32 + r, width), F32), pltpu.VMEM((r, width), F32),
                        pltpu.VMEM((7, r + 8, LANES), F32)],
        compiler_params=_cparams(("arbitrary", "arbitrary"), 32),
        name="conf",
    )(pc, pc, dw_w, dw_b.reshape(1, width), ln_g.reshape(1, width), ln_b.reshape(1, width))


def _merge_kernel(hb_ref, f0_ref, f1_ref, f2_ref, g0_ref, g1_ref, g2_ref, gb_ref, p0_ref, p1_ref, p2_ref,
                  o_ref, gs, s0, s1, s2):
    @pl.when(pl.program_id(1) == 0)
    def _():
        gs[0] = g0_ref[...].astype(BF16)
        gs[1] = g1_ref[...].astype(BF16)
        gs[2] = g2_ref[...].astype(BF16)
        s0[...] = p0_ref[...].astype(BF16)
        s1[...] = p1_ref[...].astype(BF16)
        s2[...] = p2_ref[...].astype(BF16)

    hb = hb_ref[...]
    acc = None
    for j, (f_ref, s_ref) in enumerate(((f0_ref, s0), (f1_ref, s1), (f2_ref, s2))):
        gate = _sigmoid(jnp.dot(hb, gs[j], preferred_element_type=F32) + gb_ref[j:j + 1, :])
        term = gate * jnp.dot(f_ref[...], s_ref[...], preferred_element_type=F32)
        acc = term if acc is None else acc + term
    o_ref[...] = acc.astype(BF16)


def _merge(hb, f_ssd, f_pool, f_conf, gate_w, gate_b, ssd_proj, pool_proj, conf_proj, layer):
    t, d = hb.shape
    bn = _pick(d, (512, 256, 128))
    bm = _pick(t, (528, 384, 192, 128, 64, 8))
    nbn = d // bn
    once = pl.Buffered(1)
    rowspec = lambda a: pl.BlockSpec((bm, a.shape[1]), lambda n, m: (m, 0))
    gspec = lambda j: pl.BlockSpec((None, d, bn), lambda n, m: (layer, 0, j * nbn + n), pipeline_mode=once)
    pspec = lambda w: pl.BlockSpec((None, w.shape[1], bn), lambda n, m: (layer, 0, n))
    gb = gate_b[layer].reshape(3, d)
    return pl.pallas_call(
        _merge_kernel,
        grid=(nbn, t // bm),
        in_specs=[rowspec(hb), rowspec(f_ssd), rowspec(f_pool), rowspec(f_conf),
                  gspec(0), gspec(1), gspec(2), pl.BlockSpec((3, bn), lambda n, m: (0, n)),
                  pspec(ssd_proj), pspec(pool_proj), pspec(conf_proj)],
        out_specs=pl.BlockSpec((bm, bn), lambda n, m: (m, n)),
        out_shape=jax.ShapeDtypeStruct((t, d), BF16),
        scratch_shapes=[pltpu.VMEM((3, d, bn), BF16), pltpu.VMEM((ssd_proj.shape[1], bn), BF16),
                        pltpu.VMEM((pool_proj.shape[1], bn), BF16), pltpu.VMEM((conf_proj.shape[1], bn), BF16)],
        compiler_params=_cparams(("arbitrary", "arbitrary"), 58),
        name="merge",
    )(hb, f_ssd, f_pool, f_conf, gate_w, gate_w, gate_w, gb, ssd_proj, pool_proj, conf_proj)


def _route_rows(h, w, bias, run, real, n_groups, per_group):
    bm = h.shape[0]
    h_hi = h.astype(BF16)
    h_lo = (h - h_hi.astype(F32)).astype(BF16)
    w_hi = w.astype(BF16)
    w_lo = (w - w_hi.astype(F32)).astype(BF16)
    logits = (jnp.dot(h_hi, w_hi, preferred_element_type=F32)
              + (jnp.dot(h_hi, w_lo, preferred_element_type=F32)
                 + jnp.dot(h_lo, w_hi, preferred_element_type=F32))) + bias
    lane = lax.broadcasted_iota(jnp.int32, logits.shape, 1).astype(F32)

    def first_max(vals):
        m = jnp.max(vals, axis=1, keepdims=True)
        idx = jnp.min(jnp.where(vals == m, lane, float(LANES)), axis=1, keepdims=True)
        return m, idx

    gmask = lane < n_groups
    gmax, gsel = first_max(jnp.where(gmask, logits, NEG_BIG))
    gsum = jnp.sum(jnp.where(gmask, jnp.exp(logits - gmax), 0.0), axis=1, keepdims=True)
    p_group = 1.0 / gsum
    lo = n_groups + per_group * gsel
    el = jnp.where(jnp.logical_and(lane >= lo, lane < lo + per_group), logits, NEG_BIG)
    m1, i1 = first_max(el)
    m2, i2 = first_max(jnp.where(lane == i1, NEG_BIG, el))
    ratio = jnp.exp(m2 - m1)
    w1 = p_group / (1.0 + ratio)
    w2 = w1 * ratio
    sentinel = float(n_groups * per_group)
    e1 = jnp.where(real, i1 - n_groups, sentinel)
    e2 = jnp.where(real, i2 - n_groups, sentinel)
    w1 = jnp.where(real, w1, 0.0)
    w2 = jnp.where(real, w2, 0.0)
    hit1 = jnp.logical_and(lane == i1, real)
    hit2 = jnp.logical_and(lane == i2, real)
    onehot = jnp.where(jnp.logical_or(hit1, hit2), 1.0, 0.0)
    li = lax.broadcasted_iota(jnp.int32, (bm, bm), 0)
    si = lax.broadcasted_iota(jnp.int32, (bm, bm), 1)
    before = jnp.where(li > si, 1.0, 0.0).astype(BF16)
    seen = jnp.dot(before, onehot.astype(BF16), preferred_element_type=F32) + run[...]
    r1 = jnp.sum(jnp.where(hit1, seen, 0.0), axis=1, keepdims=True)
    r2 = jnp.sum(jnp.where(hit2, seen, 0.0), axis=1, keepdims=True)
    run[...] = run[...] + jnp.sum(onehot, axis=0, keepdims=True)
    out = jnp.zeros(logits.shape, F32)
    for k, v in enumerate((w1, w2, e1, e2, r1, r2)):
        out = jnp.where(lane == k, v, out)
    return out


def _moe_kernel(be_ref, nxt1_ref, nxt2_ref, par_ref, tot_ref, tok_ref, x_hbm, wg_hbm, wu_hbm, wd_hbm, o_ref,
                wg_f, wu_f, wd_f, wg_s, wu_s, wd_s, sems, xbuf, xsems, *, layer):
    i = pl.program_id(0)
    total = tot_ref[0]
    blk = xbuf.shape[1]

    def row_copy(block, r, buf):
        tok = tok_ref[block * blk + r]
        return pltpu.make_async_copy(x_hbm.at[pl.ds(tok, 1)], xbuf.at[buf, pl.ds(r, 1)], xsems.at[buf])

    def start_rows(block, buf):
        def body(rr, carry):
            for u in range(8):
                row_copy(block, 8 * rr + u, buf).start(priority=u % 2)
            return carry
        lax.fori_loop(0, blk // 8, body, 0)

    def wait_rows(buf):
        pltpu.make_async_copy(xbuf.at[buf], xbuf.at[buf], xsems.at[buf]).wait()

    @pl.when(jnp.logical_and(i == 0, total > 0))
    def _():
        start_rows(0, 0)

    ic =jnp.maximum(jnp.minimum(i, total - 1), 0)
    e = be_ref[ic]
    e_prev = be_ref[jnp.maximum(ic - 1, 0)]
    active = i < total
    first_of_run = jnp.logical_and(active, jnp.logical_or(i == 0, e != e_prev))
    par = par_ref[ic]

    def weight_copies(expert, st):
        return (pltpu.make_async_copy(wg_hbm.at[layer, expert], wg_f.at[st], sems.at[st, 0]),
                pltpu.make_async_copy(wu_hbm.at[layer, expert], wu_f.at[st], sems.at[st, 1]),
                pltpu.make_async_copy(wd_hbm.at[layer, expert], wd_f.at[st], sems.at[st, 2]))

    @pl.when(jnp.logical_and(active, i == 0))
    def _():
        for cp in weight_copies(e, 0):
            cp.start()
        nxt1 = nxt1_ref[ic]

        @pl.when(nxt1 >= 0)
        def _():
            for cp in weight_copies(nxt1, 1):
                cp.start()

    @pl.when(first_of_run)
    def _():
        for cp, src, dst in zip(weight_copies(e, par), (wg_f, wu_f, wd_f), (wg_s, wu_s, wd_s)):
            cp.wait()
            dst[...] = src[par].astype(BF16)
        nxt2 = nxt2_ref[ic]

        @pl.when(nxt2 >= 0)
        def _():
            for cp in weight_copies(nxt2, par):
                cp.start()

    def ffn_block(request_next):
        wait_rows(i % 2)
        x = _unpack_halves(xbuf[i % 2]).astype(BF16)
        if request_next:
            for r in range(blk):
                row_copy(i + 1, r, (i + 1) % 2).start(priority=r % 2)
        gte = jnp.dot(x, wg_s[...], preferred_element_type=F32)
        up = jnp.dot(x, wu_s[...], preferred_element_type=F32)
        hid = (gte * _sigmoid(gte)) * up
        o_ref[...] = _pack_halves(jnp.dot(hid.astype(BF16), wd_s[...], preferred_element_type=F32))

    @pl.when(i + 1 < total)
    def _():
        ffn_block(True)

    @pl.when(jnp.logical_and(active, i + 1 >= total))
    def _():
        ffn_block(False)

    @pl.when(jnp.logical_not(active))
    def _():
        o_ref[...] = jnp.zeros(o_ref.shape, F32)


def _moe_ffn(x_packed, slot_tok, plan, w_gate, w_up, w_down, layer):
    d = x_packed.shape[1]
    n_slots = slot_tok.shape[0]
    hid = w_gate.shape[-1]
    assert w_gate.shape[-2] == 2 * d
    n_blocks = n_slots // MOE_BLOCK

    hbm = pl.BlockSpec(memory_space=pl.ANY)
    grid_spec = pltpu.PrefetchScalarGridSpec(
        num_scalar_prefetch=6,
        grid=(n_blocks,),
        in_specs=[hbm, hbm, hbm, hbm],
        out_specs=pl.BlockSpec((MOE_BLOCK, d), lambda i, *_: (i, 0)),
        scratch_shapes=[pltpu.VMEM((2, 2 * d, hid), F32), pltpu.VMEM((2, 2 * d, hid), F32),
                        pltpu.VMEM((2, hid, 2 * d), F32),
                        pltpu.VMEM((2 * d, hid), BF16), pltpu.VMEM((2 * d, hid), BF16),
                        pltpu.VMEM((hid, 2 * d), BF16), pltpu.SemaphoreType.DMA((2, 3)),
                        pltpu.VMEM((2, MOE_BLOCK, d), F32), pltpu.SemaphoreType.DMA((2,))],
    )
    return pl.pallas_call(
        functools.partial(_moe_kernel, layer=layer),
        grid_spec=grid_spec,
        out_shape=jax.ShapeDtypeStruct((n_slots, d), F32),
        compiler_params=_cparams(("arbitrary",), 58),
        name="moe_ffn",
    )(*plan, slot_tok, x_packed, w_gate, w_up, w_down)


def _dispatch(ew_t, cnt, n_groups, n_experts):
    t = ew_t.shape[1]
    eid = ew_t[2:4].astype(jnp.int32).reshape(-1)
    rank = ew_t[4:6].astype(jnp.int32).reshape(-1)
    counts = cnt[0, n_groups:n_groups + n_experts].astype(jnp.int32)
    n_assign = eid.shape[0]
    tok = jnp.tile(jnp.arange(t, dtype=jnp.int32), 2)
    padded = (counts + MOE_BLOCK - 1) // MOE_BLOCK * MOE_BLOCK
    pend = jnp.cumsum(padded)
    pstart = pend - padded
    n_blocks = n_assign // MOE_BLOCK + n_experts
    n_slots = n_blocks * MOE_BLOCK
    valid = eid < n_experts
    slot = jnp.where(valid, pstart[jnp.minimum(eid, n_experts - 1)] + rank,
                     n_slots + jnp.arange(n_assign, dtype=jnp.int32))
    slot_tok = (jnp.arange(n_slots, dtype=jnp.int32) % t).at[slot].set(tok, mode="drop", unique_indices=True)
    first_row = jnp.arange(n_blocks, dtype=jnp.int32) * MOE_BLOCK
    block_expert = jnp.minimum(jnp.sum((pend[None, :] <= first_row[:, None]).astype(jnp.int32), axis=1),
                               n_experts - 1)
    total_blocks = (pend[-1] // MOE_BLOCK).astype(jnp.int32).reshape(1)
    ids = jnp.arange(n_experts, dtype=jnp.int32)
    has = counts > 0
    later = jnp.where(has, ids, n_experts)
    nxt1 = lax.cummin(jnp.concatenate([later[1:], jnp.full((1,), n_experts, jnp.int32)]), reverse=True)
    nxt2 = jnp.concatenate([nxt1, jnp.full((1,), n_experts, jnp.int32)])[nxt1]
    nxt1 = jnp.where(nxt1 >= n_experts, -1, nxt1)
    nxt2 = jnp.where(nxt2 >= n_experts, -1, nxt2)
    parity = (jnp.cumsum(has.astype(jnp.int32)) - 1) % 2
    plan = (block_expert, nxt1[block_expert], nxt2[block_expert], parity[block_expert], total_blocks)
    slot_c = jnp.where(valid, slot, 0).reshape(2, t)
    return slot_tok, plan, slot_c


def kernel(x, meta_tokens, ln_emb_g, ln_emb_b, w_in, ssd_conv_w, ssd_conv_b, ssd_dt_bias, ssd_a_log, ssd_d,
           ssd_norm_g, ssd_proj, pool_w, pool_scale, pool_proj, conf_dw_w, conf_dw_b, conf_ln_g, conf_ln_b,
           conf_proj, gate_w, gate_b, w_out, ln1_g, ln1_b, router_group_w, router_group_b, router_expert_w,
           router_expert_b, exp_w_gate, exp_w_up, exp_w_down, ln2_g, ln2_b):
    nb, seq, d = x.shape
    n_meta = meta_tokens.shape[0]
    depth = w_in.shape[0]
    lp = PAD_ROWS + n_meta + seq
    t = nb * lp
    alpha = (2.0 * depth) ** 0.25
    d_inner = ssd_norm_g.shape[-1]
    xbc_w = ssd_conv_w.shape[-1]
    n_heads = ssd_a_log.shape[-1]
    pool_width = pool_scale.shape[-1]
    conf_width = conf_dw_b.shape[-1]
    n_groups = router_group_w.shape[-1]
    n_experts = router_expert_w.shape[-1]
    per_group = n_experts // n_groups
    assert (PAD_ROWS + n_meta) % SSD_CHUNK == 0 and lp % SSD_CHUNK == 0
    assert n_groups + n_experts <= LANES and n_heads <= LANES

    h, hb = _embed_ln(x, meta_tokens.astype(x.dtype), ln_emb_g, ln_emb_b, lp)

    c_dt = d_inner + xbc_w
    c_pc = c_dt + n_heads
    assert c_dt % LANES == 0 and c_dt + LANES <= w_in.shape[-1]
    w_in_t = jnp.swapaxes(w_in, 1, 2)
    rw = jnp.concatenate([router_group_w, router_expert_w], axis=-1)
    rw = jnp.pad(rw, ((0, 0), (0, 0), (0, LANES - rw.shape[-1])))
    rb = jnp.concatenate([router_group_b, router_expert_b], axis=-1)
    rb = jnp.pad(rb, ((0, 0), (0, LANES - rb.shape[-1]))).reshape(depth, 1, LANES)

    for i in range(depth):
        z = _mm(hb, w_in_t, i, 0, d_inner)
        xbc = _mm(hb, w_in_t, i, d_inner, xbc_w)
        dtr = _mm(hb, w_in_t, i, c_dt, LANES)
        pc = _mm(hb, w_in_t, i, c_pc, pool_width + 2 * conf_width)
        f_ssd = _ssd(z, xbc, dtr, ssd_conv_w[i], ssd_conv_b[i], ssd_dt_bias[i], ssd_a_log[i], ssd_d[i],
                     ssd_norm_g[i], nb, lp)
        f_pool = _pool(pc, pool_w[i], pool_scale[i], nb, lp)
        assert pool_width % conf_width == 0
        f_conf = _conf(pc, pool_width // conf_width, conf_dw_w[i], conf_dw_b[i], conf_ln_g[i], conf_ln_b[i], nb, lp)
        merged = _merge(hb, f_ssd, f_pool, f_conf, gate_w, gate_b, ssd_proj, pool_proj, conf_proj, i)
        h, h_packed, ew, ew_t, cnt = _mm_ln(merged, w_out, i, h, ln1_g[i], ln1_b[i], rw[i], rb[i], lp, alpha,
                                            n_groups, per_group)
        slot_tok, plan, slot_c = _dispatch(ew_t, cnt, n_groups, n_experts)
        out = _moe_ffn(h_packed, slot_tok, plan, exp_w_gate, exp_w_up, exp_w_down, i)
        g0 = out.at[slot_c[0]].get(mode="promise_in_bounds")
        g1 = out.at[slot_c[1]].get(mode="promise_in_bounds")
        if i + 1 < depth:
            h, hb = _combine_ln(h, g0, g1, ew, ln2_g[i], ln2_b[i], lp, alpha)
        else:
            h = _final_ln(h, g0, g1, ew, ln2_g[i], ln2_b[i], nb, lp, seq, alpha)

    return h.reshape(nb, seq, d)
```
